```python
import jax, jax.numpy as jnp
from jax import lax
import numpy as np

D_MODEL = 1024
BATCH = 8
SEQ = 2048
DEPTH = 4
DEC_BATCH = 8
DEC_SEQ = 4096
PAST_LEN = 128

GRID_W = 64
N_META = 16
D_HEAD = 64
D_ATT = D_MODEL // 2
N_ATT_HEADS = D_ATT // D_HEAD
D_FFT = D_MODEL // 4
N_FFT_HEADS = 4
D_FFT_HEAD = D_FFT // N_FFT_HEADS
D_POOL = D_MODEL // 4
POOL_WINDOWS = (2, 4, 8, 16)
N_POOL_GROUPS = len(POOL_WINDOWS)
D_POOL_GROUP = D_POOL // N_POOL_GROUPS
D_MIX = D_ATT + D_FFT + D_POOL
D_IN = 3 * D_ATT + D_FFT + D_POOL
NA_ROWS = 8
NA_COLS = 16
D_FF = 2816
N_EXPERTS = 8
TOP_K = 2
D_FF_EXPERT = 3584
EPS = 1e-6

kernel_name = "hybrid_natten_fnet_pool_moe_encoder"


def _rmsnorm(x, g):
    xf = x.astype(jnp.float32)
    y = xf * lax.rsqrt(jnp.mean(xf * xf, axis=-1, keepdims=True) + EPS)
    return (y * g.astype(jnp.float32)).astype(x.dtype)


def _neighbourhood_attention(q, k, v, rel_bias, meta_bias):
    B, L, H, dh = q.shape
    T = L - N_META
    rows = T // GRID_W
    kr = min(NA_ROWS, rows)
    scale = dh ** -0.5
    dt = q.dtype

    def grid(a):
        return a[:, N_META:].reshape(B, rows, GRID_W, H, dh).transpose(0, 3, 1, 2, 4)

    qg, kg, vg = grid(q), grid(k), grid(v)
    qm = q[:, :N_META].transpose(0, 2, 1, 3)
    km = k[:, :N_META].transpose(0, 2, 1, 3)
    vm = v[:, :N_META].transpose(0, 2, 1, 3)

    col = np.arange(GRID_W)
    col_start = np.clip(col - NA_COLS // 2, 0, GRID_W - NA_COLS)
    col_idx = col_start[:, None] + np.arange(NA_COLS)[None, :]
    dc_idx = col_idx - col[:, None] + (NA_COLS - 1)
    col_bias = rel_bias.astype(jnp.float32)[:, :, dc_idx]
    meta_b = meta_bias.astype(jnp.float32)[None, :, None, :]

    def row_step(args):
        q_r, r = args
        rs = jnp.clip(r - kr // 2, 0, rows - kr)
        k_blk = lax.dynamic_slice_in_dim(kg, rs, kr, axis=2)
        v_blk = lax.dynamic_slice_in_dim(vg, rs, kr, axis=2)
        k_win = jnp.take(k_blk, col_idx, axis=3)
        v_win = jnp.take(v_blk, col_idx, axis=3)
        dr_idx = rs + jnp.arange(kr) - r + (NA_ROWS - 1)
        bias = jnp.take(col_bias, dr_idx, axis=1).transpose(0, 2, 1, 3)
        s_loc = jnp.einsum('bhcd,bhicjd->bhcij', q_r, k_win).astype(jnp.float32) * scale + bias[None]
        s_meta = jnp.einsum('bhcd,bhmd->bhcm', q_r, km).astype(jnp.float32) * scale + meta_b
        s = jnp.concatenate([s_loc.reshape(B, H, GRID_W, kr * NA_COLS), s_meta], axis=-1)
        p = jax.nn.softmax(s, axis=-1).astype(dt)
        p_loc = p[..., :kr * NA_COLS].reshape(B, H, GRID_W, kr, NA_COLS)
        p_meta = p[..., kr * NA_COLS:]
        return (jnp.einsum('bhcij,bhicjd->bhcd', p_loc, v_win)
                + jnp.einsum('bhcm,bhmd->bhcd', p_meta, vm))

    o_grid = lax.map(row_step, (qg.transpose(2, 0, 1, 3, 4), jnp.arange(rows, dtype=jnp.int32)))
    o_real = o_grid.transpose(1, 0, 3, 2, 4).reshape(B, T, H * dh)
    s_mm = jnp.einsum('bhqd,bhmd->bhqm', qm, km).astype(jnp.float32) * scale + meta_b
    p_mm = jax.nn.softmax(s_mm, axis=-1).astype(dt)
    o_meta = jnp.einsum('bhqm,bhmd->bhqd', p_mm, vm).transpose(0, 2, 1, 3).reshape(B, N_META, H * dh)
    return jnp.concatenate([o_meta, o_real], axis=1)


def _fourier_mix(u, w_lin):
    B, L, _ = u.shape
    uh = u.astype(jnp.float32).reshape(B, L, N_FFT_HEADS, D_FFT_HEAD).transpose(0, 2, 1, 3)
    f = jnp.real(jnp.fft.fft2(uh, axes=(-2, -1), norm='ortho')).astype(u.dtype)
    return jnp.einsum('bglc,gce->blge', f, w_lin).reshape(B, L, D_FFT)


def _pool_mix(u, w_lin, ch_scale):
    B, L, _ = u.shape
    uf = u.astype(jnp.float32).reshape(B, L, N_POOL_GROUPS, D_POOL_GROUP)
    cs = jnp.concatenate([jnp.zeros((B, 1, N_POOL_GROUPS, D_POOL_GROUP), jnp.float32),
                          jnp.cumsum(uf, axis=1)], axis=1)
    t = np.arange(L)
    outs = []
    for g, w in enumerate(POOL_WINDOWS):
        lo = np.clip(t - w // 2, 0, L)
        hi = np.clip(t + w // 2, 0, L)
        cnt = jnp.asarray(hi - lo, dtype=jnp.float32)[None, :, None]
        mean = (cs[:, hi, g] - cs[:, lo, g]) / cnt
        outs.append(mean - uf[:, :, g])
    p = jnp.stack(outs, axis=2).astype(u.dtype)
    y = jnp.einsum('blgc,gce->blge', p, w_lin).reshape(B, L, D_POOL)
    return y * ch_scale


def _swiglu(x, w_gate, w_up, w_down):
    return (jax.nn.silu(x @ w_gate) * (x @ w_up)) @ w_down


def _moe(x, w_router, w_gate, w_up, w_down):
    B, L, D = x.shape
    xt = x.reshape(B * L, D)
    logits = (xt @ w_router).astype(jnp.float32)
    top_v, top_i = lax.top_k(logits, TOP_K)
    gates = jax.nn.softmax(top_v, axis=-1)
    combine = jnp.sum(jax.nn.one_hot(top_i, N_EXPERTS, dtype=jnp.float32) * gates[..., None], axis=1)
    combine = combine.astype(x.dtype)
    y = jnp.zeros_like(xt)
    for e in range(N_EXPERTS):
        y = y + combine[:, e:e + 1] * _swiglu(xt, w_gate[e], w_up[e], w_down[e])
    return y.reshape(B, L, D)


def _trunk(x, meta_tokens, norm_mix, w_in, w_fft, w_pool, pool_scale, rel_bias, meta_bias,
           norm_groups, w_out, norm_ffn, w_ff_gate, w_ff_up, w_ff_down,
           w_router, w_exp_gate, w_exp_up, w_exp_down, norm_final):
    B, T, D = x.shape
    meta = jnp.broadcast_to(meta_tokens.astype(x.dtype)[None], (B, N_META, D))
    h = jnp.concatenate([meta, x], axis=1)
    L = N_META + T
    for i in range(DEPTH):
        hn = _rmsnorm(h, norm_mix[i])
        z = hn @ w_in[i]
        q = z[..., :D_ATT].reshape(B, L, N_ATT_HEADS, D_HEAD)
        k = z[..., D_ATT:2 * D_ATT].reshape(B, L, N_ATT_HEADS, D_HEAD)
        v = z[..., 2 * D_ATT:3 * D_ATT].reshape(B, L, N_ATT_HEADS, D_HEAD)
        u_f = z[..., 3 * D_ATT:3 * D_ATT + D_FFT]
        u_p = z[..., 3 * D_ATT + D_FFT:]
        a = _neighbourhood_attention(q, k, v, rel_bias[i], meta_bias[i])
        f = _fourier_mix(u_f, w_fft[i])
        p = _pool_mix(u_p, w_pool[i], pool_scale[i])
        gn = norm_groups[i]
        m = jnp.concatenate([_rmsnorm(a, gn[:D_ATT]),
                             _rmsnorm(f, gn[D_ATT:D_ATT + D_FFT]),
                             _rmsnorm(p, gn[D_ATT + D_FFT:])], axis=-1)
        h = h + m @ w_out[i]
        hn = _rmsnorm(h, norm_ffn[i])
        if i % 2 == 0:
            j = i // 2
            h = h + _swiglu(hn, w_ff_gate[j], w_ff_up[j], w_ff_down[j])
        else:
            j = i // 2
            h = h + _moe(hn, w_router[j], w_exp_gate[j], w_exp_up[j], w_exp_down[j])
    return _rmsnorm(h, norm_final)[:, N_META:]


def setup_inputs(seed: int = 0) -> dict:
    key = jax.random.key(seed)
    ks = jax.random.split(key, 24)
    n_dense = (DEPTH + 1) // 2
    n_moe = DEPTH // 2
    f32 = jnp.float32

    def nrm(k, shape, s):
        return jax.random.normal(k, shape, f32) * s

    def gain(k, shape):
        return 1.0 + 0.02 * jax.random.normal(k, shape, f32)

    return {
        'x_prompt': jax.random.normal(ks[0], (BATCH, SEQ, D_MODEL), f32),
        'x_sample': jax.random.normal(ks[1], (DEC_BATCH, DEC_SEQ, D_MODEL), f32),
        'meta_tokens': nrm(ks[2], (N_META, D_MODEL), 1.0),
        'norm_mix': gain(ks[3], (DEPTH, D_MODEL)),
        'w_in': nrm(ks[4], (DEPTH, D_MODEL, D_IN), D_MODEL ** -0.5),
        'w_fft': nrm(ks[5], (DEPTH, N_FFT_HEADS, D_FFT_HEAD, D_FFT_HEAD), D_FFT_HEAD ** -0.5),
        'w_pool': nrm(ks[6], (DEPTH, N_POOL_GROUPS, D_POOL_GROUP, D_POOL_GROUP), D_POOL_GROUP ** -0.5),
        'pool_scale': 1.0 + 0.1 * jax.random.normal(ks[7], (DEPTH, D_POOL), f32),
        'rel_bias': nrm(ks[8], (DEPTH, N_ATT_HEADS, 2 * NA_ROWS - 1, 2 * NA_COLS - 1), 0.1),
        'meta_bias': nrm(ks[9], (DEPTH, N_ATT_HEADS, N_META), 0.1),
        'norm_groups': gain(ks[10], (DEPTH, D_MIX)),
        'w_out': nrm(ks[11], (DEPTH, D_MIX, D_MODEL), D_MIX ** -0.5),
        'norm_ffn': gain(ks[12], (DEPTH, D_MODEL)),
        'w_ff_gate': nrm(ks[13], (n_dense, D_MODEL, D_FF), D_MODEL ** -0.5),
        'w_ff_up': nrm(ks[14], (n_dense, D_MODEL, D_FF), D_MODEL ** -0.5),
        'w_ff_down': nrm(ks[15], (n_dense, D_FF, D_MODEL), D_FF ** -0.5),
        'w_router': nrm(ks[16], (n_moe, D_MODEL, N_EXPERTS), D_MODEL ** -0.5),
        'w_exp_gate': nrm(ks[17], (n_moe, N_EXPERTS, D_MODEL, D_FF_EXPERT), D_MODEL ** -0.5),
        'w_exp_up': nrm(ks[18], (n_moe, N_EXPERTS, D_MODEL, D_FF_EXPERT), D_MODEL ** -0.5),
        'w_exp_down': nrm(ks[19], (n_moe, N_EXPERTS, D_FF_EXPERT, D_MODEL), D_FF_EXPERT ** -0.5),
        'norm_final': gain(ks[20], (D_MODEL,)),
    }


def reference(x_prompt, x_sample, meta_tokens, norm_mix, w_in, w_fft, w_pool, pool_scale,
              rel_bias, meta_bias, norm_groups, w_out, norm_ffn, w_ff_gate, w_ff_up, w_ff_down,
              w_router, w_exp_gate, w_exp_up, w_exp_down, norm_final):
    y_prompt = _trunk(x_prompt, meta_tokens, norm_mix, w_in, w_fft, w_pool, pool_scale, rel_bias,
                      meta_bias, norm_groups, w_out, norm_ffn, w_ff_gate, w_ff_up, w_ff_down,
                      w_router, w_exp_gate, w_exp_up, w_exp_down, norm_final)
    y_sample = _trunk(x_sample, meta_tokens, norm_mix, w_in, w_fft, w_pool, pool_scale, rel_bias,
                      meta_bias, norm_groups, w_out, norm_ffn, w_ff_gate, w_ff_up, w_ff_down,
                      w_router, w_exp_gate, w_exp_up, w_exp_down, norm_final)
    return (y_prompt, y_sample)
```

```python
import functools
import math

import numpy as np
import jax
import jax.numpy as jnp
from jax import lax
from jax.experimental import pallas as pl
from jax.experimental.pallas import tpu as pltpu

D_MODEL = 1024
N_META = 16
GRID_W = 64
D_HEAD = 64
D_ATT = 512
N_ATT_HEADS = 8
D_FFT = 256
N_FFT_HEADS = 4
D_FFT_HEAD = 64
D_POOL = 256
POOL_WINDOWS = (2, 4, 8, 16)
D_POOL_GROUP = 64
D_IN = 2048
NA_ROWS = 8
NA_COLS = 16
N_EXPERTS = 8
EPS = 1e-6

LANE = 128
NEG_BIG = -1e30
VMEM_LIMIT = 56 * 1024 * 1024

F32 = jnp.float32
BF16 = jnp.bfloat16


def _cparams(n_axes, vmem=VMEM_LIMIT):
    return pltpu.CompilerParams(dimension_semantics=("arbitrary",) * n_axes,
                                vmem_limit_bytes=vmem)


def _rms(x, g):
    return x * lax.rsqrt(jnp.mean(x * x, axis=-1, keepdims=True) + EPS) * g


def _norm_matmul_kernel(x_ref, g_ref, w_ref, o_ref, xn_ref):
    @pl.when(pl.program_id(1) == 0)
    def _():
        xn_ref[...] = _rms(x_ref[...], g_ref[...]).astype(BF16)

    o_ref[...] = jnp.dot(xn_ref[...], w_ref[...], preferred_element_type=F32).astype(o_ref.dtype)


def _norm_matmul(x, g, w, *, tm=512, tn=1024):
    n, d = x.shape
    n_out = w.shape[1]
    return pl.pallas_call(
        _norm_matmul_kernel,
        out_shape=jax.ShapeDtypeStruct((n, n_out), BF16),
        grid=(n // tm, n_out // tn),
        in_specs=[pl.BlockSpec((tm, d), lambda i, j: (i, 0)),
                  pl.BlockSpec((1, d), lambda i, j: (0, 0)),
                  pl.BlockSpec((d, tn), lambda i, j: (0, j))],
        out_specs=pl.BlockSpec((tm, tn), lambda i, j: (i, j)),
        scratch_shapes=[pltpu.VMEM((tm, d), BF16)],
        compiler_params=_cparams(2),
        name="norm_matmul",
    )(x, g.reshape(1, d), w)


def _matmul_kernel(x_ref, w_ref, o_ref):
    o_ref[...] = jnp.dot(x_ref[...], w_ref[...], preferred_element_type=F32).astype(o_ref.dtype)


def _matmul_cols(x, w, col_block, *, tm=1024):
    n = x.shape[0]
    k, n_out = w.shape
    return pl.pallas_call(
        _matmul_kernel,
        out_shape=jax.ShapeDtypeStruct((n, n_out), BF16),
        grid=(n // tm,),
        in_specs=[pl.BlockSpec((tm, k), lambda i: (i, col_block)),
                  pl.BlockSpec((k, n_out), lambda i: (0, 0))],
        out_specs=pl.BlockSpec((tm, n_out), lambda i: (i, 0)),
        compiler_params=_cparams(1),
        name="channel_dft",
    )(x, w)


def _attn_kernel(q_ref, k_ref, v_ref, bias_ref, mb_ref, o_ref, *, rows, seq_len):
    lp = o_ref.shape[1]
    lane = lax.broadcasted_iota(jnp.int32, (1, LANE), 1)
    head0 = lane < D_HEAD
    nt = (((1,), (1,)), ((), ()))

    km = k_ref[0, 0:N_META, :]
    vm = v_ref[0, 0:N_META, :]
    mb = mb_ref[0]

    def stack(q):
        q = q * jnp.asarray(D_HEAD ** -0.5, q.dtype)
        zero = jnp.zeros_like(q)
        return jnp.concatenate([jnp.where(head0, q, zero), jnp.where(head0, zero, q)], axis=0)

    def unstack(o, n):
        return jnp.where(head0, o[0:n], o[n:2 * n])

    qs = stack(q_ref[0, 0:N_META, :])
    sm = lax.dot_general(qs, km, nt, preferred_element_type=F32)
    sm = sm + jnp.concatenate([mb[0:N_META], mb[GRID_W:GRID_W + N_META]], axis=0)
    pm = jnp.exp(sm - jnp.max(sm, axis=-1, keepdims=True))
    om = jnp.dot(pm.astype(BF16), vm, preferred_element_type=F32)
    om = om / jnp.sum(pm, axis=-1, keepdims=True)
    o_ref[0, 0:N_META, :] = unstack(om, N_META).astype(o_ref.dtype)

    def row_body(r, carry):
        rs = jnp.clip(r - NA_ROWS // 2, 0, rows - NA_ROWS)
        q0 = pl.multiple_of(N_META + r * GRID_W, 16)
        k0 = pl.multiple_of(N_META + rs * GRID_W, 16)
        qs = stack(q_ref[0, pl.ds(q0, GRID_W), :])
        kw = k_ref[0, pl.ds(k0, NA_ROWS * GRID_W), :]
        vw = v_ref[0, pl.ds(k0, NA_ROWS * GRID_W), :]
        s = lax.dot_general(qs, kw, nt, preferred_element_type=F32) + bias_ref[0, r - rs]
        sm = lax.dot_general(qs, km, nt, preferred_element_type=F32) + mb
        m = jnp.maximum(jnp.max(s, axis=-1, keepdims=True), jnp.max(sm, axis=-1, keepdims=True))
        p = jnp.exp(s - m)
        pm = jnp.exp(sm - m)
        den = jnp.sum(p, axis=-1, keepdims=True) + jnp.sum(pm, axis=-1, keepdims=True)
        o = (jnp.dot(p.astype(BF16), vw, preferred_element_type=F32)
             + jnp.dot(pm.astype(BF16), vm, preferred_element_type=F32))
        o = o / den
        o_ref[0, pl.ds(q0, GRID_W), :] = unstack(o, GRID_W).astype(o_ref.dtype)
        return carry

    lax.fori_loop(0, rows, row_body, 0, unroll=2)
    if lp > seq_len:
        o_ref[0, seq_len:lp, :] = jnp.zeros((lp - seq_len, LANE), o_ref.dtype)


def _attention(z3, bias_tab, mb_tab, *, rows, seq_len):
    b, lp, _ = z3.shape
    n_pairs = N_ATT_HEADS // 2
    blk = lambda off: pl.BlockSpec((1, lp, LANE), lambda hp, bi: (bi, 0, off + hp))
    return pl.pallas_call(
        functools.partial(_attn_kernel, rows=rows, seq_len=seq_len),
        out_shape=jax.ShapeDtypeStruct((b, lp, D_ATT), BF16),
        grid=(n_pairs, b),
        in_specs=[blk(0), blk(n_pairs), blk(2 * n_pairs),
                  pl.BlockSpec((1, NA_ROWS, LANE, NA_ROWS * GRID_W), lambda hp, bi: (hp, 0, 0, 0)),
                  pl.BlockSpec((1, LANE, N_META), lambda hp, bi: (hp, 0, 0))],
        out_specs=pl.BlockSpec((1, lp, LANE), lambda hp, bi: (bi, 0, hp)),
        compiler_params=_cparams(2),
        name="nbr_attention",
    )(z3, z3, z3, bias_tab, mb_tab)


def _attn_bias_tables(rel_bias, meta_bias):
    h = rel_bias.shape[0]
    oi = np.arange(NA_ROWS)[:, None]
    ki = np.arange(NA_ROWS)[None, :]
    dr = ki - oi + (NA_ROWS - 1)
    c = np.arange(GRID_W)[:, None]
    j = np.arange(GRID_W)[None, :]
    cs = np.clip(c - NA_COLS // 2, 0, GRID_W - NA_COLS)
    valid = (j >= cs) & (j < cs + NA_COLS)
    dc = np.clip(j - c + (NA_COLS - 1), 0, 2 * NA_COLS - 2)
    t = rel_bias.astype(F32)[:, dr[:, None, :, None], dc[None, :, None, :]]
    t = jnp.where(valid[None, None, :, None, :], t, NEG_BIG)
    t = t.reshape(h // 2, 2, NA_ROWS, GRID_W, NA_ROWS * GRID_W)
    t = t.transpose(0, 2, 1, 3, 4).reshape(h // 2, NA_ROWS, 2 * GRID_W, NA_ROWS * GRID_W)
    mb = jnp.broadcast_to(meta_bias.astype(F32)[:, None, :], (h, GRID_W, N_META))
    return t, mb.reshape(h // 2, 2 * GRID_W, N_META)


def _fft_kernel(c_ref, s_ref, ab_ref, w_ref, o_ref):
    a = ab_ref[0, :, 0:D_FFT]
    b = ab_ref[0, :, D_FFT:2 * D_FFT]
    f = (jnp.dot(c_ref[...], a, preferred_element_type=F32)
         + jnp.dot(s_ref[...], b, preferred_element_type=F32))
    o_ref[0] = jnp.dot(f.astype(BF16), w_ref[...], preferred_element_type=F32).astype(o_ref.dtype)


def _fourier(cmat, smat, ab3, w_bd, *, tk):
    b, lp, _ = ab3.shape
    return pl.pallas_call(
        _fft_kernel,
        out_shape=jax.ShapeDtypeStruct((b, lp, D_FFT), BF16),
        grid=(lp // tk, b),
        in_specs=[pl.BlockSpec((tk, lp), lambda j, bi: (j, 0)),
                  pl.BlockSpec((tk, lp), lambda j, bi: (j, 0)),
                  pl.BlockSpec((1, lp, 2 * D_FFT), lambda j, bi: (bi, 0, 0)),
                  pl.BlockSpec((D_FFT, D_FFT), lambda j, bi: (0, 0))],
        out_specs=pl.BlockSpec((1, tk, D_FFT), lambda j, bi: (bi, j, 0)),
        compiler_params=_cparams(2),
        name="fourier_mix",
    )(cmat, smat, ab3, w_bd)


def _twiddles(seq_len, lp):
    theta = 2.0 * math.pi / seq_len
    k = jnp.arange(lp, dtype=jnp.int32)[:, None]
    t1 = (GRID_W * jnp.arange(lp // GRID_W, dtype=jnp.int32))[None, :]
    t0 = jnp.arange(GRID_W, dtype=jnp.int32)[None, :]
    ang_a = ((k * t1) % seq_len).astype(F32) * theta
    ang_b = ((k * t0) % seq_len).astype(F32) * theta
    ca, sa = jnp.cos(ang_a)[:, :, None], jnp.sin(ang_a)[:, :, None]
    cb, sb = jnp.cos(ang_b)[:, None, :], jnp.sin(ang_b)[:, None, :]
    cmat = (ca * cb - sa * sb).reshape(lp, lp)
    smat = (sa * cb + ca * sb).reshape(lp, lp)
    idx = jnp.arange(lp)
    valid = (idx[:, None] < seq_len) & (idx[None, :] < seq_len)
    scale = seq_len ** -0.5
    cmat = jnp.where(valid, cmat * scale, 0.0).astype(BF16)
    smat = jnp.where(valid, -smat * scale, 0.0).astype(BF16)
    return cmat, smat


def _channel_dft_matrix():
    c = np.arange(D_FFT_HEAD)
    ang = 2.0 * np.pi * ((c[:, None] * c[None, :]) % D_FFT_HEAD) / D_FFT_HEAD
    cc = np.cos(ang) / math.sqrt(D_FFT_HEAD)
    sc = np.sin(ang) / math.sqrt(D_FFT_HEAD)
    out = np.zeros((D_FFT, 2 * D_FFT), np.float32)
    for g in range(N_FFT_HEADS):
        sl = slice(g * D_FFT_HEAD, (g + 1) * D_FFT_HEAD)
        out[sl, sl] = cc
        out[sl, D_FFT + g * D_FFT_HEAD:D_FFT + (g + 1) * D_FFT_HEAD] = sc
    return jnp.asarray(out, BF16)


def _block_diag(w):
    g, c, e = w.shape
    out = jnp.zeros((g * c, g * e), w.dtype)
    for i in range(g):
        out = lax.dynamic_update_slice(out, w[i], (i * c, i * e))
    return out


POOL_PAD = 16


def _pool_kernel(u_ref, w_ref, sc_ref, o_ref, s0, s1, s2, s3, s4, *, seq_len):
    lp = o_ref.shape[1]
    r_tot = lp + 2 * POOL_PAD
    lo, hi = 8, r_tot - 8
    row = lax.broadcasted_iota(jnp.int32, (lp, 1), 0)
    live = row < seq_len
    x = jnp.where(live, u_ref[0].astype(F32), 0.0)

    zeros_pad = jnp.zeros((POOL_PAD, D_POOL), F32)
    s0[0:POOL_PAD, :] = zeros_pad
    s0[POOL_PAD:POOL_PAD + lp, :] = x
    s0[POOL_PAD + lp:r_tot, :] = zeros_pad
    for s in (s1, s2, s3, s4):
        s[0:POOL_PAD, :] = zeros_pad
        s[POOL_PAD + lp:r_tot, :] = zeros_pad
    s1[lo:hi, :] = s0[lo - 1:hi - 1, :] + s0[lo:hi, :]
    s2[lo:hi, :] = s1[lo - 1:hi - 1, :] + s1[lo + 1:hi + 1, :]
    s3[lo:hi, :] = s2[lo - 2:hi - 2, :] + s2[lo + 2:hi + 2, :]
    s4[lo:hi, :] = s3[lo - 4:hi - 4, :] + s3[lo + 4:hi + 4, :]

    lane = lax.broadcasted_iota(jnp.int32, (1, D_POOL), 1)
    group = jnp.right_shift(lane, 6)
    half = jnp.where(group == 0, 1, jnp.where(group == 1, 2, jnp.where(group == 2, 4, 8)))
    cnt = jnp.minimum(row + half, seq_len) - jnp.maximum(row - half, 0)
    cnt = jnp.maximum(cnt, 1).astype(F32)
    sl = slice(POOL_PAD, POOL_PAD + lp)
    wsum = jnp.where(group == 0, s1[sl, :],
                     jnp.where(group == 1, s2[sl, :], jnp.where(group == 2, s3[sl, :], s4[sl, :])))
    p = jnp.where(live, wsum / cnt - x, 0.0)
    y = jnp.dot(p.astype(BF16), w_ref[...], preferred_element_type=F32) * sc_ref[...]
    o_ref[0] = y.astype(o_ref.dtype)


def _pool(z3, w_bd, scale, *, seq_len):
    b, lp, d_in = z3.shape
    col_block = (d_in - D_POOL) // D_POOL
    scratch = [pltpu.VMEM((lp + 2 * POOL_PAD, D_POOL), F32) for _ in range(5)]
    return pl.pallas_call(
        functools.partial(_pool_kernel, seq_len=seq_len),
        out_shape=jax.ShapeDtypeStruct((b, lp, D_POOL), BF16),
        grid=(b,),
        in_specs=[pl.BlockSpec((1, lp, D_POOL), lambda bi: (bi, 0, col_block)),
                  pl.BlockSpec((D_POOL, D_POOL), lambda bi: (0, 0)),
                  pl.BlockSpec((1, D_POOL), lambda bi: (0, 0))],
        out_specs=pl.BlockSpec((1, lp, D_POOL), lambda bi: (bi, 0, 0)),
        scratch_shapes=scratch,
        compiler_params=_cparams(1),
        name="pool_mix",
    )(z3, w_bd, scale.reshape(1, D_POOL))


def _outproj_kernel(a_ref, f_ref, p_ref, h_ref, g_ref, w_ref, o_ref):
    an = _rms(a_ref[...].astype(F32), g_ref[:, 0:D_ATT]).astype(BF16)
    fn = _rms(f_ref[...].astype(F32), g_ref[:, D_ATT:D_ATT + D_FFT]).astype(BF16)
    pn = _rms(p_ref[...].astype(F32), g_ref[:, D_ATT + D_FFT:]).astype(BF16)
    acc = jnp.dot(an, w_ref[0:D_ATT, :], preferred_element_type=F32)
    acc += jnp.dot(fn, w_ref[D_ATT:D_ATT + D_FFT, :], preferred_element_type=F32)
    acc += jnp.dot(pn, w_ref[D_ATT + D_FFT:, :], preferred_element_type=F32)
    o_ref[...] = h_ref[...] + acc


def _outproj(a, f, p, h, g, w, *, tm=512):
    n, d = h.shape
    row = lambda width: pl.BlockSpec((tm, width), lambda i: (i, 0))
    return pl.pallas_call(
        _outproj_kernel,
        out_shape=jax.ShapeDtypeStruct((n, d), F32),
        grid=(n // tm,),
        in_specs=[row(D_ATT), row(D_FFT), row(D_POOL), row(d),
                  pl.BlockSpec((1, d), lambda i: (0, 0)),
                  pl.BlockSpec((d, d), lambda i: (0, 0))],
        out_specs=row(d),
        compiler_params=_cparams(1),
        name="out_proj",
    )(a, f, p, h, g.reshape(1, d), w)


def _ffn_kernel(h_ref, g_ref, wg_ref, wu_ref, wd_ref, o_ref, xn_ref, acc_ref):
    c = pl.program_id(1)

    @pl.when(c == 0)
    def _():
        xn_ref[...] = _rms(h_ref[...], g_ref[...]).astype(BF16)

    xn = xn_ref[...]
    gate = jnp.dot(xn, wg_ref[...], preferred_element_type=F32)
    up = jnp.dot(xn, wu_ref[...], preferred_element_type=F32)
    hh = (gate * jax.nn.sigmoid(gate) * up).astype(BF16)
    part = jnp.dot(hh, wd_ref[...], preferred_element_type=F32)

    @pl.when(c == 0)
    def _():
        acc_ref[...] = part

    @pl.when(c > 0)
    def _():
        acc_ref[...] += part

    @pl.when(c == pl.num_programs(1) - 1)
    def _():
        o_ref[...] = h_ref[...] + acc_ref[...]


def _ffn(h, g, wg, wu, wd, *, tm=512, n_chunks=2):
    n, d = h.shape
    d_ff = wg.shape[1]
    fc = d_ff // n_chunks
    return pl.pallas_call(
        _ffn_kernel,
        out_shape=jax.ShapeDtypeStruct((n, d), F32),
        grid=(n // tm, n_chunks),
        in_specs=[pl.BlockSpec((tm, d), lambda i, c: (i, 0)),
                  pl.BlockSpec((1, d), lambda i, c: (0, 0)),
                  pl.BlockSpec((d, fc), lambda i, c: (0, c)),
                  pl.BlockSpec((d, fc), lambda i, c: (0, c)),
                  pl.BlockSpec((fc, d), lambda i, c: (c, 0))],
        out_specs=pl.BlockSpec((tm, d), lambda i, c: (i, 0)),
        scratch_shapes=[pltpu.VMEM((tm, d), BF16), pltpu.VMEM((tm, d), F32)],
        compiler_params=_cparams(2),
        name="swiglu_ffn",
    )(h, g.reshape(1, d), wg, wu, wd)


MOE_TILE = 1024
MOE_ROWS = 128


def _router_kernel(h_ref, g_ref, wr_ref, tri_ref, xn_ref, comb_ref, rank_ref, *, seq_len, lp):
    t = h_ref.shape[0]
    y = _rms(h_ref[...], g_ref[...])
    xn_ref[...] = y.astype(BF16)
    logits = jnp.dot(y, wr_ref[...], preferred_element_type=F32, precision=lax.Precision.HIGHEST)
    lane = lax.broadcasted_iota(jnp.int32, (t, LANE), 1)
    lg = jnp.where(lane < N_EXPERTS, logits, -jnp.inf)
    m1 = jnp.max(lg, axis=-1, keepdims=True)
    i1 = jnp.min(jnp.where(lg == m1, lane, LANE), axis=-1, keepdims=True)
    lg2 = jnp.where(lane == i1, -jnp.inf, lg)
    m2 = jnp.max(lg2, axis=-1, keepdims=True)
    i2 = jnp.min(jnp.where(lg2 == m2, lane, LANE), axis=-1, keepdims=True)
    e2 = jnp.exp(m2 - m1)
    g1 = 1.0 / (1.0 + e2)
    g2 = e2 / (1.0 + e2)
    rowf = (pl.program_id(0) * t + lax.broadcasted_iota(jnp.int32, (t, 1), 0)).astype(F32)
    seq = jnp.floor((rowf + 0.5) * (1.0 / lp))
    live = (rowf - seq * lp) < seq_len
    first = (lane == i1) & live
    second = (lane == i2) & live
    sel = first | second
    comb_ref[...] = jnp.where(first, g1, 0.0) + jnp.where(second, g2, 0.0)
    member = jnp.where(sel, 1.0, 0.0).astype(BF16)
    rank = jnp.dot(tri_ref[...], member, preferred_element_type=F32)
    rank_ref[...] = jnp.where(sel, rank, -1.0)


def _router(h, g, w_router_pad, tri, *, seq_len, lp):
    n, d = h.shape
    t = MOE_TILE
    row = lambda width: pl.BlockSpec((t, width), lambda i: (i, 0))
    return pl.pallas_call(
        functools.partial(_router_kernel, seq_len=seq_len, lp=lp),
        out_shape=(jax.ShapeDtypeStruct((n, d), BF16),
                   jax.ShapeDtypeStruct((n, LANE), F32),
                   jax.ShapeDtypeStruct((n, LANE), F32)),
        grid=(n // t,),
        in_specs=[row(d),
                  pl.BlockSpec((1, d), lambda i: (0, 0)),
                  pl.BlockSpec((d, LANE), lambda i: (0, 0)),
                  pl.BlockSpec((t, t), lambda i: (0, 0))],
        out_specs=(row(d), row(LANE), row(LANE)),
        compiler_params=_cparams(1),
        name="moe_router",
    )(h, g.reshape(1, d), w_router_pad, tri)


def _moe_kernel(cnt_ref, xn_ref, h_ref, comb_ref, rank_ref, rrow_ref, wg_ref, wu_ref, wd_ref,
                y_ref, xg_ref, oacc_ref):
    i, e, c = pl.program_id(0), pl.program_id(1), pl.program_id(2)
    t = xn_ref.shape[0]
    rc = MOE_ROWS
    n_pass = (cnt_ref[i * N_EXPERTS + e] + rc - 1) // rc

    @pl.when((e == 0) & (c == 0))
    def _():
        y_ref[...] = h_ref[...]

    @pl.when(c == 0)
    def _():
        rrow = rrow_ref[0, pl.ds(e, 1), :]
        sub = lax.broadcasted_iota(jnp.int32, (rc, 1), 0).astype(F32)

        def compact(r, carry):
            base = pl.multiple_of(r * rc, rc)
            onehot = jnp.where(rrow == sub + (r * rc).astype(F32), 1.0, 0.0).astype(BF16)
            xg = jnp.dot(onehot, xn_ref[...], preferred_element_type=F32)
            xg_ref[pl.ds(base, rc), :] = xg.astype(BF16)
            oacc_ref[pl.ds(base, rc), :] = jnp.zeros((rc, oacc_ref.shape[1]), F32)
            return carry

        lax.fori_loop(0, n_pass, compact, 0)

    def expert(r, carry):
        base = pl.multiple_of(r * rc, rc)
        xr = xg_ref[pl.ds(base, rc), :]
        gate = jnp.dot(xr, wg_ref[0], preferred_element_type=F32)
        up = jnp.dot(xr, wu_ref[0], preferred_element_type=F32)
        hh = (gate * jax.nn.sigmoid(gate) * up).astype(BF16)
        oacc_ref[pl.ds(base, rc), :] += jnp.dot(hh, wd_ref[0], preferred_element_type=F32)
        return carry

    lax.fori_loop(0, n_pass, expert, 0)

    @pl.when(c == pl.num_programs(2) - 1)
    def _():
        lane = lax.broadcasted_iota(jnp.int32, (1, LANE), 1)
        pick = lane == e
        rcol = jnp.sum(jnp.where(pick, rank_ref[...], 0.0), axis=-1, keepdims=True)
        gcol = jnp.sum(jnp.where(pick, comb_ref[...], 0.0), axis=-1, keepdims=True)
        lane_rc = lax.broadcasted_iota(jnp.int32, (1, rc), 1).astype(F32)

        def scatter(r, carry):
            base = pl.multiple_of(r * rc, rc)
            onehot_t = jnp.where(rcol == lane_rc + (r * rc).astype(F32), 1.0, 0.0).astype(BF16)
            o = oacc_ref[pl.ds(base, rc), :]
            o_hi = o.astype(BF16)
            o_lo = (o - o_hi.astype(F32)).astype(BF16)
            back = (jnp.dot(onehot_t, o_hi, preferred_element_type=F32)
                    + jnp.dot(onehot_t, o_lo, preferred_element_type=F32))
            y_ref[...] += back * gcol
            return carry

        lax.fori_loop(0, n_pass, scatter, 0)


def _moe(xn, h, comb, rank, rank_rows, counts, wg, wu, wd, *, n_chunks=4):
    n, d = h.shape
    t = MOE_TILE
    d_ff = wg.shape[2]
    fc = d_ff // n_chunks
    row = lambda width: pl.BlockSpec((t, width), lambda i, e, c, cnt: (i, 0))
    grid_spec = pltpu.PrefetchScalarGridSpec(
        num_scalar_prefetch=1,
        grid=(n // t, N_EXPERTS, n_chunks),
        in_specs=[row(d), row(d), row(LANE), row(LANE),
                  pl.BlockSpec((1, N_EXPERTS, t), lambda i, e, c, cnt: (i, 0, 0)),
                  pl.BlockSpec((1, d, fc), lambda i, e, c, cnt: (e, 0, c)),
                  pl.BlockSpec((1, d, fc), lambda i, e, c, cnt: (e, 0, c)),
                  pl.BlockSpec((1, fc, d), lambda i, e, c, cnt: (e, c, 0))],
        out_specs=row(d),
        scratch_shapes=[pltpu.VMEM((t, d), BF16), pltpu.VMEM((t, d), F32)],
    )
    return pl.pallas_call(
        _moe_kernel,
        out_shape=jax.ShapeDtypeStruct((n, d), F32),
        grid_spec=grid_spec,
        compiler_params=_cparams(3),
        name="moe_experts",
    )(counts, xn, h, comb, rank, rank_rows, wg, wu, wd)


def _norm_kernel(x_ref, g_ref, o_ref):
    o_ref[...] = _rms(x_ref[...], g_ref[...])


def _final_norm(h, g, *, tm=1024):
    n, d = h.shape
    return pl.pallas_call(
        _norm_kernel,
        out_shape=jax.ShapeDtypeStruct((n, d), F32),
        grid=(n // tm,),
        in_specs=[pl.BlockSpec((tm, d), lambda i: (i, 0)), pl.BlockSpec((1, d), lambda i: (0, 0))],
        out_specs=pl.BlockSpec((tm, d), lambda i: (i, 0)),
        compiler_params=_cparams(1),
        name="final_norm",
    )(h, g.reshape(1, d))


def _fft_row_tile(lp):
    best = 16
    for tk in range(16, lp + 1, 16):
        if lp % tk == 0 and tk * lp * 2 <= 4608 * 1024:
            best = tk
    return best


def _trunk(x, meta_tokens, prm):
    b, t_len, d = x.shape
    depth = prm["w_in"].shape[0]
    seq_len = N_META + t_len
    lp = -(-seq_len // LANE) * LANE
    n = b * lp
    rows = t_len // GRID_W
    assert t_len % GRID_W == 0 and rows >= NA_ROWS and n % MOE_TILE == 0

    meta = jnp.broadcast_to(meta_tokens.astype(x.dtype)[None], (b, N_META, d))
    h = jnp.concatenate([meta, x, jnp.zeros((b, lp - seq_len, d), x.dtype)], axis=1).reshape(n, d)

    cmat, smat = _twiddles(seq_len, lp)
    tk = _fft_row_tile(lp)
    tri = jnp.tril(jnp.ones((MOE_TILE, MOE_TILE), BF16), -1)

    for i in range(depth):
        z = _norm_matmul(h, prm["norm_mix"][i], prm["w_in"][i])
        z3 = z.reshape(b, lp, D_IN)
        ab = _matmul_cols(z, prm["chan_dft"], (3 * D_ATT) // D_FFT)
        a = _attention(z3, prm["attn_bias"][i], prm["attn_mbias"][i], rows=rows, seq_len=seq_len)
        f = _fourier(cmat, smat, ab.reshape(b, lp, 2 * D_FFT), prm["w_fft"][i], tk=tk)
        p = _pool(z3, prm["w_pool"][i], prm["pool_scale"][i], seq_len=seq_len)
        h = _outproj(a.reshape(n, D_ATT), f.reshape(n, D_FFT), p.reshape(n, D_POOL), h,
                     prm["norm_groups"][i], prm["w_out"][i])
        j = i // 2
        if i % 2 == 0:
            h = _ffn(h, prm["norm_ffn"][i], prm["w_ff_gate"][j], prm["w_ff_up"][j], prm["w_ff_down"][j])
        else:
            xn, comb, rank = _router(h, prm["norm_ffn"][i], prm["w_router"][j], tri,
                                     seq_len=seq_len, lp=lp)
            n_tiles = n // MOE_TILE
            rank8 = rank[:, :N_EXPERTS].reshape(n_tiles, MOE_TILE, N_EXPERTS)
            rank_rows = rank8.transpose(0, 2, 1)
            counts = jnp.sum(rank8 >= 0.0, axis=1).astype(jnp.int32).reshape(-1)
            h = _moe(xn, h, comb, rank, rank_rows, counts,
                     prm["w_exp_gate"][j], prm["w_exp_up"][j], prm["w_exp_down"][j])
    y = _final_norm(h, prm["norm_final"])
    return y.reshape(b, lp, d)[:, N_META:seq_len]


def kernel(x_prompt, x_sample, meta_tokens, norm_mix, w_in, w_fft, w_pool, pool_scale, rel_bias, meta_bias,
           norm_groups, w_out, norm_ffn, w_ff_gate, w_ff_up, w_ff_down, w_router, w_exp_gate, w_exp_up,
           w_exp_down, norm_final):
    depth = w_in.shape[0]
    tabs = [_attn_bias_tables(rel_bias[i], meta_bias[i]) for i in range(depth)]
    prm = {
        "norm_mix": norm_mix, "norm_groups": norm_groups, "norm_ffn": norm_ffn, "norm_final": norm_final,
        "pool_scale": pool_scale,
        "w_in": w_in.astype(BF16), "w_out": w_out.astype(BF16),
        "w_ff_gate": w_ff_gate.astype(BF16), "w_ff_up": w_ff_up.astype(BF16), "w_ff_down": w_ff_down.astype(BF16),
        "w_exp_gate": w_exp_gate.astype(BF16), "w_exp_up": w_exp_up.astype(BF16),
        "w_exp_down": w_exp_down.astype(BF16),
        "w_router": jnp.pad(w_router, ((0, 0), (0, 0), (0, LANE - N_EXPERTS))),
        "w_fft": jnp.stack([_block_diag(w_fft[i]) for i in range(depth)]).astype(BF16),
        "w_pool": jnp.stack([_block_diag(w_pool[i]) for i in range(depth)]).astype(BF16),
        "chan_dft": _channel_dft_matrix(),
        "attn_bias": [tb[0] for tb in tabs], "attn_mbias": [tb[1] for tb in tabs],
    }
    y_prompt = _trunk(x_prompt, meta_tokens, prm)
    y_sample = _trunk(x_sample, meta_tokens, prm)
    return (y_prompt, y_sample)
```

```python
import functools
import math

import numpy as np
import jax
import jax.numpy as jnp
from jax import lax
from jax.experimental import pallas as pl
from jax.experimental.pallas import tpu as pltpu

D_MODEL = 1024
N_META = 16
GRID_W = 64
D_HEAD = 64
D_ATT = 512
N_ATT_HEADS = 8
D_FFT = 256
N_FFT_HEADS = 4
D_FFT_HEAD = 64
D_POOL = 256
POOL_WINDOWS = (2, 4, 8, 16)
D_POOL_GROUP = 64
D_IN = 2048
NA_ROWS = 8
NA_COLS = 16
N_EXPERTS = 8
EPS = 1e-6

LANE = 128
NEG_BIG = -1e30
VMEM_LIMIT = 56 * 1024 * 1024

F32 = jnp.float32
BF16 = jnp.bfloat16


def _cparams(n_axes, vmem=VMEM_LIMIT):
    return pltpu.CompilerParams(dimension_semantics=("arbitrary",) * n_axes,
                                vmem_limit_bytes=vmem)


def _rms(x, g):
    return x * lax.rsqrt(jnp.mean(x * x, axis=-1, keepdims=True) + EPS) * g


def _norm_matmul_kernel(x_ref, g_ref, w_ref, o_ref, xn_ref):
    @pl.when(pl.program_id(1) == 0)
    def _():
        xn_ref[...] = _rms(x_ref[...], g_ref[...]).astype(BF16)

    o_ref[...] = jnp.dot(xn_ref[...], w_ref[...], preferred_element_type=F32).astype(o_ref.dtype)


def _norm_matmul(x, g, w, *, tm=512, tn=1024):
    n, d = x.shape
    n_out = w.shape[1]
    return pl.pallas_call(
        _norm_matmul_kernel,
        out_shape=jax.ShapeDtypeStruct((n, n_out), BF16),
        grid=(n // tm, n_out // tn),
        in_specs=[pl.BlockSpec((tm, d), lambda i, j: (i, 0)),
                  pl.BlockSpec((1, d), lambda i, j: (0, 0)),
                  pl.BlockSpec((d, tn), lambda i, j: (0, j))],
        out_specs=pl.BlockSpec((tm, tn), lambda i, j: (i, j)),
        scratch_shapes=[pltpu.VMEM((tm, d), BF16)],
        compiler_params=_cparams(2),
        name="norm_matmul",
    )(x, g.reshape(1, d), w)


def _matmul_kernel(x_ref, w_ref, o_ref):
    o_ref[...] = jnp.dot(x_ref[...], w_ref[...], preferred_element_type=F32).astype(o_ref.dtype)


def _matmul_cols(x, w, col_block, *, tm=1024):
    n = x.shape[0]
    k, n_out = w.shape
    return pl.pallas_call(
        _matmul_kernel,
        out_shape=jax.ShapeDtypeStruct((n, n_out), BF16),
        grid=(n // tm,),
        in_specs=[pl.BlockSpec((tm, k), lambda i: (i, col_block)),
                  pl.BlockSpec((k, n_out), lambda i: (0, 0))],
        out_specs=pl.BlockSpec((tm, n_out), lambda i: (i, 0)),
        compiler_params=_cparams(1),
        name="channel_dft",
    )(x, w)


def _attn_kernel(q_ref, k_ref, v_ref, bias_ref, mb_ref, o_ref, *, rows, seq_len):
    lp = o_ref.shape[1]
    lane = lax.broadcasted_iota(jnp.int32, (1, LANE), 1)
    head0 = lane < D_HEAD
    nt = (((1,), (1,)), ((), ()))

    km = k_ref[0, 0:N_META, :]
    vm = v_ref[0, 0:N_META, :]
    mb = mb_ref[0]

    def stack(q):
        q = q * jnp.asarray(D_HEAD ** -0.5, q.dtype)
        zero = jnp.zeros_like(q)
        return jnp.concatenate([jnp.where(head0, q, zero), jnp.where(head0, zero, q)], axis=0)

    def unstack(o, n):
        return jnp.where(head0, o[0:n], o[n:2 * n])

    qs = stack(q_ref[0, 0:N_META, :])
    sm = lax.dot_general(qs, km, nt, preferred_element_type=F32)
    sm = sm + jnp.concatenate([mb[0:N_META], mb[GRID_W:GRID_W + N_META]], axis=0)
    pm = jnp.exp(sm - jnp.max(sm, axis=-1, keepdims=True))
    om = jnp.dot(pm.astype(BF16), vm, preferred_element_type=F32)
    om = om / jnp.sum(pm, axis=-1, keepdims=True)
    o_ref[0, 0:N_META, :] = unstack(om, N_META).astype(o_ref.dtype)

    def row_body(r, carry):
        rs = jnp.clip(r - NA_ROWS // 2, 0, rows - NA_ROWS)
        q0 = pl.multiple_of(N_META + r * GRID_W, 16)
        k0 = pl.multiple_of(N_META + rs * GRID_W, 16)
        qs = stack(q_ref[0, pl.ds(q0, GRID_W), :])
        kw = k_ref[0, pl.ds(k0, NA_ROWS * GRID_W), :]
        vw = v_ref[0, pl.ds(k0, NA_ROWS * GRID_W), :]
        s = lax.dot_general(qs, kw, nt, preferred_element_type=F32) + bias_ref[0, r - rs]
        sm = lax.dot_general(qs, km, nt, preferred_element_type=F32) + mb
        m = jnp.maximum(jnp.max(s, axis=-1, keepdims=True), jnp.max(sm, axis=-1, keepdims=True))
        p = jnp.exp(s - m)
        pm = jnp.exp(sm - m)
        den = jnp.sum(p, axis=-1, keepdims=True) + jnp.sum(pm, axis=-1, keepdims=True)
        o = (jnp.dot(p.astype(BF16), vw, preferred_element_type=F32)
             + jnp.dot(pm.astype(BF16), vm, preferred_element_type=F32))
        o = o / den
        o_ref[0, pl.ds(q0, GRID_W), :] = unstack(o, GRID_W).astype(o_ref.dtype)
        return carry

    lax.fori_loop(0, rows, row_body, 0, unroll=2)
    if lp > seq_len:
        o_ref[0, seq_len:lp, :] = jnp.zeros((lp - seq_len, LANE), o_ref.dtype)


def _attention(z3, bias_tab, mb_tab, *, rows, seq_len):
    b, lp, _ = z3.shape
    n_pairs = N_ATT_HEADS // 2
    blk = lambda off: pl.BlockSpec((1, lp, LANE), lambda hp, bi: (bi, 0, off + hp))
    return pl.pallas_call(
        functools.partial(_attn_kernel, rows=rows, seq_len=seq_len),
        out_shape=jax.ShapeDtypeStruct((b, lp, D_ATT), BF16),
        grid=(n_pairs, b),
        in_specs=[blk(0), blk(n_pairs), blk(2 * n_pairs),
                  pl.BlockSpec((1, NA_ROWS, LANE, NA_ROWS * GRID_W), lambda hp, bi: (hp, 0, 0, 0)),
                  pl.BlockSpec((1, LANE, N_META), lambda hp, bi: (hp, 0, 0))],
        out_specs=pl.BlockSpec((1, lp, LANE), lambda hp, bi: (bi, 0, hp)),
        compiler_params=_cparams(2),
        name="nbr_attention",
    )(z3, z3, z3, bias_tab, mb_tab)


def _attn_bias_tables(rel_bias, meta_bias):
    h, n_dr, n_dc = rel_bias.shape
    c = np.arange(GRID_W)[:, None]
    j = np.arange(GRID_W)[None, :]
    cs = np.clip(c - NA_COLS // 2, 0, GRID_W - NA_COLS)
    valid = (j >= cs) & (j < cs + NA_COLS)
    ext = jnp.zeros((h, n_dr, 2 * GRID_W), F32)
    ext = lax.dynamic_update_slice(ext, rel_bias.astype(F32), (0, 0, GRID_W - NA_COLS))
    skew = jnp.tile(ext, (1, 1, GRID_W))[:, :, :GRID_W * (2 * GRID_W - 1)]
    toep = skew.reshape(h, n_dr, GRID_W, 2 * GRID_W - 1)[..., GRID_W - 1:]
    toep = jnp.where(valid[None, None], toep, NEG_BIG)
    t = jnp.stack([toep[:, NA_ROWS - 1 - oi:2 * NA_ROWS - 1 - oi] for oi in range(NA_ROWS)], axis=1)
    t = t.transpose(0, 1, 3, 2, 4)
    t = t.reshape(h // 2, 2, NA_ROWS, GRID_W, NA_ROWS * GRID_W)
    t = t.transpose(0, 2, 1, 3, 4).reshape(h // 2, NA_ROWS, 2 * GRID_W, NA_ROWS * GRID_W)
    mb = jnp.broadcast_to(meta_bias.astype(F32)[:, None, :], (h, GRID_W, N_META))
    return t, mb.reshape(h // 2, 2 * GRID_W, N_META)


def _fft_kernel(c_ref, s_ref, ab_ref, w_ref, o_ref):
    a = ab_ref[0, :, 0:D_FFT]
    b = ab_ref[0, :, D_FFT:2 * D_FFT]
    f = (jnp.dot(c_ref[...], a, preferred_element_type=F32)
         + jnp.dot(s_ref[...], b, preferred_element_type=F32))
    o_ref[0] = jnp.dot(f.astype(BF16), w_ref[...], preferred_element_type=F32).astype(o_ref.dtype)


def _fourier(cmat, smat, ab3, w_bd, *, tk):
    b, lp, _ = ab3.shape
    return pl.pallas_call(
        _fft_kernel,
        out_shape=jax.ShapeDtypeStruct((b, lp, D_FFT), BF16),
        grid=(lp // tk, b),
        in_specs=[pl.BlockSpec((tk, lp), lambda j, bi: (j, 0)),
                  pl.BlockSpec((tk, lp), lambda j, bi: (j, 0)),
                  pl.BlockSpec((1, lp, 2 * D_FFT), lambda j, bi: (bi, 0, 0)),
                  pl.BlockSpec((D_FFT, D_FFT), lambda j, bi: (0, 0))],
        out_specs=pl.BlockSpec((1, tk, D_FFT), lambda j, bi: (bi, j, 0)),
        compiler_params=_cparams(2),
        name="fourier_mix",
    )(cmat, smat, ab3, w_bd)


def _twiddles(seq_len, lp):
    theta = 2.0 * math.pi / seq_len
    k = jnp.arange(lp, dtype=jnp.int32)[:, None]
    t1 = (GRID_W * jnp.arange(lp // GRID_W, dtype=jnp.int32))[None, :]
    t0 = jnp.arange(GRID_W, dtype=jnp.int32)[None, :]
    ang_a = ((k * t1) % seq_len).astype(F32) * theta
    ang_b = ((k * t0) % seq_len).astype(F32) * theta
    ca, sa = jnp.cos(ang_a)[:, :, None], jnp.sin(ang_a)[:, :, None]
    cb, sb = jnp.cos(ang_b)[:, None, :], jnp.sin(ang_b)[:, None, :]
    cmat = (ca * cb - sa * sb).reshape(lp, lp)
    smat = (sa * cb + ca * sb).reshape(lp, lp)
    idx = jnp.arange(lp)
    valid = (idx[:, None] < seq_len) & (idx[None, :] < seq_len)
    scale = seq_len ** -0.5
    cmat = jnp.where(valid, cmat * scale, 0.0).astype(BF16)
    smat = jnp.where(valid, -smat * scale, 0.0).astype(BF16)
    return cmat, smat


def _channel_dft_matrix():
    c = np.arange(D_FFT_HEAD)
    ang = 2.0 * np.pi * ((c[:, None] * c[None, :]) % D_FFT_HEAD) / D_FFT_HEAD
    cc = np.cos(ang) / math.sqrt(D_FFT_HEAD)
    sc = np.sin(ang) / math.sqrt(D_FFT_HEAD)
    out = np.zeros((D_FFT, 2 * D_FFT), np.float32)
    for g in range(N_FFT_HEADS):
        sl = slice(g * D_FFT_HEAD, (g + 1) * D_FFT_HEAD)
        out[sl, sl] = cc
        out[sl, D_FFT + g * D_FFT_HEAD:D_FFT + (g + 1) * D_FFT_HEAD] = sc
    return jnp.asarray(out, BF16)


def _block_diag(w):
    g, c, e = w.shape
    out = jnp.zeros((g * c, g * e), w.dtype)
    for i in range(g):
        out = lax.dynamic_update_slice(out, w[i], (i * c, i * e))
    return out


POOL_PAD = 16


def _pool_kernel(u_ref, w_ref, sc_ref, o_ref, s0, s1, s2, s3, s4, *, seq_len):
    lp = o_ref.shape[1]
    r_tot = lp + 2 * POOL_PAD
    lo, hi = 8, r_tot - 8
    row = lax.broadcasted_iota(jnp.int32, (lp, 1), 0)
    live = row < seq_len
    x = jnp.where(live, u_ref[0].astype(F32), 0.0)

    zeros_pad = jnp.zeros((POOL_PAD, D_POOL), F32)
    s0[0:POOL_PAD, :] = zeros_pad
    s0[POOL_PAD:POOL_PAD + lp, :] = x
    s0[POOL_PAD + lp:r_tot, :] = zeros_pad
    for s in (s1, s2, s3, s4):
        s[0:POOL_PAD, :] = zeros_pad
        s[POOL_PAD + lp:r_tot, :] = zeros_pad
    s1[lo:hi, :] = s0[lo - 1:hi - 1, :] + s0[lo:hi, :]
    s2[lo:hi, :] = s1[lo - 1:hi - 1, :] + s1[lo + 1:hi + 1, :]
    s3[lo:hi, :] = s2[lo - 2:hi - 2, :] + s2[lo + 2:hi + 2, :]
    s4[lo:hi, :] = s3[lo - 4:hi - 4, :] + s3[lo + 4:hi + 4, :]

    lane = lax.broadcasted_iota(jnp.int32, (1, D_POOL), 1)
    group = jnp.right_shift(lane, 6)
    half = jnp.where(group == 0, 1, jnp.where(group == 1, 2, jnp.where(group == 2, 4, 8)))
    cnt = jnp.minimum(row + half, seq_len) - jnp.maximum(row - half, 0)
    cnt = jnp.maximum(cnt, 1).astype(F32)
    sl = slice(POOL_PAD, POOL_PAD + lp)
    wsum = jnp.where(group == 0, s1[sl, :],
                     jnp.where(group == 1, s2[sl, :], jnp.where(group == 2, s3[sl, :], s4[sl, :])))
    p = jnp.where(live, wsum / cnt - x, 0.0)
    y = jnp.dot(p.astype(BF16), w_ref[...], preferred_element_type=F32) * sc_ref[...]
    o_ref[0] = y.astype(o_ref.dtype)


def _pool(z3, w_bd, scale, *, seq_len):
    b, lp, d_in = z3.shape
    col_block = (d_in - D_POOL) // D_POOL
    scratch = [pltpu.VMEM((lp + 2 * POOL_PAD, D_POOL), F32) for _ in range(5)]
    return pl.pallas_call(
        functools.partial(_pool_kernel, seq_len=seq_len),
        out_shape=jax.ShapeDtypeStruct((b, lp, D_POOL), BF16),
        grid=(b,),
        in_specs=[pl.BlockSpec((1, lp, D_POOL), lambda bi: (bi, 0, col_block)),
                  pl.BlockSpec((D_POOL, D_POOL), lambda bi: (0, 0)),
                  pl.BlockSpec((1, D_POOL), lambda bi: (0, 0))],
        out_specs=pl.BlockSpec((1, lp, D_POOL), lambda bi: (bi, 0, 0)),
        scratch_shapes=scratch,
        compiler_params=_cparams(1),
        name="pool_mix",
    )(z3, w_bd, scale.reshape(1, D_POOL))


def _outproj_kernel(a_ref, f_ref, p_ref, h_ref, g_ref, w_ref, o_ref):
    an = _rms(a_ref[...].astype(F32), g_ref[:, 0:D_ATT]).astype(BF16)
    fn = _rms(f_ref[...].astype(F32), g_ref[:, D_ATT:D_ATT + D_FFT]).astype(BF16)
    pn = _rms(p_ref[...].astype(F32), g_ref[:, D_ATT + D_FFT:]).astype(BF16)
    acc = jnp.dot(an, w_ref[0:D_ATT, :], preferred_element_type=F32)
    acc += jnp.dot(fn, w_ref[D_ATT:D_ATT + D_FFT, :], preferred_element_type=F32)
    acc += jnp.dot(pn, w_ref[D_ATT + D_FFT:, :], preferred_element_type=F32)
    o_ref[...] = h_ref[...] + acc


def _outproj(a, f, p, h, g, w, *, tm=512):
    n, d = h.shape
    row = lambda width: pl.BlockSpec((tm, width), lambda i: (i, 0))
    return pl.pallas_call(
        _outproj_kernel,
        out_shape=jax.ShapeDtypeStruct((n, d), F32),
        grid=(n // tm,),
        in_specs=[row(D_ATT), row(D_FFT), row(D_POOL), row(d),
                  pl.BlockSpec((1, d), lambda i: (0, 0)),
                  pl.BlockSpec((d, d), lambda i: (0, 0))],
        out_specs=row(d),
        compiler_params=_cparams(1),
        name="out_proj",
    )(a, f, p, h, g.reshape(1, d), w)


def _ffn_kernel(h_ref, g_ref, wg_ref, wu_ref, wd_ref, o_ref, xn_ref, acc_ref):
    c = pl.program_id(1)

    @pl.when(c == 0)
    def _():
        xn_ref[...] = _rms(h_ref[...], g_ref[...]).astype(BF16)

    xn = xn_ref[...]
    gate = jnp.dot(xn, wg_ref[...], preferred_element_type=F32)
    up = jnp.dot(xn, wu_ref[...], preferred_element_type=F32)
    hh = (gate * jax.nn.sigmoid(gate) * up).astype(BF16)
    part = jnp.dot(hh, wd_ref[...], preferred_element_type=F32)

    @pl.when(c == 0)
    def _():
        acc_ref[...] = part

    @pl.when(c > 0)
    def _():
        acc_ref[...] += part

    @pl.when(c == pl.num_programs(1) - 1)
    def _():
        o_ref[...] = h_ref[...] + acc_ref[...]


def _ffn(h, g, wg, wu, wd, *, tm=512, n_chunks=2):
    n, d = h.shape
    d_ff = wg.shape[1]
    fc = d_ff // n_chunks
    return pl.pallas_call(
        _ffn_kernel,
        out_shape=jax.ShapeDtypeStruct((n, d), F32),
        grid=(n // tm, n_chunks),
        in_specs=[pl.BlockSpec((tm, d), lambda i, c: (i, 0)),
                  pl.BlockSpec((1, d), lambda i, c: (0, 0)),
                  pl.BlockSpec((d, fc), lambda i, c: (0, c)),
                  pl.BlockSpec((d, fc), lambda i, c: (0, c)),
                  pl.BlockSpec((fc, d), lambda i, c: (c, 0))],
        out_specs=pl.BlockSpec((tm, d), lambda i, c: (i, 0)),
        scratch_shapes=[pltpu.VMEM((tm, d), BF16), pltpu.VMEM((tm, d), F32)],
        compiler_params=_cparams(2),
        name="swiglu_ffn",
    )(h, g.reshape(1, d), wg, wu, wd)


MOE_TILE = 1024
EXPERT_ROWS = 512
R_GATE, R_EXPERT, R_RANK = 0, 2, 4


def _router_kernel(h_ref, g_ref, wr_ref, tri_ref, xn_ref, rec_ref, cnt_ref, *, seq_len, lp):
    t = h_ref.shape[0]
    y = _rms(h_ref[...], g_ref[...])
    xn_ref[...] = y
    logits = jnp.dot(y, wr_ref[...], preferred_element_type=F32, precision=lax.Precision.HIGHEST)
    lane = lax.broadcasted_iota(jnp.int32, (t, LANE), 1)
    lg = jnp.where(lane < N_EXPERTS, logits, -jnp.inf)
    m1 = jnp.max(lg, axis=-1, keepdims=True)
    i1 = jnp.min(jnp.where(lg == m1, lane, LANE), axis=-1, keepdims=True)
    lg2 = jnp.where(lane == i1, -jnp.inf, lg)
    m2 = jnp.max(lg2, axis=-1, keepdims=True)
    i2 = jnp.min(jnp.where(lg2 == m2, lane, LANE), axis=-1, keepdims=True)
    e2 = jnp.exp(m2 - m1)
    g1 = 1.0 / (1.0 + e2)
    g2 = e2 / (1.0 + e2)
    rowf = (pl.program_id(0) * t + lax.broadcasted_iota(jnp.int32, (t, 1), 0)).astype(F32)
    seq = jnp.floor((rowf + 0.5) * (1.0 / lp))
    live = (rowf - seq * lp) < seq_len
    first = lane == i1
    second = lane == i2
    member = jnp.where((first | second) & live, 1.0, 0.0)
    rank = jnp.dot(tri_ref[...], member.astype(BF16), preferred_element_type=F32)
    r1 = jnp.sum(jnp.where(first, rank, 0.0), axis=-1, keepdims=True)
    r2 = jnp.sum(jnp.where(second, rank, 0.0), axis=-1, keepdims=True)
    dead = jnp.logical_not(live)
    rec = jnp.zeros((t, LANE), F32)
    for ln, val in ((R_GATE, jnp.where(dead, 0.0, g1)), (R_GATE + 1, jnp.where(dead, 0.0, g2)),
                    (R_EXPERT, jnp.where(dead, -1.0, i1.astype(F32))),
                    (R_EXPERT + 1, jnp.where(dead, -1.0, i2.astype(F32))),
                    (R_RANK, r1), (R_RANK + 1, r2)):
        rec = jnp.where(lane == ln, val, rec)
    rec_ref[...] = rec
    cnt_ref[0] = jnp.broadcast_to(jnp.sum(member, axis=0, keepdims=True), (8, LANE))


def _router(h, g, w_router_pad, tri, *, seq_len, lp):
    n, d = h.shape
    t = MOE_TILE
    row = lambda width: pl.BlockSpec((t, width), lambda i: (i, 0))
    return pl.pallas_call(
        functools.partial(_router_kernel, seq_len=seq_len, lp=lp),
        out_shape=(jax.ShapeDtypeStruct((n, d), F32),
                   jax.ShapeDtypeStruct((n, LANE), F32),
                   jax.ShapeDtypeStruct((n // t, 8, LANE), F32)),
        grid=(n // t,),
        in_specs=[row(d),
                  pl.BlockSpec((1, d), lambda i: (0, 0)),
                  pl.BlockSpec((d, LANE), lambda i: (0, 0)),
                  pl.BlockSpec((t, t), lambda i: (0, 0))],
        out_specs=(row(d), row(LANE), pl.BlockSpec((1, 8, LANE), lambda i: (i, 0, 0))),
        compiler_params=_cparams(1),
        name="moe_router",
    )(h, g.reshape(1, d), w_router_pad, tri)


def _dispatch_plan(rec, counts, *, n_row_tiles):
    n = rec.shape[0]
    n_tiles = counts.shape[0]
    cnt = counts[:, 0, :N_EXPERTS]
    before = jnp.cumsum(cnt, axis=0) - cnt
    total = jnp.sum(cnt, axis=0)
    padded = jnp.ceil(total / EXPERT_ROWS) * EXPERT_ROWS
    ends = jnp.cumsum(padded)
    start = ends - padded
    base = (start[None, :] + before)[:, None, :]
    slots = []
    for k in range(2):
        e = rec[:, R_EXPERT + k].reshape(n_tiles, MOE_TILE, 1)
        r = rec[:, R_RANK + k].reshape(n_tiles, MOE_TILE)
        hit = e == jnp.arange(N_EXPERTS, dtype=F32)[None, None, :]
        slot = jnp.sum(jnp.where(hit, base, 0.0), axis=-1) + r
        slots.append(jnp.where(e[..., 0] < 0, -1.0, slot))
    pos = jnp.stack(slots, axis=-1).astype(jnp.int32).reshape(n * 2)
    tile_row0 = jnp.arange(n_row_tiles, dtype=F32) * EXPERT_ROWS
    tile_expert = jnp.sum(tile_row0[:, None] >= ends[None, :], axis=-1)
    tile_live = (tile_row0 < ends[-1]).astype(jnp.int32)
    tile_expert = jnp.minimum(tile_expert, N_EXPERTS - 1).astype(jnp.int32)
    return pos, tile_expert, tile_live


def _row_copy(src_ref, src_row, dst_ref, dst_row, sem):
    return pltpu.make_async_copy(src_ref.at[pl.ds(src_row, 1)], dst_ref.at[pl.ds(dst_row, 1)], sem)


def _dispatch_kernel(pos_ref, x_ref, zero_ref, xs_ref, sem):
    del zero_ref
    t = x_ref.shape[0]
    base = pl.program_id(0) * t * 2

    def issue(j, carry):
        for k in range(2):
            p = pos_ref[base + 2 * j + k]

            @pl.when(p >= 0)
            def _():
                _row_copy(x_ref, j, xs_ref, p, sem).start()
        return carry

    def drain(j, carry):
        for k in range(2):
            p = pos_ref[base + 2 * j + k]

            @pl.when(p >= 0)
            def _():
                _row_copy(x_ref, j, xs_ref, p, sem).wait()
        return carry

    lax.fori_loop(0, t, issue, 0, unroll=8)
    lax.fori_loop(0, t, drain, 0, unroll=8)


def _dispatch(pos, xn, n_rows):
    n, d = xn.shape
    t = MOE_TILE
    grid_spec = pltpu.PrefetchScalarGridSpec(
        num_scalar_prefetch=1,
        grid=(n // t,),
        in_specs=[pl.BlockSpec((t, d), lambda i, pos: (i, 0)),
                  pl.BlockSpec(memory_space=pl.ANY)],
        out_specs=pl.BlockSpec(memory_space=pl.ANY),
        scratch_shapes=[pltpu.SemaphoreType.DMA],
    )
    return pl.pallas_call(
        _dispatch_kernel,
        out_shape=jax.ShapeDtypeStruct((n_rows, d), F32),
        grid_spec=grid_spec,
        input_output_aliases={2: 0},
        compiler_params=_cparams(1),
        name="moe_dispatch",
    )(pos, xn, jnp.zeros((n_rows, d), F32))


def _expert_kernel(te_ref, live_ref, x_ref, wg_ref, wu_ref, wd_ref, o_ref, xb_ref, acc_ref):
    del te_ref
    i, c = pl.program_id(0), pl.program_id(1)
    last = pl.num_programs(1) - 1
    live = live_ref[i] > 0

    @pl.when(live)
    def _():
        @pl.when(c == 0)
        def _():
            xb_ref[...] = x_ref[...].astype(BF16)

        xb = xb_ref[...]
        gate = jnp.dot(xb, wg_ref[0], preferred_element_type=F32)
        up = jnp.dot(xb, wu_ref[0], preferred_element_type=F32)
        hh = (gate * jax.nn.sigmoid(gate) * up).astype(BF16)
        part = jnp.dot(hh, wd_ref[0], preferred_element_type=F32)

        @pl.when(c == 0)
        def _():
            acc_ref[...] = part

        @pl.when(c > 0)
        def _():
            acc_ref[...] += part

        @pl.when(c == last)
        def _():
            o_ref[...] = acc_ref[...]

    @pl.when(jnp.logical_not(live) & (c == last))
    def _():
        o_ref[...] = jnp.zeros(o_ref.shape, o_ref.dtype)


def _experts(tile_expert, tile_live, xs, wg, wu, wd, *, n_chunks=2):
    n_rows, d = xs.shape
    tm = EXPERT_ROWS
    fc = wg.shape[2] // n_chunks
    chunk = lambda i, c, te, lv: jnp.where(lv[i] > 0, c, n_chunks - 1)
    grid_spec = pltpu.PrefetchScalarGridSpec(
        num_scalar_prefetch=2,
        grid=(n_rows // tm, n_chunks),
        in_specs=[pl.BlockSpec((tm, d), lambda i, c, te, lv: (i, 0)),
                  pl.BlockSpec((1, d, fc), lambda i, c, te, lv: (te[i], 0, chunk(i, c, te, lv))),
                  pl.BlockSpec((1, d, fc), lambda i, c, te, lv: (te[i], 0, chunk(i, c, te, lv))),
                  pl.BlockSpec((1, fc, d), lambda i, c, te, lv: (te[i], chunk(i, c, te, lv), 0))],
        out_specs=pl.BlockSpec((tm, d), lambda i, c, te, lv: (i, 0)),
        scratch_shapes=[pltpu.VMEM((tm, d), BF16), pltpu.VMEM((tm, d), F32)],
    )
    return pl.pallas_call(
        _expert_kernel,
        out_shape=jax.ShapeDtypeStruct((n_rows, d), F32),
        grid_spec=grid_spec,
        compiler_params=_cparams(2),
        name="moe_experts",
    )(tile_expert, tile_live, xs, wg, wu, wd)


def _combine_kernel(pos_ref, h_ref, rec_ref, ys_ref, o_ref, buf0, buf1, sem):
    t = h_ref.shape[0]
    base = pl.program_id(0) * t * 2
    bufs = (buf0, buf1)

    def issue(j, carry):
        for k in range(2):
            p = jnp.maximum(pos_ref[base + 2 * j + k], 0)
            _row_copy(ys_ref, p, bufs[k], j, sem).start()
        return carry

    def drain(j, carry):
        for k in range(2):
            _row_copy(ys_ref, 0, bufs[k], j, sem).wait()
        return carry

    lax.fori_loop(0, t, issue, 0, unroll=8)
    lax.fori_loop(0, t, drain, 0, unroll=8)
    rec = rec_ref[...]
    g1 = rec[:, R_GATE:R_GATE + 1]
    g2 = rec[:, R_GATE + 1:R_GATE + 2]
    o_ref[...] = h_ref[...] + g1 * buf0[...] + g2 * buf1[...]


def _combine(pos, h, rec, ys):
    n, d = h.shape
    t = MOE_TILE
    grid_spec = pltpu.PrefetchScalarGridSpec(
        num_scalar_prefetch=1,
        grid=(n // t,),
        in_specs=[pl.BlockSpec((t, d), lambda i, pos: (i, 0)),
                  pl.BlockSpec((t, LANE), lambda i, pos: (i, 0)),
                  pl.BlockSpec(memory_space=pl.ANY)],
        out_specs=pl.BlockSpec((t, d), lambda i, pos: (i, 0)),
        scratch_shapes=[pltpu.VMEM((t, d), F32), pltpu.VMEM((t, d), F32), pltpu.SemaphoreType.DMA],
    )
    return pl.pallas_call(
        _combine_kernel,
        out_shape=jax.ShapeDtypeStruct((n, d), F32),
        grid_spec=grid_spec,
        compiler_params=_cparams(1),
        name="moe_combine",
    )(pos, h, rec, ys)


def _norm_kernel(x_ref, g_ref, o_ref):
    o_ref[...] = _rms(x_ref[...], g_ref[...])


def _final_norm(h, g, *, tm=1024):
    n, d = h.shape
    return pl.pallas_call(
        _norm_kernel,
        out_shape=jax.ShapeDtypeStruct((n, d), F32),
        grid=(n // tm,),
        in_specs=[pl.BlockSpec((tm, d), lambda i: (i, 0)), pl.BlockSpec((1, d), lambda i: (0, 0))],
        out_specs=pl.BlockSpec((tm, d), lambda i: (i, 0)),
        compiler_params=_cparams(1),
        name="final_norm",
    )(h, g.reshape(1, d))


def _fft_row_tile(lp):
    best = 16
    for tk in range(16, lp + 1, 16):
        if lp % tk == 0 and tk * lp * 2 <= 4608 * 1024:
            best = tk
    return best


def _trunk(x, meta_tokens, prm):
    b, t_len, d = x.shape
    depth = prm["w_in"].shape[0]
    seq_len = N_META + t_len
    lp = -(-seq_len // LANE) * LANE
    n = b * lp
    rows = t_len // GRID_W
    assert t_len % GRID_W == 0 and rows >= NA_ROWS and n % MOE_TILE == 0

    meta = jnp.broadcast_to(meta_tokens.astype(x.dtype)[None], (b, N_META, d))
    h = jnp.concatenate([meta, x, jnp.zeros((b, lp - seq_len, d), x.dtype)], axis=1).reshape(n, d)

    cmat, smat = _twiddles(seq_len, lp)
    tk = _fft_row_tile(lp)
    tri = jnp.tril(jnp.ones((MOE_TILE, MOE_TILE), BF16), -1)

    for i in range(depth):
        z = _norm_matmul(h, prm["norm_mix"][i], prm["w_in"][i])
        z3 = z.reshape(b, lp, D_IN)
        ab = _matmul_cols(z, prm["chan_dft"], (3 * D_ATT) // D_FFT)
        a = _attention(z3, prm["attn_bias"][i], prm["attn_mbias"][i], rows=rows, seq_len=seq_len)
        f = _fourier(cmat, smat, ab.reshape(b, lp, 2 * D_FFT), prm["w_fft"][i], tk=tk)
        p = _pool(z3, prm["w_pool"][i], prm["pool_scale"][i], seq_len=seq_len)
        h = _outproj(a.reshape(n, D_ATT), f.reshape(n, D_FFT), p.reshape(n, D_POOL), h,
                     prm["norm_groups"][i], prm["w_out"][i])
        j = i // 2
        if i % 2 == 0:
            h = _ffn(h, prm["norm_ffn"][i], prm["w_ff_gate"][j], prm["w_ff_up"][j], prm["w_ff_down"][j])
        else:
            xn, rec, counts = _router(h, prm["norm_ffn"][i], prm["w_router"][j], tri,
                                      seq_len=seq_len, lp=lp)
            n_row_tiles = -(-2 * b * seq_len // EXPERT_ROWS) + N_EXPERTS
            pos, tile_expert, tile_live = _dispatch_plan(rec, counts, n_row_tiles=n_row_tiles)
            xs = _dispatch(pos, xn, n_row_tiles * EXPERT_ROWS)
            ys = _experts(tile_expert, tile_live, xs,
                          prm["w_exp_gate"][j], prm["w_exp_up"][j], prm["w_exp_down"][j])
            h = _combine(pos, h, rec, ys)
    y = _final_norm(h, prm["norm_final"])
    return y.reshape(b, lp, d)[:, N_META:seq_len]


def kernel(x_prompt, x_sample, meta_tokens, norm_mix, w_in, w_fft, w_pool, pool_scale, rel_bias, meta_bias,
           norm_groups, w_out, norm_ffn, w_ff_gate, w_ff_up, w_ff_down, w_router, w_exp_gate, w_exp_up,
           w_exp_down, norm_final):
    depth = w_in.shape[0]
    tabs = [_attn_bias_tables(rel_bias[i], meta_bias[i]) for i in range(depth)]
    prm = {
        "norm_mix": norm_mix, "norm_groups": norm_groups, "norm_ffn": norm_ffn, "norm_final": norm_final,
        "pool_scale": pool_scale,
        "w_in": w_in.astype(BF16), "w_out": w_out.astype(BF16),
        "w_ff_gate": w_ff_gate.astype(BF16), "w_ff_up": w_ff_up.astype(BF16), "w_ff_down": w_ff_down.astype(BF16),
        "w_exp_gate": w_exp_gate.astype(BF16), "w_exp_up": w_exp_up.astype(BF16),
        "w_exp_down": w_exp_down.astype(BF16),
        "w_router": jnp.pad(w_router, ((0, 0), (0, 0), (0, LANE - N_EXPERTS))),
        "w_fft": jnp.stack([_block_diag(w_fft[i]) for i in range(depth)]).astype(BF16),
        "w_pool": jnp.stack([_block_diag(w_pool[i]) for i in range(depth)]).astype(BF16),
        "chan_dft": _channel_dft_matrix(),
        "attn_bias": [tb[0] for tb in tabs], "attn_mbias": [tb[1] for tb in tabs],
    }
    y_prompt = _trunk(x_prompt, meta_tokens, prm)
    y_sample = _trunk(x_sample, meta_tokens, prm)
    return (y_prompt, y_sample)
```

```python
import functools
import math

import numpy as np
import jax
import jax.numpy as jnp
from jax import lax
from jax.experimental import pallas as pl
from jax.experimental.pallas import tpu as pltpu

D_MODEL = 1024
N_META = 16
GRID_W = 64
D_HEAD = 64
D_ATT = 512
N_ATT_HEADS = 8
D_FFT = 256
N_FFT_HEADS = 4
D_FFT_HEAD = 64
D_POOL = 256
POOL_WINDOWS = (2, 4, 8, 16)
D_POOL_GROUP = 64
D_IN = 2048
NA_ROWS = 8
NA_COLS = 16
N_EXPERTS = 8
EPS = 1e-6

LANE = 128
NEG_BIG = -1e30
VMEM_LIMIT = 56 * 1024 * 1024

F32 = jnp.float32
BF16 = jnp.bfloat16


def _cparams(n_axes, vmem=VMEM_LIMIT):
    return pltpu.CompilerParams(dimension_semantics=("arbitrary",) * n_axes,
                                vmem_limit_bytes=vmem)


def _rms(x, g):
    return x * lax.rsqrt(jnp.mean(x * x, axis=-1, keepdims=True) + EPS) * g


def _norm_matmul_kernel(x_ref, g_ref, w_ref, o_ref, xn_ref, *, tn):
    xn_ref[...] = _rms(x_ref[...], g_ref[...]).astype(BF16)
    for n0 in range(0, o_ref.shape[1], tn):
        o_ref[:, n0:n0 + tn] = jnp.dot(xn_ref[...], w_ref[:, n0:n0 + tn],
                                       preferred_element_type=F32).astype(o_ref.dtype)


def _norm_matmul(x, g, w, *, tm=1024, tn=512):
    n, d = x.shape
    n_out = w.shape[1]
    return pl.pallas_call(
        functools.partial(_norm_matmul_kernel, tn=tn),
        out_shape=jax.ShapeDtypeStruct((n, n_out), BF16),
        grid=(n // tm,),
        in_specs=[pl.BlockSpec((tm, d), lambda i: (i, 0)),
                  pl.BlockSpec((1, d), lambda i: (0, 0)),
                  pl.BlockSpec((d, n_out), lambda i: (0, 0))],
        out_specs=pl.BlockSpec((tm, n_out), lambda i: (i, 0)),
        scratch_shapes=[pltpu.VMEM((tm, d), BF16)],
        compiler_params=_cparams(1),
        name="norm_matmul",
    )(x, g.reshape(1, d), w)


def _matmul_kernel(x_ref, w_ref, o_ref):
    o_ref[...] = jnp.dot(x_ref[...], w_ref[...], preferred_element_type=F32).astype(o_ref.dtype)


def _matmul_cols(x, w, col_block, *, tm=1024):
    n = x.shape[0]
    k, n_out = w.shape
    return pl.pallas_call(
        _matmul_kernel,
        out_shape=jax.ShapeDtypeStruct((n, n_out), BF16),
        grid=(n // tm,),
        in_specs=[pl.BlockSpec((tm, k), lambda i: (i, col_block)),
                  pl.BlockSpec((k, n_out), lambda i: (0, 0))],
        out_specs=pl.BlockSpec((tm, n_out), lambda i: (i, 0)),
        compiler_params=_cparams(1),
        name="channel_dft",
    )(x, w)


def _attn_kernel(q_ref, k_ref, v_ref, bias_ref, mb_ref, o_ref, smq_ref, s_a, s_b, s_c, s_d,
                 p_a, p_b, p_c, p_d, l_a, l_b, l_c, l_d, *, rows, seq_len):
    lp = o_ref.shape[1]
    n_win = NA_ROWS * GRID_W
    lane = lax.broadcasted_iota(jnp.int32, (1, LANE), 1)
    head0 = lane < D_HEAD
    nt = (((1,), (1,)), ((), ()))

    km16 = k_ref[0, 0:N_META, :]
    zero16 = jnp.zeros_like(km16)
    km = jnp.concatenate([jnp.where(head0, km16, zero16), jnp.where(head0, zero16, km16),
                          k_ref[0, 2 * N_META:LANE, :]], axis=0)
    vm = jnp.concatenate([v_ref[0, 0:N_META, :], v_ref[0, 0:N_META, :], v_ref[0, 2 * N_META:LANE, :]], axis=0)
    mb = mb_ref[0]

    def stack(q):
        q = q * jnp.asarray(D_HEAD ** -0.5, q.dtype)
        zero = jnp.zeros_like(q)
        return jnp.concatenate([jnp.where(head0, q, zero), jnp.where(head0, zero, q)], axis=0)

    def unstack(o, n):
        return jnp.where(head0, o[0:n], o[n:2 * n])

    q_all = q_ref[0] * jnp.asarray(D_HEAD ** -0.5, q_ref.dtype)
    smq_ref[...] = lax.dot_general(q_all, km, nt, preferred_element_type=F32)

    def meta_scores(q0, n, bias):
        blk = smq_ref[pl.ds(q0, n), :]
        return jnp.concatenate([blk, blk], axis=0) + bias

    sm = meta_scores(0, N_META, jnp.concatenate([mb[0:N_META], mb[GRID_W:GRID_W + N_META]], axis=0))
    pm = jnp.exp(sm - jnp.max(sm, axis=-1, keepdims=True))
    om = jnp.dot(pm.astype(BF16), vm, preferred_element_type=F32)
    om = om / jnp.sum(pm, axis=-1, keepdims=True)
    o_ref[0, 0:N_META, :] = unstack(om, N_META).astype(o_ref.dtype)

    def window(t):
        t = jnp.minimum(t, rows - 1)
        rs = jnp.clip(t - NA_ROWS // 2, 0, rows - NA_ROWS)
        q0 = pl.multiple_of(N_META + t * GRID_W, 16)
        k0 = pl.multiple_of(N_META + rs * GRID_W, 16)
        return q0, k0, t - rs

    def scores(t, s_ref):
        q0, k0, off = window(t)
        qs = stack(q_ref[0, pl.ds(q0, GRID_W), :])
        kw = k_ref[0, pl.ds(k0, n_win), :]
        s_ref[...] = lax.dot_general(qs, kw, nt, preferred_element_type=F32) + bias_ref[0, off]

    dyn_zero = pl.multiple_of(jnp.minimum(pl.program_id(0), 0), LANE)

    def lane_tiles(x):
        return [x[:, i:i + LANE] for i in range(0, x.shape[1], LANE)]

    def softmax(t, s_ref, p_ref, l_ref):
        q0, _, _ = window(t)
        s = s_ref[pl.ds(dyn_zero, LANE), :]
        sm = meta_scores(q0, GRID_W, mb)
        m = jnp.max(functools.reduce(jnp.maximum, lane_tiles(s) + [sm]), axis=-1, keepdims=True)
        p = jnp.exp(s - m)
        pm = jnp.exp(sm - m)
        l_ref[...] = jnp.sum(functools.reduce(jnp.add, lane_tiles(p) + [pm]), axis=-1, keepdims=True)
        p_ref[:, 0:n_win] = p.astype(BF16)
        p_ref[:, n_win:] = pm.astype(BF16)

    def values(t, p_ref, l_ref):
        q0, k0, _ = window(t)
        vw = v_ref[0, pl.ds(k0, n_win), :]
        o = (jnp.dot(p_ref[:, 0:n_win], vw, preferred_element_type=F32)
             + jnp.dot(p_ref[:, n_win:], vm, preferred_element_type=F32))
        o = o / l_ref[...]
        o_ref[0, pl.ds(q0, GRID_W), :] = unstack(o, GRID_W).astype(o_ref.dtype)

    def step(t, s_in, s_out, pl_in, pl_out):
        scores(t + 4, s_out[0])
        scores(t + 5, s_out[1])
        values(t, *pl_in[0])
        values(t + 1, *pl_in[1])
        softmax(t + 2, s_in[0], *pl_out[0])
        softmax(t + 3, s_in[1], *pl_out[1])

    set_0 = ((p_a, l_a), (p_b, l_b))
    set_1 = ((p_c, l_c), (p_d, l_d))
    scores(0, s_c)
    scores(1, s_d)
    softmax(0, s_c, *set_0[0])
    softmax(1, s_d, *set_0[1])
    scores(2, s_a)
    scores(3, s_b)

    def quad_body(u, carry):
        t = 4 * u
        step(t, (s_a, s_b), (s_c, s_d), set_0, set_1)
        step(t + 2, (s_c, s_d), (s_a, s_b), set_1, set_0)
        return carry

    lax.fori_loop(0, rows // 4, quad_body, 0)
    if lp > seq_len:
        o_ref[0, seq_len:lp, :] = jnp.zeros((lp - seq_len, LANE), o_ref.dtype)


def _attention(z3, bias_tab, mb_tab, *, rows, seq_len):
    b, lp, _ = z3.shape
    assert rows % 4 == 0
    n_pairs = N_ATT_HEADS // 2
    n_keys = NA_ROWS * GRID_W
    blk = lambda off: pl.BlockSpec((1, lp, LANE), lambda hp, bi: (bi, 0, off + hp))
    scratch = ([pltpu.VMEM((lp, LANE), F32)]
               + [pltpu.VMEM((LANE, n_keys), F32) for _ in range(4)]
               + [pltpu.VMEM((LANE, n_keys + LANE), BF16) for _ in range(4)]
               + [pltpu.VMEM((LANE, 1), F32) for _ in range(4)])
    return pl.pallas_call(
        functools.partial(_attn_kernel, rows=rows, seq_len=seq_len),
        out_shape=jax.ShapeDtypeStruct((b, lp, D_ATT), BF16),
        grid=(n_pairs, b),
        in_specs=[blk(0), blk(n_pairs), blk(2 * n_pairs),
                  pl.BlockSpec((1, NA_ROWS, LANE, n_keys), lambda hp, bi: (hp, 0, 0, 0)),
                  pl.BlockSpec((1, LANE, LANE), lambda hp, bi: (hp, 0, 0))],
        out_specs=pl.BlockSpec((1, lp, LANE), lambda hp, bi: (bi, 0, hp)),
        scratch_shapes=scratch,
        compiler_params=_cparams(2),
        name="nbr_attention",
    )(z3, z3, z3, bias_tab, mb_tab)


def _attn_bias_tables(rel_bias, meta_bias):
    h, n_dr, n_dc = rel_bias.shape
    c = np.arange(GRID_W)[:, None]
    j = np.arange(GRID_W)[None, :]
    cs = np.clip(c - NA_COLS // 2, 0, GRID_W - NA_COLS)
    valid = (j >= cs) & (j < cs + NA_COLS)
    ext = jnp.zeros((h, n_dr, 2 * GRID_W), F32)
    ext = lax.dynamic_update_slice(ext, rel_bias.astype(F32), (0, 0, GRID_W - NA_COLS))
    skew = jnp.tile(ext, (1, 1, GRID_W))[:, :, :GRID_W * (2 * GRID_W - 1)]
    toep = skew.reshape(h, n_dr, GRID_W, 2 * GRID_W - 1)[..., GRID_W - 1:]
    toep = jnp.where(valid[None, None], toep, NEG_BIG)
    t = jnp.stack([toep[:, NA_ROWS - 1 - oi:2 * NA_ROWS - 1 - oi] for oi in range(NA_ROWS)], axis=1)
    t = t.transpose(0, 1, 3, 2, 4)
    t = t.reshape(h // 2, 2, NA_ROWS, GRID_W, NA_ROWS * GRID_W)
    t = t.transpose(0, 2, 1, 3, 4).reshape(h // 2, NA_ROWS, 2 * GRID_W, NA_ROWS * GRID_W)
    mbp = meta_bias.astype(F32).reshape(h // 2, 2, N_META)
    neg = jnp.full((h // 2, N_META), NEG_BIG, F32)
    mb0 = jnp.concatenate([mbp[:, 0], neg], axis=-1)
    mb1 = jnp.concatenate([neg, mbp[:, 1]], axis=-1)
    mb = jnp.stack([mb0, mb1], axis=1)
    mb = jnp.pad(mb, ((0, 0), (0, 0), (0, LANE - 2 * N_META)), constant_values=NEG_BIG)
    mb = jnp.broadcast_to(mb[:, :, None, :], (h // 2, 2, GRID_W, LANE))
    return t, mb.reshape(h // 2, 2 * GRID_W, LANE)


def _fft_kernel(c_ref, s_ref, ab_ref, w_ref, o_ref):
    a = ab_ref[0, :, 0:D_FFT]
    b = ab_ref[0, :, D_FFT:2 * D_FFT]
    f = (jnp.dot(c_ref[...], a, preferred_element_type=F32)
         + jnp.dot(s_ref[...], b, preferred_element_type=F32))
    o_ref[0] = jnp.dot(f.astype(BF16), w_ref[...], preferred_element_type=F32).astype(o_ref.dtype)


def _fourier(cmat, smat, ab3, w_bd, *, tk):
    b, lp, _ = ab3.shape
    return pl.pallas_call(
        _fft_kernel,
        out_shape=jax.ShapeDtypeStruct((b, lp, D_FFT), BF16),
        grid=(lp // tk, b),
        in_specs=[pl.BlockSpec((tk, lp), lambda j, bi: (j, 0)),
                  pl.BlockSpec((tk, lp), lambda j, bi: (j, 0)),
                  pl.BlockSpec((1, lp, 2 * D_FFT), lambda j, bi: (bi, 0, 0)),
                  pl.BlockSpec((D_FFT, D_FFT), lambda j, bi: (0, 0))],
        out_specs=pl.BlockSpec((1, tk, D_FFT), lambda j, bi: (bi, j, 0)),
        compiler_params=_cparams(2),
        name="fourier_mix",
    )(cmat, smat, ab3, w_bd)


def _twiddles(seq_len, lp):
    theta = 2.0 * math.pi / seq_len
    k = jnp.arange(lp, dtype=jnp.int32)[:, None]
    t1 = (GRID_W * jnp.arange(lp // GRID_W, dtype=jnp.int32))[None, :]
    t0 = jnp.arange(GRID_W, dtype=jnp.int32)[None, :]
    ang_a = ((k * t1) % seq_len).astype(F32) * theta
    ang_b = ((k * t0) % seq_len).astype(F32) * theta
    ca, sa = jnp.cos(ang_a)[:, :, None], jnp.sin(ang_a)[:, :, None]
    cb, sb = jnp.cos(ang_b)[:, None, :], jnp.sin(ang_b)[:, None, :]
    cmat = (ca * cb - sa * sb).reshape(lp, lp)
    smat = (sa * cb + ca * sb).reshape(lp, lp)
    idx = jnp.arange(lp)
    valid = (idx[:, None] < seq_len) & (idx[None, :] < seq_len)
    scale = seq_len ** -0.5
    cmat = jnp.where(valid, cmat * scale, 0.0).astype(BF16)
    smat = jnp.where(valid, -smat * scale, 0.0).astype(BF16)
    return cmat, smat


def _channel_dft_matrix():
    c = np.arange(D_FFT_HEAD)
    ang = 2.0 * np.pi * ((c[:, None] * c[None, :]) % D_FFT_HEAD) / D_FFT_HEAD
    cc = np.cos(ang) / math.sqrt(D_FFT_HEAD)
    sc = np.sin(ang) / math.sqrt(D_FFT_HEAD)
    out = np.zeros((D_FFT, 2 * D_FFT), np.float32)
    for g in range(N_FFT_HEADS):
        sl = slice(g * D_FFT_HEAD, (g + 1) * D_FFT_HEAD)
        out[sl, sl] = cc
        out[sl, D_FFT + g * D_FFT_HEAD:D_FFT + (g + 1) * D_FFT_HEAD] = sc
    return jnp.asarray(out, BF16)


def _block_diag(w):
    g, c, e = w.shape
    out = jnp.zeros((g * c, g * e), w.dtype)
    for i in range(g):
        out = lax.dynamic_update_slice(out, w[i], (i * c, i * e))
    return out


POOL_PAD = 16


def _pool_kernel(u_ref, w_ref, sc_ref, o_ref, s0, s1, s2, s3, s4, *, seq_len):
    lp = o_ref.shape[1]
    r_tot = lp + 2 * POOL_PAD
    lo, hi = 8, r_tot - 8
    row = lax.broadcasted_iota(jnp.int32, (lp, 1), 0)
    live = row < seq_len
    x = jnp.where(live, u_ref[0].astype(F32), 0.0)

    zeros_pad = jnp.zeros((POOL_PAD, D_POOL), F32)
    s0[0:POOL_PAD, :] = zeros_pad
    s0[POOL_PAD:POOL_PAD + lp, :] = x
    s0[POOL_PAD + lp:r_tot, :] = zeros_pad
    for s in (s1, s2, s3, s4):
        s[0:POOL_PAD, :] = zeros_pad
        s[POOL_PAD + lp:r_tot, :] = zeros_pad
    s1[lo:hi, :] = s0[lo - 1:hi - 1, :] + s0[lo:hi, :]
    s2[lo:hi, :] = s1[lo - 1:hi - 1, :] + s1[lo + 1:hi + 1, :]
    s3[lo:hi, :] = s2[lo - 2:hi - 2, :] + s2[lo + 2:hi + 2, :]
    s4[lo:hi, :] = s3[lo - 4:hi - 4, :] + s3[lo + 4:hi + 4, :]

    lane = lax.broadcasted_iota(jnp.int32, (1, D_POOL), 1)
    group = jnp.right_shift(lane, 6)
    half = jnp.where(group == 0, 1, jnp.where(group == 1, 2, jnp.where(group == 2, 4, 8)))
    cnt = jnp.minimum(row + half, seq_len) - jnp.maximum(row - half, 0)
    cnt = jnp.maximum(cnt, 1).astype(F32)
    sl = slice(POOL_PAD, POOL_PAD + lp)
    wsum = jnp.where(group == 0, s1[sl, :],
                     jnp.where(group == 1, s2[sl, :], jnp.where(group == 2, s3[sl, :], s4[sl, :])))
    p = jnp.where(live, wsum / cnt - x, 0.0)
    y = jnp.dot(p.astype(BF16), w_ref[...], preferred_element_type=F32) * sc_ref[...]
    o_ref[0] = y.astype(o_ref.dtype)


def _pool(z3, w_bd, scale, *, seq_len):
    b, lp, d_in = z3.shape
    col_block = (d_in - D_POOL) // D_POOL
    scratch = [pltpu.VMEM((lp + 2 * POOL_PAD, D_POOL), F32) for _ in range(5)]
    return pl.pallas_call(
        functools.partial(_pool_kernel, seq_len=seq_len),
        out_shape=jax.ShapeDtypeStruct((b, lp, D_POOL), BF16),
        grid=(b,),
        in_specs=[pl.BlockSpec((1, lp, D_POOL), lambda bi: (bi, 0, col_block)),
                  pl.BlockSpec((D_POOL, D_POOL), lambda bi: (0, 0)),
                  pl.BlockSpec((1, D_POOL), lambda bi: (0, 0))],
        out_specs=pl.BlockSpec((1, lp, D_POOL), lambda bi: (bi, 0, 0)),
        scratch_shapes=scratch,
        compiler_params=_cparams(1),
        name="pool_mix",
    )(z3, w_bd, scale.reshape(1, D_POOL))


def _outproj_kernel(a_ref, f_ref, p_ref, h_ref, g_ref, w_ref, o_ref):
    an = _rms(a_ref[...].astype(F32), g_ref[:, 0:D_ATT]).astype(BF16)
    fn = _rms(f_ref[...].astype(F32), g_ref[:, D_ATT:D_ATT + D_FFT]).astype(BF16)
    pn = _rms(p_ref[...].astype(F32), g_ref[:, D_ATT + D_FFT:]).astype(BF16)
    acc = jnp.dot(an, w_ref[0:D_ATT, :], preferred_element_type=F32)
    acc += jnp.dot(fn, w_ref[D_ATT:D_ATT + D_FFT, :], preferred_element_type=F32)
    acc += jnp.dot(pn, w_ref[D_ATT + D_FFT:, :], preferred_element_type=F32)
    o_ref[...] = h_ref[...] + acc


def _outproj(a, f, p, h, g, w, *, tm=1024):
    n, d = h.shape
    row = lambda width: pl.BlockSpec((tm, width), lambda i: (i, 0))
    return pl.pallas_call(
        _outproj_kernel,
        out_shape=jax.ShapeDtypeStruct((n, d), F32),
        grid=(n // tm,),
        in_specs=[row(D_ATT), row(D_FFT), row(D_POOL), row(d),
                  pl.BlockSpec((1, d), lambda i: (0, 0)),
                  pl.BlockSpec((d, d), lambda i: (0, 0))],
        out_specs=row(d),
        compiler_params=_cparams(1),
        name="out_proj",
    )(a, f, p, h, g.reshape(1, d), w)


def _ffn_kernel(h_ref, g_ref, wg_ref, wu_ref, wd_ref, o_ref, xn_ref, acc_ref):
    c = pl.program_id(1)

    @pl.when(c == 0)
    def _():
        xn_ref[...] = _rms(h_ref[...], g_ref[...]).astype(BF16)

    xn = xn_ref[...]
    gate = jnp.dot(xn, wg_ref[...], preferred_element_type=F32)
    up = jnp.dot(xn, wu_ref[...], preferred_element_type=F32)
    hh = (gate * jax.nn.sigmoid(gate) * up).astype(BF16)
    part = jnp.dot(hh, wd_ref[...], preferred_element_type=F32)

    @pl.when(c == 0)
    def _():
        acc_ref[...] = part

    @pl.when(c > 0)
    def _():
        acc_ref[...] += part

    @pl.when(c == pl.num_programs(1) - 1)
    def _():
        o_ref[...] = h_ref[...] + acc_ref[...]


def _ffn(h, g, wg, wu, wd, *, tm=512, n_chunks=2):
    n, d = h.shape
    d_ff = wg.shape[1]
    fc = d_ff // n_chunks
    return pl.pallas_call(
        _ffn_kernel,
        out_shape=jax.ShapeDtypeStruct((n, d), F32),
        grid=(n // tm, n_chunks),
        in_specs=[pl.BlockSpec((tm, d), lambda i, c: (i, 0)),
                  pl.BlockSpec((1, d), lambda i, c: (0, 0)),
                  pl.BlockSpec((d, fc), lambda i, c: (0, c)),
                  pl.BlockSpec((d, fc), lambda i, c: (0, c)),
                  pl.BlockSpec((fc, d), lambda i, c: (c, 0))],
        out_specs=pl.BlockSpec((tm, d), lambda i, c: (i, 0)),
        scratch_shapes=[pltpu.VMEM((tm, d), BF16), pltpu.VMEM((tm, d), F32)],
        compiler_params=_cparams(2),
        name="swiglu_ffn",
    )(h, g.reshape(1, d), wg, wu, wd)


MOE_TILE = 1024
EXPERT_ROWS = 512
R_GATE, R_EXPERT, R_RANK = 0, 2, 4


def _router_kernel(h_ref, g_ref, wr_ref, tri_ref, xn_ref, rec_ref, cnt_ref, *, seq_len, lp):
    t = h_ref.shape[0]
    y = _rms(h_ref[...], g_ref[...])
    xn_ref[...] = y
    logits = jnp.dot(y, wr_ref[...], preferred_element_type=F32, precision=lax.Precision.HIGHEST)
    lane = lax.broadcasted_iota(jnp.int32, (t, LANE), 1)
    lg = jnp.where(lane < N_EXPERTS, logits, -jnp.inf)
    m1 = jnp.max(lg, axis=-1, keepdims=True)
    i1 = jnp.min(jnp.where(lg == m1, lane, LANE), axis=-1, keepdims=True)
    lg2 = jnp.where(lane == i1, -jnp.inf, lg)
    m2 = jnp.max(lg2, axis=-1, keepdims=True)
    i2 = jnp.min(jnp.where(lg2 == m2, lane, LANE), axis=-1, keepdims=True)
    e2 = jnp.exp(m2 - m1)
    g1 = 1.0 / (1.0 + e2)
    g2 = e2 / (1.0 + e2)
    rowf = (pl.program_id(0) * t + lax.broadcasted_iota(jnp.int32, (t, 1), 0)).astype(F32)
    seq = jnp.floor((rowf + 0.5) * (1.0 / lp))
    live = (rowf - seq * lp) < seq_len
    first = lane == i1
    second = lane == i2
    member = jnp.where((first | second) & live, 1.0, 0.0)
    rank = jnp.dot(tri_ref[...], member.astype(BF16), preferred_element_type=F32)
    r1 = jnp.sum(jnp.where(first, rank, 0.0), axis=-1, keepdims=True)
    r2 = jnp.sum(jnp.where(second, rank, 0.0), axis=-1, keepdims=True)
    dead = jnp.logical_not(live)
    rec = jnp.zeros((t, LANE), F32)
    for ln, val in ((R_GATE, jnp.where(dead, 0.0, g1)), (R_GATE + 1, jnp.where(dead, 0.0, g2)),
                    (R_EXPERT, jnp.where(dead, -1.0, i1.astype(F32))),
                    (R_EXPERT + 1, jnp.where(dead, -1.0, i2.astype(F32))),
                    (R_RANK, r1), (R_RANK + 1, r2)):
        rec = jnp.where(lane == ln, val, rec)
    rec_ref[...] = rec
    cnt_ref[0] = jnp.broadcast_to(jnp.sum(member, axis=0, keepdims=True), (8, LANE))


def _router(h, g, w_router_pad, tri, *, seq_len, lp):
    n, d = h.shape
    t = MOE_TILE
    row = lambda width: pl.BlockSpec((t, width), lambda i: (i, 0))
    return pl.pallas_call(
        functools.partial(_router_kernel, seq_len=seq_len, lp=lp),
        out_shape=(jax.ShapeDtypeStruct((n, d), F32),
                   jax.ShapeDtypeStruct((n, LANE), F32),
                   jax.ShapeDtypeStruct((n // t, 8, LANE), F32)),
        grid=(n // t,),
        in_specs=[row(d),
                  pl.BlockSpec((1, d), lambda i: (0, 0)),
                  pl.BlockSpec((d, LANE), lambda i: (0, 0)),
                  pl.BlockSpec((t, t), lambda i: (0, 0))],
        out_specs=(row(d), row(LANE), pl.BlockSpec((1, 8, LANE), lambda i: (i, 0, 0))),
        compiler_params=_cparams(1),
        name="moe_router",
    )(h, g.reshape(1, d), w_router_pad, tri)


def _dispatch_plan(rec, counts, *, n_row_tiles):
    n = rec.shape[0]
    n_tiles = counts.shape[0]
    cnt = counts[:, 0, :N_EXPERTS]
    before = jnp.cumsum(cnt, axis=0) - cnt
    total = jnp.sum(cnt, axis=0)
    padded = jnp.ceil(total / EXPERT_ROWS) * EXPERT_ROWS
    ends = jnp.cumsum(padded)
    start = ends - padded
    base = (start[None, :] + before)[:, None, :]
    slots = []
    for k in range(2):
        e = rec[:, R_EXPERT + k].reshape(n_tiles, MOE_TILE, 1)
        r = rec[:, R_RANK + k].reshape(n_tiles, MOE_TILE)
        hit = e == jnp.arange(N_EXPERTS, dtype=F32)[None, None, :]
        slot = jnp.sum(jnp.where(hit, base, 0.0), axis=-1) + r
        slots.append(jnp.where(e[..., 0] < 0, -1.0, slot))
    pos = jnp.stack(slots, axis=-1).astype(jnp.int32).reshape(n * 2)
    tile_row0 = jnp.arange(n_row_tiles, dtype=F32) * EXPERT_ROWS
    tile_expert = jnp.sum(tile_row0[:, None] >= ends[None, :], axis=-1)
    tile_live = (tile_row0 < ends[-1]).astype(jnp.int32)
    tile_expert = jnp.minimum(tile_expert, N_EXPERTS - 1).astype(jnp.int32)
    return pos, tile_expert, tile_live


def _row_copy(src_ref, src_row, dst_ref, dst_row, sem):
    return pltpu.make_async_copy(src_ref.at[pl.ds(src_row, 1)], dst_ref.at[pl.ds(dst_row, 1)], sem)


def _dispatch_kernel(pos_ref, x_ref, zero_ref, xs_ref, sem):
    del zero_ref
    t = x_ref.shape[0]
    base = pl.program_id(0) * t * 2

    def issue(j, carry):
        for k in range(2):
            p = pos_ref[base + 2 * j + k]

            @pl.when(p >= 0)
            def _():
                _row_copy(x_ref, j, xs_ref, p, sem).start(priority=k)
        return carry

    def drain(j, carry):
        for k in range(2):
            p = pos_ref[base + 2 * j + k]

            @pl.when(p >= 0)
            def _():
                _row_copy(x_ref, j, xs_ref, p, sem).wait()
        return carry

    lax.fori_loop(0, t, issue, 0, unroll=8)
    lax.fori_loop(0, t, drain, 0, unroll=8)


def _dispatch(pos, xn, n_rows):
    n, d = xn.shape
    t = MOE_TILE
    grid_spec = pltpu.PrefetchScalarGridSpec(
        num_scalar_prefetch=1,
        grid=(n // t,),
        in_specs=[pl.BlockSpec((t, d), lambda i, pos: (i, 0)),
                  pl.BlockSpec(memory_space=pl.ANY)],
        out_specs=pl.BlockSpec(memory_space=pl.ANY),
        scratch_shapes=[pltpu.SemaphoreType.DMA],
    )
    return pl.pallas_call(
        _dispatch_kernel,
        out_shape=jax.ShapeDtypeStruct((n_rows, d), F32),
        grid_spec=grid_spec,
        input_output_aliases={2: 0},
        compiler_params=_cparams(1),
        name="moe_dispatch",
    )(pos, xn, jnp.zeros((n_rows, d), F32))


def _expert_kernel(te_ref, live_ref, x_ref, wg_ref, wu_ref, wd_ref, o_ref, xb_ref, acc_ref):
    del te_ref
    i, c = pl.program_id(0), pl.program_id(1)
    last = pl.num_programs(1) - 1
    live = live_ref[i] > 0

    @pl.when(live)
    def _():
        @pl.when(c == 0)
        def _():
            xb_ref[...] = x_ref[...].astype(BF16)

        xb = xb_ref[...]
        gate = jnp.dot(xb, wg_ref[0], preferred_element_type=F32)
        up = jnp.dot(xb, wu_ref[0], preferred_element_type=F32)
        hh = (gate * jax.nn.sigmoid(gate) * up).astype(BF16)
        part = jnp.dot(hh, wd_ref[0], preferred_element_type=F32)

        @pl.when(c == 0)
        def _():
            acc_ref[...] = part

        @pl.when(c > 0)
        def _():
            acc_ref[...] += part

        @pl.when(c == last)
        def _():
            o_ref[...] = acc_ref[...]

    @pl.when(jnp.logical_not(live) & (c == last))
    def _():
        o_ref[...] = jnp.zeros(o_ref.shape, o_ref.dtype)


def _experts(tile_expert, tile_live, xs, wg, wu, wd, *, n_chunks=2):
    n_rows, d = xs.shape
    tm = EXPERT_ROWS
    fc = wg.shape[2] // n_chunks
    chunk = lambda i, c, te, lv: jnp.where(lv[i] > 0, c, n_chunks - 1)
    grid_spec = pltpu.PrefetchScalarGridSpec(
        num_scalar_prefetch=2,
        grid=(n_rows // tm, n_chunks),
        in_specs=[pl.BlockSpec((tm, d), lambda i, c, te, lv: (i, 0)),
                  pl.BlockSpec((1, d, fc), lambda i, c, te, lv: (te[i], 0, chunk(i, c, te, lv))),
                  pl.BlockSpec((1, d, fc), lambda i, c, te, lv: (te[i], 0, chunk(i, c, te, lv))),
                  pl.BlockSpec((1, fc, d), lambda i, c, te, lv: (te[i], chunk(i, c, te, lv), 0))],
        out_specs=pl.BlockSpec((tm, d), lambda i, c, te, lv: (i, 0)),
        scratch_shapes=[pltpu.VMEM((tm, d), BF16), pltpu.VMEM((tm, d), F32)],
    )
    return pl.pallas_call(
        _expert_kernel,
        out_shape=jax.ShapeDtypeStruct((n_rows, d), F32),
        grid_spec=grid_spec,
        compiler_params=_cparams(2),
        name="moe_experts",
    )(tile_expert, tile_live, xs, wg, wu, wd)


def _combine_kernel(pos_ref, h_ref, rec_ref, ys_ref, o_ref, buf0, buf1, sem):
    t = h_ref.shape[0]
    base = pl.program_id(0) * t * 2
    bufs = (buf0, buf1)

    def issue(j, carry):
        for k in range(2):
            p = jnp.maximum(pos_ref[base + 2 * j + k], 0)
            _row_copy(ys_ref, p, bufs[k], j, sem).start(priority=k)
        return carry

    def drain(j, carry):
        for k in range(2):
            _row_copy(ys_ref, 0, bufs[k], j, sem).wait()
        return carry

    lax.fori_loop(0, t, issue, 0, unroll=8)
    lax.fori_loop(0, t, drain, 0, unroll=8)
    rec = rec_ref[...]
    g1 = rec[:, R_GATE:R_GATE + 1]
    g2 = rec[:, R_GATE + 1:R_GATE + 2]
    o_ref[...] = h_ref[...] + g1 * buf0[...] + g2 * buf1[...]


def _combine(pos, h, rec, ys):
    n, d = h.shape
    t = MOE_TILE
    grid_spec = pltpu.PrefetchScalarGridSpec(
        num_scalar_prefetch=1,
        grid=(n // t,),
        in_specs=[pl.BlockSpec((t, d), lambda i, pos: (i, 0)),
                  pl.BlockSpec((t, LANE), lambda i, pos: (i, 0)),
                  pl.BlockSpec(memory_space=pl.ANY)],
        out_specs=pl.BlockSpec((t, d), lambda i, pos: (i, 0)),
        scratch_shapes=[pltpu.VMEM((t, d), F32), pltpu.VMEM((t, d), F32), pltpu.SemaphoreType.DMA],
    )
    return pl.pallas_call(
        _combine_kernel,
        out_shape=jax.ShapeDtypeStruct((n, d), F32),
        grid_spec=grid_spec,
        compiler_params=_cparams(1),
        name="moe_combine",
    )(pos, h, rec, ys)


def _norm_kernel(x_ref, g_ref, o_ref):
    o_ref[...] = _rms(x_ref[...], g_ref[...])


def _final_norm(h, g, *, tm=1024):
    n, d = h.shape
    return pl.pallas_call(
        _norm_kernel,
        out_shape=jax.ShapeDtypeStruct((n, d), F32),
        grid=(n // tm,),
        in_specs=[pl.BlockSpec((tm, d), lambda i: (i, 0)), pl.BlockSpec((1, d), lambda i: (0, 0))],
        out_specs=pl.BlockSpec((tm, d), lambda i: (i, 0)),
        compiler_params=_cparams(1),
        name="final_norm",
    )(h, g.reshape(1, d))


def _fft_row_tile(lp):
    best = 16
    for tk in range(16, lp + 1, 16):
        if lp % tk == 0 and tk * lp * 2 <= 4608 * 1024:
            best = tk
    return best


def _trunk(x, meta_tokens, prm):
    b, t_len, d = x.shape
    depth = prm["w_in"].shape[0]
    seq_len = N_META + t_len
    lp = -(-seq_len // LANE) * LANE
    n = b * lp
    rows = t_len // GRID_W
    assert t_len % GRID_W == 0 and rows >= NA_ROWS and n % MOE_TILE == 0

    meta = jnp.broadcast_to(meta_tokens.astype(x.dtype)[None], (b, N_META, d))
    h = jnp.concatenate([meta, x, jnp.zeros((b, lp - seq_len, d), x.dtype)], axis=1).reshape(n, d)

    cmat, smat = _twiddles(seq_len, lp)
    tk = _fft_row_tile(lp)
    tri = jnp.tril(jnp.ones((MOE_TILE, MOE_TILE), BF16), -1)

    for i in range(depth):
        z = _norm_matmul(h, prm["norm_mix"][i], prm["w_in"][i])
        z3 = z.reshape(b, lp, D_IN)
        ab = _matmul_cols(z, prm["chan_dft"], (3 * D_ATT) // D_FFT)
        a = _attention(z3, prm["attn_bias"][i], prm["attn_mbias"][i], rows=rows, seq_len=seq_len)
        f = _fourier(cmat, smat, ab.reshape(b, lp, 2 * D_FFT), prm["w_fft"][i], tk=tk)
        p = _pool(z3, prm["w_pool"][i], prm["pool_scale"][i], seq_len=seq_len)
        h = _outproj(a.reshape(n, D_ATT), f.reshape(n, D_FFT), p.reshape(n, D_POOL), h,
                     prm["norm_groups"][i], prm["w_out"][i])
        j = i // 2
        if i % 2 == 0:
            h = _ffn(h, prm["norm_ffn"][i], prm["w_ff_gate"][j], prm["w_ff_up"][j], prm["w_ff_down"][j])
        else:
            xn, rec, counts = _router(h, prm["norm_ffn"][i], prm["w_router"][j], tri,
                                      seq_len=seq_len, lp=lp)
            n_row_tiles = -(-2 * b * seq_len // EXPERT_ROWS) + N_EXPERTS
            pos, tile_expert, tile_live = _dispatch_plan(rec, counts, n_row_tiles=n_row_tiles)
            xs = _dispatch(pos, xn, n_row_tiles * EXPERT_ROWS)
            ys = _experts(tile_expert, tile_live, xs,
                          prm["w_exp_gate"][j], prm["w_exp_up"][j], prm["w_exp_down"][j])
            h = _combine(pos, h, rec, ys)
    y = _final_norm(h, prm["norm_final"])
    return y.reshape(b, lp, d)[:, N_META:seq_len]


def kernel(x_prompt, x_sample, meta_tokens, norm_mix, w_in, w_fft, w_pool, pool_scale, rel_bias, meta_bias,
           norm_groups, w_out, norm_ffn, w_ff_gate, w_ff_up, w_ff_down, w_router, w_exp_gate, w_exp_up,
           w_exp_down, norm_final):
    depth = w_in.shape[0]
    tabs = [_attn_bias_tables(rel_bias[i], meta_bias[i]) for i in range(depth)]
    prm = {
        "norm_mix": norm_mix, "norm_groups": norm_groups, "norm_ffn": norm_ffn, "norm_final": norm_final,
        "pool_scale": pool_scale,
        "w_in": w_in.astype(BF16), "w_out": w_out.astype(BF16),
        "w_ff_gate": w_ff_gate.astype(BF16), "w_ff_up": w_ff_up.astype(BF16), "w_ff_down": w_ff_down.astype(BF16),
        "w_exp_gate": w_exp_gate.astype(BF16), "w_exp_up": w_exp_up.astype(BF16),
        "w_exp_down": w_exp_down.astype(BF16),
        "w_router": jnp.pad(w_router, ((0, 0), (0, 0), (0, LANE - N_EXPERTS))),
        "w_fft": jnp.stack([_block_diag(w_fft[i]) for i in range(depth)]).astype(BF16),
        "w_pool": jnp.stack([_block_diag(w_pool[i]) for i in range(depth)]).astype(BF16),
        "chan_dft": _channel_dft_matrix(),
        "attn_bias": [tb[0] for tb in tabs], "attn_mbias": [tb[1] for tb in tabs],
    }
    y_prompt = _trunk(x_prompt, meta_tokens, prm)
    y_sample = _trunk(x_sample, meta_tokens, prm)
    return (y_prompt, y_sample)
```

```python
import functools
import math

import numpy as np
import jax
import jax.numpy as jnp
from jax import lax
from jax.experimental import pallas as pl
from jax.experimental.pallas import tpu as pltpu

D_MODEL = 1024
N_META = 16
GRID_W = 64
D_HEAD = 64
D_ATT = 512
N_ATT_HEADS = 8
D_FFT = 256
N_FFT_HEADS = 4
D_FFT_HEAD = 64
D_POOL = 256
POOL_WINDOWS = (2, 4, 8, 16)
D_POOL_GROUP = 64
D_IN = 2048
NA_ROWS = 8
NA_COLS = 16
N_EXPERTS = 8
EPS = 1e-6

LANE = 128
NEG_BIG = -1e30
VMEM_LIMIT = 56 * 1024 * 1024

F32 = jnp.float32
BF16 = jnp.bfloat16


def _cparams(n_axes, vmem=VMEM_LIMIT):
    return pltpu.CompilerParams(dimension_semantics=("arbitrary",) * n_axes,
                                vmem_limit_bytes=vmem)


def _rms(x, g):
    return x * lax.rsqrt(jnp.mean(x * x, axis=-1, keepdims=True) + EPS) * g


def _norm_matmul_kernel(x_ref, g_ref, w_ref, o_ref, xn_ref, *, tn):
    xn_ref[...] = _rms(x_ref[...], g_ref[...]).astype(BF16)
    for n0 in range(0, o_ref.shape[1], tn):
        o_ref[:, n0:n0 + tn] = jnp.dot(xn_ref[...], w_ref[:, n0:n0 + tn],
                                       preferred_element_type=F32).astype(o_ref.dtype)


def _norm_matmul(x, g, w, *, tm=1024, tn=512):
    n, d = x.shape
    n_out = w.shape[1]
    return pl.pallas_call(
        functools.partial(_norm_matmul_kernel, tn=tn),
        out_shape=jax.ShapeDtypeStruct((n, n_out), BF16),
        grid=(n // tm,),
        in_specs=[pl.BlockSpec((tm, d), lambda i: (i, 0)),
                  pl.BlockSpec((1, d), lambda i: (0, 0)),
                  pl.BlockSpec((d, n_out), lambda i: (0, 0))],
        out_specs=pl.BlockSpec((tm, n_out), lambda i: (i, 0)),
        scratch_shapes=[pltpu.VMEM((tm, d), BF16)],
        compiler_params=_cparams(1),
        name="norm_matmul",
    )(x, g.reshape(1, d), w)


def _matmul_kernel(x_ref, w_ref, o_ref):
    o_ref[...] = jnp.dot(x_ref[...], w_ref[...], preferred_element_type=F32).astype(o_ref.dtype)


def _matmul_cols(x, w, col_block, *, tm=1024):
    n = x.shape[0]
    k, n_out = w.shape
    return pl.pallas_call(
        _matmul_kernel,
        out_shape=jax.ShapeDtypeStruct((n, n_out), BF16),
        grid=(n // tm,),
        in_specs=[pl.BlockSpec((tm, k), lambda i: (i, col_block)),
                  pl.BlockSpec((k, n_out), lambda i: (0, 0))],
        out_specs=pl.BlockSpec((tm, n_out), lambda i: (i, 0)),
        compiler_params=_cparams(1),
        name="channel_dft",
    )(x, w)


def _attn_kernel(q_ref, k_ref, v_ref, bias_ref, mb_ref, o_ref, smq_ref, s_a, s_b, s_c, s_d,
                 p_a, p_b, p_c, p_d, l_a, l_b, l_c, l_d, *, rows, seq_len):
    lp = o_ref.shape[1]
    n_win = NA_ROWS * GRID_W
    lane = lax.broadcasted_iota(jnp.int32, (1, LANE), 1)
    head0 = lane < D_HEAD
    nt = (((1,), (1,)), ((), ()))

    km16 = k_ref[0, 0:N_META, :]
    zero16 = jnp.zeros_like(km16)
    km = jnp.concatenate([jnp.where(head0, km16, zero16), jnp.where(head0, zero16, km16),
                          k_ref[0, 2 * N_META:LANE, :]], axis=0)
    vm = jnp.concatenate([v_ref[0, 0:N_META, :], v_ref[0, 0:N_META, :], v_ref[0, 2 * N_META:LANE, :]], axis=0)
    mb = mb_ref[0]

    def stack(q):
        q = q * jnp.asarray(D_HEAD ** -0.5, q.dtype)
        zero = jnp.zeros_like(q)
        return jnp.concatenate([jnp.where(head0, q, zero), jnp.where(head0, zero, q)], axis=0)

    def unstack(o, n):
        return jnp.where(head0, o[0:n], o[n:2 * n])

    q_all = q_ref[0] * jnp.asarray(D_HEAD ** -0.5, q_ref.dtype)
    smq_ref[...] = lax.dot_general(q_all, km, nt, preferred_element_type=F32)

    def meta_scores(q0, n, bias):
        blk = smq_ref[pl.ds(q0, n), :]
        return jnp.concatenate([blk, blk], axis=0) + bias

    sm = meta_scores(0, N_META, jnp.concatenate([mb[0:N_META], mb[GRID_W:GRID_W + N_META]], axis=0))
    pm = jnp.exp(sm - jnp.max(sm, axis=-1, keepdims=True))
    om = jnp.dot(pm.astype(BF16), vm, preferred_element_type=F32)
    om = om / jnp.sum(pm, axis=-1, keepdims=True)
    o_ref[0, 0:N_META, :] = unstack(om, N_META).astype(o_ref.dtype)

    def window(t):
        t = jnp.minimum(t, rows - 1)
        rs = jnp.clip(t - NA_ROWS // 2, 0, rows - NA_ROWS)
        q0 = pl.multiple_of(N_META + t * GRID_W, 16)
        k0 = pl.multiple_of(N_META + rs * GRID_W, 16)
        return q0, k0, t - rs

    def scores(t, s_ref):
        q0, k0, off = window(t)
        qs = stack(q_ref[0, pl.ds(q0, GRID_W), :])
        kw = k_ref[0, pl.ds(k0, n_win), :]
        s_ref[...] = lax.dot_general(qs, kw, nt, preferred_element_type=F32) + bias_ref[0, off]

    dyn_zero = pl.multiple_of(jnp.minimum(pl.program_id(0), 0), LANE)

    def lane_tiles(x):
        return [x[:, i:i + LANE] for i in range(0, x.shape[1], LANE)]

    def softmax(t, s_ref, p_ref, l_ref):
        q0, _, _ = window(t)
        s = s_ref[pl.ds(dyn_zero, LANE), :]
        sm = meta_scores(q0, GRID_W, mb)
        m = jnp.max(functools.reduce(jnp.maximum, lane_tiles(s) + [sm]), axis=-1, keepdims=True)
        p = jnp.exp(s - m)
        pm = jnp.exp(sm - m)
        l_ref[...] = jnp.sum(functools.reduce(jnp.add, lane_tiles(p) + [pm]), axis=-1, keepdims=True)
        p_ref[:, 0:n_win] = p.astype(BF16)
        p_ref[:, n_win:] = pm.astype(BF16)

    def values(t, p_ref, l_ref):
        q0, k0, _ = window(t)
        vw = v_ref[0, pl.ds(k0, n_win), :]
        o = (jnp.dot(p_ref[:, 0:n_win], vw, preferred_element_type=F32)
             + jnp.dot(p_ref[:, n_win:], vm, preferred_element_type=F32))
        o = o / l_ref[...]
        o_ref[0, pl.ds(q0, GRID_W), :] = unstack(o, GRID_W).astype(o_ref.dtype)

    def step(t, s_in, s_out, pl_in, pl_out):
        scores(t + 4, s_out[0])
        scores(t + 5, s_out[1])
        values(t, *pl_in[0])
        values(t + 1, *pl_in[1])
        softmax(t + 2, s_in[0], *pl_out[0])
        softmax(t + 3, s_in[1], *pl_out[1])

    set_0 = ((p_a, l_a), (p_b, l_b))
    set_1 = ((p_c, l_c), (p_d, l_d))
    scores(0, s_c)
    scores(1, s_d)
    softmax(0, s_c, *set_0[0])
    softmax(1, s_d, *set_0[1])
    scores(2, s_a)
    scores(3, s_b)

    def quad_body(u, carry):
        t = 4 * u
        step(t, (s_a, s_b), (s_c, s_d), set_0, set_1)
        step(t + 2, (s_c, s_d), (s_a, s_b), set_1, set_0)
        return carry

    lax.fori_loop(0, rows // 4, quad_body, 0)
    if lp > seq_len:
        o_ref[0, seq_len:lp, :] = jnp.zeros((lp - seq_len, LANE), o_ref.dtype)


def _attention(z3, bias_tab, mb_tab, *, rows, seq_len):
    b, lp, _ = z3.shape
    assert rows % 4 == 0
    n_pairs = N_ATT_HEADS // 2
    n_keys = NA_ROWS * GRID_W
    blk = lambda off: pl.BlockSpec((1, lp, LANE), lambda hp, bi: (bi, 0, off + hp))
    scratch = ([pltpu.VMEM((lp, LANE), F32)]
               + [pltpu.VMEM((LANE, n_keys), F32) for _ in range(4)]
               + [pltpu.VMEM((LANE, n_keys + LANE), BF16) for _ in range(4)]
               + [pltpu.VMEM((LANE, 1), F32) for _ in range(4)])
    return pl.pallas_call(
        functools.partial(_attn_kernel, rows=rows, seq_len=seq_len),
        out_shape=jax.ShapeDtypeStruct((b, lp, D_ATT), BF16),
        grid=(n_pairs, b),
        in_specs=[blk(0), blk(n_pairs), blk(2 * n_pairs),
                  pl.BlockSpec((1, NA_ROWS, LANE, n_keys), lambda hp, bi: (hp, 0, 0, 0)),
                  pl.BlockSpec((1, LANE, LANE), lambda hp, bi: (hp, 0, 0))],
        out_specs=pl.BlockSpec((1, lp, LANE), lambda hp, bi: (bi, 0, hp)),
        scratch_shapes=scratch,
        compiler_params=_cparams(2),
        name="nbr_attention",
    )(z3, z3, z3, bias_tab, mb_tab)


def _attn_bias_tables(rel_bias, meta_bias):
    h, n_dr, n_dc = rel_bias.shape
    c = np.arange(GRID_W)[:, None]
    j = np.arange(GRID_W)[None, :]
    cs = np.clip(c - NA_COLS // 2, 0, GRID_W - NA_COLS)
    valid = (j >= cs) & (j < cs + NA_COLS)
    ext = jnp.zeros((h, n_dr, 2 * GRID_W), F32)
    ext = lax.dynamic_update_slice(ext, rel_bias.astype(F32), (0, 0, GRID_W - NA_COLS))
    skew = jnp.tile(ext, (1, 1, GRID_W))[:, :, :GRID_W * (2 * GRID_W - 1)]
    toep = skew.reshape(h, n_dr, GRID_W, 2 * GRID_W - 1)[..., GRID_W - 1:]
    toep = jnp.where(valid[None, None], toep, NEG_BIG)
    t = jnp.stack([toep[:, NA_ROWS - 1 - oi:2 * NA_ROWS - 1 - oi] for oi in range(NA_ROWS)], axis=1)
    t = t.transpose(0, 1, 3, 2, 4)
    t = t.reshape(h // 2, 2, NA_ROWS, GRID_W, NA_ROWS * GRID_W)
    t = t.transpose(0, 2, 1, 3, 4).reshape(h // 2, NA_ROWS, 2 * GRID_W, NA_ROWS * GRID_W)
    mbp = meta_bias.astype(F32).reshape(h // 2, 2, N_META)
    neg = jnp.full((h // 2, N_META), NEG_BIG, F32)
    mb0 = jnp.concatenate([mbp[:, 0], neg], axis=-1)
    mb1 = jnp.concatenate([neg, mbp[:, 1]], axis=-1)
    mb = jnp.stack([mb0, mb1], axis=1)
    mb = jnp.pad(mb, ((0, 0), (0, 0), (0, LANE - 2 * N_META)), constant_values=NEG_BIG)
    mb = jnp.broadcast_to(mb[:, :, None, :], (h // 2, 2, GRID_W, LANE))
    return t, mb.reshape(h // 2, 2 * GRID_W, LANE)


def _fft_kernel(c_ref, s_ref, ab_ref, w_ref, o_ref):
    a = ab_ref[0, :, 0:D_FFT]
    b = ab_ref[0, :, D_FFT:2 * D_FFT]
    f = (jnp.dot(c_ref[...], a, preferred_element_type=F32)
         + jnp.dot(s_ref[...], b, preferred_element_type=F32))
    o_ref[0] = jnp.dot(f.astype(BF16), w_ref[...], preferred_element_type=F32).astype(o_ref.dtype)


def _fourier(cmat, smat, ab3, w_bd, *, tk):
    b, lp, _ = ab3.shape
    return pl.pallas_call(
        _fft_kernel,
        out_shape=jax.ShapeDtypeStruct((b, lp, D_FFT), BF16),
        grid=(lp // tk, b),
        in_specs=[pl.BlockSpec((tk, lp), lambda j, bi: (j, 0)),
                  pl.BlockSpec((tk, lp), lambda j, bi: (j, 0)),
                  pl.BlockSpec((1, lp, 2 * D_FFT), lambda j, bi: (bi, 0, 0)),
                  pl.BlockSpec((D_FFT, D_FFT), lambda j, bi: (0, 0))],
        out_specs=pl.BlockSpec((1, tk, D_FFT), lambda j, bi: (bi, j, 0)),
        compiler_params=_cparams(2),
        name="fourier_mix",
    )(cmat, smat, ab3, w_bd)


def _twiddles(seq_len, lp):
    theta = 2.0 * math.pi / seq_len
    k = jnp.arange(lp, dtype=jnp.int32)[:, None]
    t1 = (GRID_W * jnp.arange(lp // GRID_W, dtype=jnp.int32))[None, :]
    t0 = jnp.arange(GRID_W, dtype=jnp.int32)[None, :]
    ang_a = ((k * t1) % seq_len).astype(F32) * theta
    ang_b = ((k * t0) % seq_len).astype(F32) * theta
    ca, sa = jnp.cos(ang_a)[:, :, None], jnp.sin(ang_a)[:, :, None]
    cb, sb = jnp.cos(ang_b)[:, None, :], jnp.sin(ang_b)[:, None, :]
    cmat = (ca * cb - sa * sb).reshape(lp, lp)
    smat = (sa * cb + ca * sb).reshape(lp, lp)
    idx = jnp.arange(lp)
    valid = (idx[:, None] < seq_len) & (idx[None, :] < seq_len)
    scale = seq_len ** -0.5
    cmat = jnp.where(valid, cmat * scale, 0.0).astype(BF16)
    smat = jnp.where(valid, -smat * scale, 0.0).astype(BF16)
    return cmat, smat


def _channel_dft_matrix():
    c = np.arange(D_FFT_HEAD)
    ang = 2.0 * np.pi * ((c[:, None] * c[None, :]) % D_FFT_HEAD) / D_FFT_HEAD
    cc = np.cos(ang) / math.sqrt(D_FFT_HEAD)
    sc = np.sin(ang) / math.sqrt(D_FFT_HEAD)
    out = np.zeros((D_FFT, 2 * D_FFT), np.float32)
    for g in range(N_FFT_HEADS):
        sl = slice(g * D_FFT_HEAD, (g + 1) * D_FFT_HEAD)
        out[sl, sl] = cc
        out[sl, D_FFT + g * D_FFT_HEAD:D_FFT + (g + 1) * D_FFT_HEAD] = sc
    return jnp.asarray(out, BF16)


def _block_diag(w):
    g, c, e = w.shape
    out = jnp.zeros((g * c, g * e), w.dtype)
    for i in range(g):
        out = lax.dynamic_update_slice(out, w[i], (i * c, i * e))
    return out


POOL_PAD = 16


def _pool_kernel(u_ref, w_ref, sc_ref, o_ref, s0, s1, s2, s3, s4, *, seq_len):
    lp = o_ref.shape[1]
    r_tot = lp + 2 * POOL_PAD
    lo, hi = 8, r_tot - 8
    row = lax.broadcasted_iota(jnp.int32, (lp, 1), 0)
    live = row < seq_len
    x = jnp.where(live, u_ref[0].astype(F32), 0.0)

    zeros_pad = jnp.zeros((POOL_PAD, D_POOL), F32)
    s0[0:POOL_PAD, :] = zeros_pad
    s0[POOL_PAD:POOL_PAD + lp, :] = x
    s0[POOL_PAD + lp:r_tot, :] = zeros_pad
    for s in (s1, s2, s3, s4):
        s[0:POOL_PAD, :] = zeros_pad
        s[POOL_PAD + lp:r_tot, :] = zeros_pad
    s1[lo:hi, :] = s0[lo - 1:hi - 1, :] + s0[lo:hi, :]
    s2[lo:hi, :] = s1[lo - 1:hi - 1, :] + s1[lo + 1:hi + 1, :]
    s3[lo:hi, :] = s2[lo - 2:hi - 2, :] + s2[lo + 2:hi + 2, :]
    s4[lo:hi, :] = s3[lo - 4:hi - 4, :] + s3[lo + 4:hi + 4, :]

    lane = lax.broadcasted_iota(jnp.int32, (1, D_POOL), 1)
    group = jnp.right_shift(lane, 6)
    half = jnp.where(group == 0, 1, jnp.where(group == 1, 2, jnp.where(group == 2, 4, 8)))
    cnt = jnp.minimum(row + half, seq_len) - jnp.maximum(row - half, 0)
    cnt = jnp.maximum(cnt, 1).astype(F32)
    sl = slice(POOL_PAD, POOL_PAD + lp)
    wsum = jnp.where(group == 0, s1[sl, :],
                     jnp.where(group == 1, s2[sl, :], jnp.where(group == 2, s3[sl, :], s4[sl, :])))
    p = jnp.where(live, wsum / cnt - x, 0.0)
    y = jnp.dot(p.astype(BF16), w_ref[...], preferred_element_type=F32) * sc_ref[...]
    o_ref[0] = y.astype(o_ref.dtype)


def _pool(z3, w_bd, scale, *, seq_len):
    b, lp, d_in = z3.shape
    col_block = (d_in - D_POOL) // D_POOL
    scratch = [pltpu.VMEM((lp + 2 * POOL_PAD, D_POOL), F32) for _ in range(5)]
    return pl.pallas_call(
        functools.partial(_pool_kernel, seq_len=seq_len),
        out_shape=jax.ShapeDtypeStruct((b, lp, D_POOL), BF16),
        grid=(b,),
        in_specs=[pl.BlockSpec((1, lp, D_POOL), lambda bi: (bi, 0, col_block)),
                  pl.BlockSpec((D_POOL, D_POOL), lambda bi: (0, 0)),
                  pl.BlockSpec((1, D_POOL), lambda bi: (0, 0))],
        out_specs=pl.BlockSpec((1, lp, D_POOL), lambda bi: (bi, 0, 0)),
        scratch_shapes=scratch,
        compiler_params=_cparams(1),
        name="pool_mix",
    )(z3, w_bd, scale.reshape(1, D_POOL))


def _outproj_kernel(a_ref, f_ref, p_ref, h_ref, g_ref, w_ref, o_ref):
    an = _rms(a_ref[...].astype(F32), g_ref[:, 0:D_ATT]).astype(BF16)
    fn = _rms(f_ref[...].astype(F32), g_ref[:, D_ATT:D_ATT + D_FFT]).astype(BF16)
    pn = _rms(p_ref[...].astype(F32), g_ref[:, D_ATT + D_FFT:]).astype(BF16)
    acc = jnp.dot(an, w_ref[0:D_ATT, :], preferred_element_type=F32)
    acc += jnp.dot(fn, w_ref[D_ATT:D_ATT + D_FFT, :], preferred_element_type=F32)
    acc += jnp.dot(pn, w_ref[D_ATT + D_FFT:, :], preferred_element_type=F32)
    o_ref[...] = h_ref[...] + acc


def _outproj(a, f, p, h, g, w, *, tm=1024):
    n, d = h.shape
    row = lambda width: pl.BlockSpec((tm, width), lambda i: (i, 0))
    return pl.pallas_call(
        _outproj_kernel,
        out_shape=jax.ShapeDtypeStruct((n, d), F32),
        grid=(n // tm,),
        in_specs=[row(D_ATT), row(D_FFT), row(D_POOL), row(d),
                  pl.BlockSpec((1, d), lambda i: (0, 0)),
                  pl.BlockSpec((d, d), lambda i: (0, 0))],
        out_specs=row(d),
        compiler_params=_cparams(1),
        name="out_proj",
    )(a, f, p, h, g.reshape(1, d), w)


def _ffn_kernel(h_ref, g_ref, wg_ref, wu_ref, wd_ref, o_ref, xn_ref, acc_ref):
    c = pl.program_id(1)

    @pl.when(c == 0)
    def _():
        xn_ref[...] = _rms(h_ref[...], g_ref[...]).astype(BF16)

    xn = xn_ref[...]
    gate = jnp.dot(xn, wg_ref[...], preferred_element_type=F32)
    up = jnp.dot(xn, wu_ref[...], preferred_element_type=F32)
    hh = (gate * jax.nn.sigmoid(gate) * up).astype(BF16)
    part = jnp.dot(hh, wd_ref[...], preferred_element_type=F32)

    @pl.when(c == 0)
    def _():
        acc_ref[...] = part

    @pl.when(c > 0)
    def _():
        acc_ref[...] += part

    @pl.when(c == pl.num_programs(1) - 1)
    def _():
        o_ref[...] = h_ref[...] + acc_ref[...]


def _ffn(h, g, wg, wu, wd, *, tm=512, n_chunks=2):
    n, d = h.shape
    d_ff = wg.shape[1]
    fc = d_ff // n_chunks
    return pl.pallas_call(
        _ffn_kernel,
        out_shape=jax.ShapeDtypeStruct((n, d), F32),
        grid=(n // tm, n_chunks),
        in_specs=[pl.BlockSpec((tm, d), lambda i, c: (i, 0)),
                  pl.BlockSpec((1, d), lambda i, c: (0, 0)),
                  pl.BlockSpec((d, fc), lambda i, c: (0, c)),
                  pl.BlockSpec((d, fc), lambda i, c: (0, c)),
                  pl.BlockSpec((fc, d), lambda i, c: (c, 0))],
        out_specs=pl.BlockSpec((tm, d), lambda i, c: (i, 0)),
        scratch_shapes=[pltpu.VMEM((tm, d), BF16), pltpu.VMEM((tm, d), F32)],
        compiler_params=_cparams(2),
        name="swiglu_ffn",
    )(h, g.reshape(1, d), wg, wu, wd)


MOE_TILE = 1024
EXPERT_ROWS = 512
R_GATE, R_EXPERT, R_RANK = 0, 2, 4


def _router_kernel(h_ref, g_ref, wr_ref, tri_ref, xn_ref, rec_ref, cnt_ref, *, seq_len, lp):
    t = h_ref.shape[0]
    y = _rms(h_ref[...], g_ref[...])
    xn_ref[...] = y
    logits = jnp.dot(y, wr_ref[...], preferred_element_type=F32, precision=lax.Precision.HIGHEST)
    lane = lax.broadcasted_iota(jnp.int32, (t, LANE), 1)
    lg = jnp.where(lane < N_EXPERTS, logits, -jnp.inf)
    m1 = jnp.max(lg, axis=-1, keepdims=True)
    i1 = jnp.min(jnp.where(lg == m1, lane, LANE), axis=-1, keepdims=True)
    lg2 = jnp.where(lane == i1, -jnp.inf, lg)
    m2 = jnp.max(lg2, axis=-1, keepdims=True)
    i2 = jnp.min(jnp.where(lg2 == m2, lane, LANE), axis=-1, keepdims=True)
    e2 = jnp.exp(m2 - m1)
    g1 = 1.0 / (1.0 + e2)
    g2 = e2 / (1.0 + e2)
    rowf = (pl.program_id(0) * t + lax.broadcasted_iota(jnp.int32, (t, 1), 0)).astype(F32)
    seq = jnp.floor((rowf + 0.5) * (1.0 / lp))
    live = (rowf - seq * lp) < seq_len
    first = lane == i1
    second = lane == i2
    member = jnp.where((first | second) & live, 1.0, 0.0)
    rank = jnp.dot(tri_ref[...], member.astype(BF16), preferred_element_type=F32)
    r1 = jnp.sum(jnp.where(first, rank, 0.0), axis=-1, keepdims=True)
    r2 = jnp.sum(jnp.where(second, rank, 0.0), axis=-1, keepdims=True)
    dead = jnp.logical_not(live)
    rec = jnp.zeros((t, LANE), F32)
    for ln, val in ((R_GATE, jnp.where(dead, 0.0, g1)), (R_GATE + 1, jnp.where(dead, 0.0, g2)),
                    (R_EXPERT, jnp.where(dead, -1.0, i1.astype(F32))),
                    (R_EXPERT + 1, jnp.where(dead, -1.0, i2.astype(F32))),
                    (R_RANK, r1), (R_RANK + 1, r2)):
        rec = jnp.where(lane == ln, val, rec)
    rec_ref[...] = rec
    cnt_ref[0] = jnp.broadcast_to(jnp.sum(member, axis=0, keepdims=True), (8, LANE))


def _router(h, g, w_router_pad, tri, *, seq_len, lp):
    n, d = h.shape
    t = MOE_TILE
    row = lambda width: pl.BlockSpec((t, width), lambda i: (i, 0))
    return pl.pallas_call(
        functools.partial(_router_kernel, seq_len=seq_len, lp=lp),
        out_shape=(jax.ShapeDtypeStruct((n, d), F32),
                   jax.ShapeDtypeStruct((n, LANE), F32),
                   jax.ShapeDtypeStruct((n // t, 8, LANE), F32)),
        grid=(n // t,),
        in_specs=[row(d),
                  pl.BlockSpec((1, d), lambda i: (0, 0)),
                  pl.BlockSpec((d, LANE), lambda i: (0, 0)),
                  pl.BlockSpec((t, t), lambda i: (0, 0))],
        out_specs=(row(d), row(LANE), pl.BlockSpec((1, 8, LANE), lambda i: (i, 0, 0))),
        compiler_params=_cparams(1),
        name="moe_router",
    )(h, g.reshape(1, d), w_router_pad, tri)


def _dispatch_plan(rec, counts, pad_token, *, n_row_tiles):
    n = rec.shape[0]
    n_tiles = counts.shape[0]
    cnt = counts[:, 0, :N_EXPERTS]
    before = jnp.cumsum(cnt, axis=0) - cnt
    total = jnp.sum(cnt, axis=0)
    padded = jnp.ceil(total / EXPERT_ROWS) * EXPERT_ROWS
    ends = jnp.cumsum(padded)
    start = ends - padded
    base = (start[None, :] + before)[:, None, :]
    pad_rank = np.cumsum(np.asarray(pad_token, np.int64)) - 1
    spill = n_row_tiles * EXPERT_ROWS + 2 * pad_rank.reshape(n_tiles, MOE_TILE)
    slots_out, slots_back = [], []
    for k in range(2):
        e = rec[:, R_EXPERT + k].reshape(n_tiles, MOE_TILE, 1)
        r = rec[:, R_RANK + k].reshape(n_tiles, MOE_TILE)
        hit = e == jnp.arange(N_EXPERTS, dtype=F32)[None, None, :]
        slot = (jnp.sum(jnp.where(hit, base, 0.0), axis=-1) + r).astype(jnp.int32)
        dead = e[..., 0] < 0
        slots_out.append(jnp.where(dead, jnp.asarray(spill + k, jnp.int32), slot))
        slots_back.append(jnp.where(dead, 0, slot))
    pos_out = jnp.stack(slots_out, axis=-1).reshape(n * 2)
    pos_back = jnp.stack(slots_back, axis=-1).reshape(n * 2)
    tile_row0 = jnp.arange(n_row_tiles, dtype=F32) * EXPERT_ROWS
    tile_expert = jnp.sum(tile_row0[:, None] >= ends[None, :], axis=-1)
    tile_live = (tile_row0 < ends[-1]).astype(jnp.int32)
    tile_expert = jnp.minimum(tile_expert, N_EXPERTS - 1).astype(jnp.int32)
    return pos_out, pos_back, tile_expert, tile_live


ROW_GROUP = 8


def _row_copy(src_ref, src_row, dst_ref, dst_row, sem):
    return pltpu.make_async_copy(src_ref.at[pl.ds(src_row, 1)], dst_ref.at[pl.ds(dst_row, 1)], sem)


def _dispatch_kernel(pos_ref, x_ref, zero_ref, xs_ref, sem):
    del zero_ref
    t = x_ref.shape[0]
    base = pl.program_id(0) * t * 2

    def issue(g, carry):
        row0 = pl.multiple_of(g * ROW_GROUP, ROW_GROUP)
        for c in range(ROW_GROUP):
            for k in range(2):
                p = pos_ref[base + 2 * (row0 + c) + k]
                _row_copy(x_ref, row0 + c, xs_ref, p, sem).start(priority=k)
        return carry

    def drain(g, carry):
        for _ in range(2 * ROW_GROUP):
            _row_copy(x_ref, 0, xs_ref, 0, sem).wait()
        return carry

    lax.fori_loop(0, t // ROW_GROUP, issue, 0)
    lax.fori_loop(0, t // ROW_GROUP, drain, 0)


def _dispatch(pos, xn, n_rows):
    n, d = xn.shape
    t = MOE_TILE
    grid_spec = pltpu.PrefetchScalarGridSpec(
        num_scalar_prefetch=1,
        grid=(n // t,),
        in_specs=[pl.BlockSpec((t, d), lambda i, pos: (i, 0)),
                  pl.BlockSpec(memory_space=pl.ANY)],
        out_specs=pl.BlockSpec(memory_space=pl.ANY),
        scratch_shapes=[pltpu.SemaphoreType.DMA],
    )
    return pl.pallas_call(
        _dispatch_kernel,
        out_shape=jax.ShapeDtypeStruct((n_rows, d), F32),
        grid_spec=grid_spec,
        input_output_aliases={2: 0},
        compiler_params=_cparams(1),
        name="moe_dispatch",
    )(pos, xn, jnp.zeros((n_rows, d), F32))


def _expert_kernel(te_ref, live_ref, x_ref, wg_ref, wu_ref, wd_ref, o_ref, xb_ref, acc_ref):
    del te_ref
    i, c = pl.program_id(0), pl.program_id(1)
    last = pl.num_programs(1) - 1
    live = live_ref[i] > 0

    @pl.when(live)
    def _():
        @pl.when(c == 0)
        def _():
            xb_ref[...] = x_ref[...].astype(BF16)

        xb = xb_ref[...]
        gate = jnp.dot(xb, wg_ref[0], preferred_element_type=F32)
        up = jnp.dot(xb, wu_ref[0], preferred_element_type=F32)
        hh = (gate * jax.nn.sigmoid(gate) * up).astype(BF16)
        part = jnp.dot(hh, wd_ref[0], preferred_element_type=F32)

        @pl.when(c == 0)
        def _():
            acc_ref[...] = part

        @pl.when(c > 0)
        def _():
            acc_ref[...] += part

        @pl.when(c == last)
        def _():
            o_ref[...] = acc_ref[...]

    @pl.when(jnp.logical_not(live) & (c == last))
    def _():
        o_ref[...] = jnp.zeros(o_ref.shape, o_ref.dtype)


def _experts(tile_expert, tile_live, xs, n_row_tiles, wg, wu, wd, *, n_chunks=2):
    d = xs.shape[1]
    tm = EXPERT_ROWS
    n_rows = n_row_tiles * tm
    fc = wg.shape[2] // n_chunks
    chunk = lambda i, c, te, lv: jnp.where(lv[i] > 0, c, n_chunks - 1)
    grid_spec = pltpu.PrefetchScalarGridSpec(
        num_scalar_prefetch=2,
        grid=(n_rows // tm, n_chunks),
        in_specs=[pl.BlockSpec((tm, d), lambda i, c, te, lv: (i, 0)),
                  pl.BlockSpec((1, d, fc), lambda i, c, te, lv: (te[i], 0, chunk(i, c, te, lv))),
                  pl.BlockSpec((1, d, fc), lambda i, c, te, lv: (te[i], 0, chunk(i, c, te, lv))),
                  pl.BlockSpec((1, fc, d), lambda i, c, te, lv: (te[i], chunk(i, c, te, lv), 0))],
        out_specs=pl.BlockSpec((tm, d), lambda i, c, te, lv: (i, 0)),
        scratch_shapes=[pltpu.VMEM((tm, d), BF16), pltpu.VMEM((tm, d), F32)],
    )
    return pl.pallas_call(
        _expert_kernel,
        out_shape=jax.ShapeDtypeStruct((n_rows, d), F32),
        grid_spec=grid_spec,
        compiler_params=_cparams(2),
        name="moe_experts",
    )(tile_expert, tile_live, xs, wg, wu, wd)


def _combine_kernel(pos_ref, h_ref, rec_ref, ys_ref, o_ref, buf0, buf1, sem):
    t = h_ref.shape[0]
    base = pl.program_id(0) * t * 2
    bufs = (buf0, buf1)

    def issue(g, carry):
        row0 = pl.multiple_of(g * ROW_GROUP, ROW_GROUP)
        for c in range(ROW_GROUP):
            for k in range(2):
                p = pos_ref[base + 2 * (row0 + c) + k]
                _row_copy(ys_ref, p, bufs[k], row0 + c, sem).start(priority=k)
        return carry

    def drain(g, carry):
        for _ in range(2 * ROW_GROUP):
            _row_copy(ys_ref, 0, buf0, 0, sem).wait()
        return carry

    lax.fori_loop(0, t // ROW_GROUP, issue, 0)
    lax.fori_loop(0, t // ROW_GROUP, drain, 0)
    rec = rec_ref[...]
    g1 = rec[:, R_GATE:R_GATE + 1]
    g2 = rec[:, R_GATE + 1:R_GATE + 2]
    o_ref[...] = h_ref[...] + g1 * buf0[...] + g2 * buf1[...]


def _combine(pos, h, rec, ys):
    n, d = h.shape
    t = MOE_TILE
    grid_spec = pltpu.PrefetchScalarGridSpec(
        num_scalar_prefetch=1,
        grid=(n // t,),
        in_specs=[pl.BlockSpec((t, d), lambda i, pos: (i, 0)),
                  pl.BlockSpec((t, LANE), lambda i, pos: (i, 0)),
                  pl.BlockSpec(memory_space=pl.ANY)],
        out_specs=pl.BlockSpec((t, d), lambda i, pos: (i, 0)),
        scratch_shapes=[pltpu.VMEM((t, d), F32), pltpu.VMEM((t, d), F32), pltpu.SemaphoreType.DMA],
    )
    return pl.pallas_call(
        _combine_kernel,
        out_shape=jax.ShapeDtypeStruct((n, d), F32),
        grid_spec=grid_spec,
        compiler_params=_cparams(1),
        name="moe_combine",
    )(pos, h, rec, ys)


def _norm_kernel(x_ref, tail_ref, g_ref, o_ref):
    tm = x_ref.shape[1]
    o_ref[0, 0:tm - N_META, :] = _rms(x_ref[0, N_META:tm, :], g_ref[...])
    o_ref[0, tm - N_META:tm, :] = _rms(tail_ref[0], g_ref[...])


def _final_norm(h3, g, *, t_len, tm=1024):
    b, lp, d = h3.shape
    assert t_len % tm == 0 and tm % N_META == 0
    return pl.pallas_call(
        _norm_kernel,
        out_shape=jax.ShapeDtypeStruct((b, t_len, d), F32),
        grid=(b, t_len // tm),
        in_specs=[pl.BlockSpec((1, tm, d), lambda bi, j: (bi, j, 0)),
                  pl.BlockSpec((1, N_META, d), lambda bi, j: (bi, (j + 1) * (tm // N_META), 0)),
                  pl.BlockSpec((1, d), lambda bi, j: (0, 0))],
        out_specs=pl.BlockSpec((1, tm, d), lambda bi, j: (bi, j, 0)),
        compiler_params=_cparams(2),
        name="final_norm",
    )(h3, h3, g.reshape(1, d))


def _fft_row_tile(lp):
    best = 16
    for tk in range(16, lp + 1, 16):
        if lp % tk == 0 and tk * lp * 2 <= 4608 * 1024:
            best = tk
    return best


def _trunk(x, meta_tokens, prm):
    b, t_len, d = x.shape
    depth = prm["w_in"].shape[0]
    seq_len = N_META + t_len
    lp = -(-seq_len // LANE) * LANE
    n = b * lp
    rows = t_len // GRID_W
    assert t_len % GRID_W == 0 and rows >= NA_ROWS and n % MOE_TILE == 0

    meta = jnp.broadcast_to(meta_tokens.astype(x.dtype)[None], (b, N_META, d))
    h = jnp.concatenate([meta, x, jnp.zeros((b, lp - seq_len, d), x.dtype)], axis=1).reshape(n, d)

    cmat, smat = _twiddles(seq_len, lp)
    tk = _fft_row_tile(lp)
    tri = jnp.tril(jnp.ones((MOE_TILE, MOE_TILE), BF16), -1)

    for i in range(depth):
        z = _norm_matmul(h, prm["norm_mix"][i], prm["w_in"][i])
        z3 = z.reshape(b, lp, D_IN)
        ab = _matmul_cols(z, prm["chan_dft"], (3 * D_ATT) // D_FFT)
        a = _attention(z3, prm["attn_bias"][i], prm["attn_mbias"][i], rows=rows, seq_len=seq_len)
        f = _fourier(cmat, smat, ab.reshape(b, lp, 2 * D_FFT), prm["w_fft"][i], tk=tk)
        p = _pool(z3, prm["w_pool"][i], prm["pool_scale"][i], seq_len=seq_len)
        h = _outproj(a.reshape(n, D_ATT), f.reshape(n, D_FFT), p.reshape(n, D_POOL), h,
                     prm["norm_groups"][i], prm["w_out"][i])
        j = i // 2
        if i % 2 == 0:
            h = _ffn(h, prm["norm_ffn"][i], prm["w_ff_gate"][j], prm["w_ff_up"][j], prm["w_ff_down"][j])
        else:
            xn, rec, counts = _router(h, prm["norm_ffn"][i], prm["w_router"][j], tri,
                                      seq_len=seq_len, lp=lp)
            n_row_tiles = -(-2 * b * seq_len // EXPERT_ROWS) + N_EXPERTS
            pad_token = (np.arange(n) % lp) >= seq_len
            pos_out, pos_back, tile_expert, tile_live = _dispatch_plan(rec, counts, pad_token,
                                                                       n_row_tiles=n_row_tiles)
            n_spill = -(-2 * int(pad_token.sum()) // ROW_GROUP) * ROW_GROUP
            xs = _dispatch(pos_out, xn, n_row_tiles * EXPERT_ROWS + n_spill)
            ys = _experts(tile_expert, tile_live, xs, n_row_tiles,
                          prm["w_exp_gate"][j], prm["w_exp_up"][j], prm["w_exp_down"][j])
            h = _combine(pos_back, h, rec, ys)
    return _final_norm(h.reshape(b, lp, d), prm["norm_final"], t_len=t_len)


def kernel(x_prompt, x_sample, meta_tokens, norm_mix, w_in, w_fft, w_pool, pool_scale, rel_bias, meta_bias,
           norm_groups, w_out, norm_ffn, w_ff_gate, w_ff_up, w_ff_down, w_router, w_exp_gate, w_exp_up,
           w_exp_down, norm_final):
    depth = w_in.shape[0]
    tabs = [_attn_bias_tables(rel_bias[i], meta_bias[i]) for i in range(depth)]
    prm = {
        "norm_mix": norm_mix, "norm_groups": norm_groups, "norm_ffn": norm_ffn, "norm_final": norm_final,
        "pool_scale": pool_scale,
        "w_in": w_in.astype(BF16), "w_out": w_out.astype(BF16),
        "w_ff_gate": w_ff_gate.astype(BF16), "w_ff_up": w_ff_up.astype(BF16), "w_ff_down": w_ff_down.astype(BF16),
        "w_exp_gate": w_exp_gate.astype(BF16), "w_exp_up": w_exp_up.astype(BF16),
        "w_exp_down": w_exp_down.astype(BF16),
        "w_router": jnp.pad(w_router, ((0, 0), (0, 0), (0, LANE - N_EXPERTS))),
        "w_fft": jnp.stack([_block_diag(w_fft[i]) for i in range(depth)]).astype(BF16),
        "w_pool": jnp.stack([_block_diag(w_pool[i]) for i in range(depth)]).astype(BF16),
        "chan_dft": _channel_dft_matrix(),
        "attn_bias": [tb[0] for tb in tabs], "attn_mbias": [tb[1] for tb in tabs],
    }
    y_prompt = _trunk(x_prompt, meta_tokens, prm)
    y_sample = _trunk(x_sample, meta_tokens, prm)
    return (y_prompt, y_sample)
```

```python
import functools
import math

import numpy as np
import jax
import jax.numpy as jnp
from jax import lax
from jax.experimental import pallas as pl
from jax.experimental.pallas import tpu as pltpu

D_MODEL = 1024
N_META = 16
GRID_W = 64
D_HEAD = 64
D_ATT = 512
N_ATT_HEADS = 8
D_FFT = 256
N_FFT_HEADS = 4
D_FFT_HEAD = 64
D_POOL = 256
POOL_WINDOWS = (2, 4, 8, 16)
D_POOL_GROUP = 64
D_IN = 2048
NA_ROWS = 8
NA_COLS = 16
N_EXPERTS = 8
EPS = 1e-6

LANE = 128
NEG_BIG = -1e30
LOG2E = math.log2(math.e)
VMEM_LIMIT = 56 * 1024 * 1024

F32 = jnp.float32
BF16 = jnp.bfloat16


def _cparams(n_axes, vmem=VMEM_LIMIT):
    return pltpu.CompilerParams(dimension_semantics=("arbitrary",) * n_axes,
                                vmem_limit_bytes=vmem)


def _rms(x, g):
    return x * lax.rsqrt(jnp.mean(x * x, axis=-1, keepdims=True) + EPS) * g


def _norm_matmul_kernel(x_ref, g_ref, w_ref, o_ref, xn_ref, *, tn):
    xn_ref[...] = _rms(x_ref[...], g_ref[...]).astype(BF16)
    for n0 in range(0, o_ref.shape[1], tn):
        o_ref[:, n0:n0 + tn] = jnp.dot(xn_ref[...], w_ref[:, n0:n0 + tn],
                                       preferred_element_type=F32).astype(o_ref.dtype)


def _norm_matmul(x, g, w, *, tm=1024, tn=512):
    n, d = x.shape
    n_out = w.shape[1]
    return pl.pallas_call(
        functools.partial(_norm_matmul_kernel, tn=tn),
        out_shape=jax.ShapeDtypeStruct((n, n_out), BF16),
        grid=(n // tm,),
        in_specs=[pl.BlockSpec((tm, d), lambda i: (i, 0)),
                  pl.BlockSpec((1, d), lambda i: (0, 0)),
                  pl.BlockSpec((d, n_out), lambda i: (0, 0))],
        out_specs=pl.BlockSpec((tm, n_out), lambda i: (i, 0)),
        scratch_shapes=[pltpu.VMEM((tm, d), BF16)],
        compiler_params=_cparams(1),
        name="norm_matmul",
    )(x, g.reshape(1, d), w)


def _matmul_kernel(x_ref, w_ref, o_ref):
    o_ref[...] = jnp.dot(x_ref[...], w_ref[...], preferred_element_type=F32).astype(o_ref.dtype)


def _matmul_cols(x, w, col_block, *, tm=1024):
    n = x.shape[0]
    k, n_out = w.shape
    return pl.pallas_call(
        _matmul_kernel,
        out_shape=jax.ShapeDtypeStruct((n, n_out), BF16),
        grid=(n // tm,),
        in_specs=[pl.BlockSpec((tm, k), lambda i: (i, col_block)),
                  pl.BlockSpec((k, n_out), lambda i: (0, 0))],
        out_specs=pl.BlockSpec((tm, n_out), lambda i: (i, 0)),
        compiler_params=_cparams(1),
        name="channel_dft",
    )(x, w)


def _attn_kernel(q_ref, k_ref, v_ref, bias_ref, mb_ref, o_ref, smq_ref, s_a, s_b, s_c, s_d,
                 p_a, p_b, p_c, p_d, l_a, l_b, l_c, l_d, *, rows, seq_len):
    lp = o_ref.shape[1]
    n_win = NA_ROWS * GRID_W
    lane = lax.broadcasted_iota(jnp.int32, (1, LANE), 1)
    head0 = lane < D_HEAD
    nt = (((1,), (1,)), ((), ()))

    km16 = k_ref[0, 0:N_META, :]
    zero16 = jnp.zeros_like(km16)
    km = jnp.concatenate([jnp.where(head0, km16, zero16), jnp.where(head0, zero16, km16),
                          k_ref[0, 2 * N_META:LANE, :]], axis=0)
    vm = jnp.concatenate([v_ref[0, 0:N_META, :], v_ref[0, 0:N_META, :], v_ref[0, 2 * N_META:LANE, :]], axis=0)
    mb = mb_ref[0]

    def stack(q):
        q = q * jnp.asarray(D_HEAD ** -0.5, q.dtype)
        zero = jnp.zeros_like(q)
        return jnp.concatenate([jnp.where(head0, q, zero), jnp.where(head0, zero, q)], axis=0)

    def unstack(o, n):
        return jnp.where(head0, o[0:n], o[n:2 * n])

    q_all = q_ref[0] * jnp.asarray(D_HEAD ** -0.5, q_ref.dtype)
    smq_ref[...] = lax.dot_general(q_all, km, nt, preferred_element_type=F32)

    def meta_scores(q0, n, bias):
        blk = smq_ref[pl.ds(q0, n), :]
        return jnp.concatenate([blk, blk], axis=0) + bias

    sm = meta_scores(0, N_META, jnp.concatenate([mb[0:N_META], mb[GRID_W:GRID_W + N_META]], axis=0))
    pm = jnp.exp2(sm - jnp.max(sm, axis=-1, keepdims=True))
    om = jnp.dot(pm.astype(BF16), vm, preferred_element_type=F32)
    om = om / jnp.sum(pm, axis=-1, keepdims=True)
    o_ref[0, 0:N_META, :] = unstack(om, N_META).astype(o_ref.dtype)

    def window(t):
        t = jnp.minimum(t, rows - 1)
        rs = jnp.clip(t - NA_ROWS // 2, 0, rows - NA_ROWS)
        q0 = pl.multiple_of(N_META + t * GRID_W, 16)
        k0 = pl.multiple_of(N_META + rs * GRID_W, 16)
        return q0, k0, t - rs

    def scores(t, s_ref):
        q0, k0, off = window(t)
        qs = stack(q_ref[0, pl.ds(q0, GRID_W), :])
        kw = k_ref[0, pl.ds(k0, n_win), :]
        s_ref[...] = lax.dot_general(qs, kw, nt, preferred_element_type=F32) + bias_ref[0, off]

    dyn_zero = pl.multiple_of(jnp.minimum(pl.program_id(0), 0), LANE)

    def lane_tiles(x):
        return [x[:, i:i + LANE] for i in range(0, x.shape[1], LANE)]

    def softmax(t, s_ref, p_ref, l_ref):
        q0, _, _ = window(t)
        s = s_ref[pl.ds(dyn_zero, LANE), :]
        sm = meta_scores(q0, GRID_W, mb)
        m = jnp.max(functools.reduce(jnp.maximum, lane_tiles(s) + [sm]), axis=-1, keepdims=True)
        p = jnp.exp2(s - m)
        pm = jnp.exp2(sm - m)
        l_ref[...] = jnp.sum(functools.reduce(jnp.add, lane_tiles(p) + [pm]), axis=-1, keepdims=True)
        p_ref[:, 0:n_win] = p.astype(BF16)
        p_ref[:, n_win:] = pm.astype(BF16)

    def values(t, p_ref, l_ref):
        q0, k0, _ = window(t)
        vw = v_ref[0, pl.ds(k0, n_win), :]
        o = (jnp.dot(p_ref[:, 0:n_win], vw, preferred_element_type=F32)
             + jnp.dot(p_ref[:, n_win:], vm, preferred_element_type=F32))
        o = o / l_ref[...]
        o_ref[0, pl.ds(q0, GRID_W), :] = unstack(o, GRID_W).astype(o_ref.dtype)

    def step(t, s_in, s_out, pl_in, pl_out):
        scores(t + 4, s_out[0])
        scores(t + 5, s_out[1])
        values(t, *pl_in[0])
        values(t + 1, *pl_in[1])
        softmax(t + 2, s_in[0], *pl_out[0])
        softmax(t + 3, s_in[1], *pl_out[1])

    set_0 = ((p_a, l_a), (p_b, l_b))
    set_1 = ((p_c, l_c), (p_d, l_d))
    scores(0, s_c)
    scores(1, s_d)
    softmax(0, s_c, *set_0[0])
    softmax(1, s_d, *set_0[1])
    scores(2, s_a)
    scores(3, s_b)

    def quad_body(u, carry):
        t = 4 * u
        step(t, (s_a, s_b), (s_c, s_d), set_0, set_1)
        step(t + 2, (s_c, s_d), (s_a, s_b), set_1, set_0)
        return carry

    lax.fori_loop(0, rows // 4, quad_body, 0)
    if lp > seq_len:
        o_ref[0, seq_len:lp, :] = jnp.zeros((lp - seq_len, LANE), o_ref.dtype)


def _attention(z3, bias_tab, mb_tab, *, rows, seq_len):
    b, lp, _ = z3.shape
    assert rows % 4 == 0
    n_pairs = N_ATT_HEADS // 2
    n_keys = NA_ROWS * GRID_W
    blk = lambda off: pl.BlockSpec((1, lp, LANE), lambda hp, bi: (bi, 0, off + hp))
    scratch = ([pltpu.VMEM((lp, LANE), F32)]
               + [pltpu.VMEM((LANE, n_keys), F32) for _ in range(4)]
               + [pltpu.VMEM((LANE, n_keys + LANE), BF16) for _ in range(4)]
               + [pltpu.VMEM((LANE, 1), F32) for _ in range(4)])
    return pl.pallas_call(
        functools.partial(_attn_kernel, rows=rows, seq_len=seq_len),
        out_shape=jax.ShapeDtypeStruct((b, lp, D_ATT), BF16),
        grid=(n_pairs, b),
        in_specs=[blk(0), blk(n_pairs), blk(2 * n_pairs),
                  pl.BlockSpec((1, NA_ROWS, LANE, n_keys), lambda hp, bi: (hp, 0, 0, 0)),
                  pl.BlockSpec((1, LANE, LANE), lambda hp, bi: (hp, 0, 0))],
        out_specs=pl.BlockSpec((1, lp, LANE), lambda hp, bi: (bi, 0, hp)),
        scratch_shapes=scratch,
        compiler_params=_cparams(2),
        name="nbr_attention",
    )(z3, z3, z3, bias_tab, mb_tab)


def _attn_bias_tables(rel_bias, meta_bias):
    h, n_dr, n_dc = rel_bias.shape
    c = np.arange(GRID_W)[:, None]
    j = np.arange(GRID_W)[None, :]
    cs = np.clip(c - NA_COLS // 2, 0, GRID_W - NA_COLS)
    valid = (j >= cs) & (j < cs + NA_COLS)
    ext = jnp.zeros((h, n_dr, 2 * GRID_W), F32)
    ext = lax.dynamic_update_slice(ext, rel_bias.astype(F32), (0, 0, GRID_W - NA_COLS))
    skew = jnp.tile(ext, (1, 1, GRID_W))[:, :, :GRID_W * (2 * GRID_W - 1)]
    toep = skew.reshape(h, n_dr, GRID_W, 2 * GRID_W - 1)[..., GRID_W - 1:]
    toep = jnp.where(valid[None, None], toep * LOG2E, NEG_BIG)
    t = jnp.stack([toep[:, NA_ROWS - 1 - oi:2 * NA_ROWS - 1 - oi] for oi in range(NA_ROWS)], axis=1)
    t = t.transpose(0, 1, 3, 2, 4)
    t = t.reshape(h // 2, 2, NA_ROWS, GRID_W, NA_ROWS * GRID_W)
    t = t.transpose(0, 2, 1, 3, 4).reshape(h // 2, NA_ROWS, 2 * GRID_W, NA_ROWS * GRID_W)
    mbp = (meta_bias.astype(F32) * LOG2E).reshape(h // 2, 2, N_META)
    neg = jnp.full((h // 2, N_META), NEG_BIG, F32)
    mb0 = jnp.concatenate([mbp[:, 0], neg], axis=-1)
    mb1 = jnp.concatenate([neg, mbp[:, 1]], axis=-1)
    mb = jnp.stack([mb0, mb1], axis=1)
    mb = jnp.pad(mb, ((0, 0), (0, 0), (0, LANE - 2 * N_META)), constant_values=NEG_BIG)
    mb = jnp.broadcast_to(mb[:, :, None, :], (h // 2, 2, GRID_W, LANE))
    return t, mb.reshape(h // 2, 2 * GRID_W, LANE)


def _fft_kernel(c_ref, s_ref, ab_ref, w_ref, o_ref):
    a = ab_ref[0, :, 0:D_FFT]
    b = ab_ref[0, :, D_FFT:2 * D_FFT]
    f = (jnp.dot(c_ref[...], a, preferred_element_type=F32)
         + jnp.dot(s_ref[...], b, preferred_element_type=F32))
    o_ref[0] = jnp.dot(f.astype(BF16), w_ref[...], preferred_element_type=F32).astype(o_ref.dtype)


def _fourier(cmat, smat, ab3, w_bd, *, tk):
    b, lp, _ = ab3.shape
    return pl.pallas_call(
        _fft_kernel,
        out_shape=jax.ShapeDtypeStruct((b, lp, D_FFT), BF16),
        grid=(lp // tk, b),
        in_specs=[pl.BlockSpec((tk, lp), lambda j, bi: (j, 0)),
                  pl.BlockSpec((tk, lp), lambda j, bi: (j, 0)),
                  pl.BlockSpec((1, lp, 2 * D_FFT), lambda j, bi: (bi, 0, 0)),
                  pl.BlockSpec((D_FFT, D_FFT), lambda j, bi: (0, 0))],
        out_specs=pl.BlockSpec((1, tk, D_FFT), lambda j, bi: (bi, j, 0)),
        compiler_params=_cparams(2),
        name="fourier_mix",
    )(cmat, smat, ab3, w_bd)


def _twiddles(seq_len, lp):
    theta = 2.0 * math.pi / seq_len
    k = jnp.arange(lp, dtype=jnp.int32)[:, None]
    t1 = (GRID_W * jnp.arange(lp // GRID_W, dtype=jnp.int32))[None, :]
    t0 = jnp.arange(GRID_W, dtype=jnp.int32)[None, :]
    ang_a = ((k * t1) % seq_len).astype(F32) * theta
    ang_b = ((k * t0) % seq_len).astype(F32) * theta
    ca, sa = jnp.cos(ang_a)[:, :, None], jnp.sin(ang_a)[:, :, None]
    cb, sb = jnp.cos(ang_b)[:, None, :], jnp.sin(ang_b)[:, None, :]
    cmat = (ca * cb - sa * sb).reshape(lp, lp)
    smat = (sa * cb + ca * sb).reshape(lp, lp)
    idx = jnp.arange(lp)
    valid = (idx[:, None] < seq_len) & (idx[None, :] < seq_len)
    scale = seq_len ** -0.5
    cmat = jnp.where(valid, cmat * scale, 0.0).astype(BF16)
    smat = jnp.where(valid, -smat * scale, 0.0).astype(BF16)
    return cmat, smat


def _channel_dft_matrix():
    c = np.arange(D_FFT_HEAD)
    ang = 2.0 * np.pi * ((c[:, None] * c[None, :]) % D_FFT_HEAD) / D_FFT_HEAD
    cc = np.cos(ang) / math.sqrt(D_FFT_HEAD)
    sc = np.sin(ang) / math.sqrt(D_FFT_HEAD)
    out = np.zeros((D_FFT, 2 * D_FFT), np.float32)
    for g in range(N_FFT_HEADS):
        sl = slice(g * D_FFT_HEAD, (g + 1) * D_FFT_HEAD)
        out[sl, sl] = cc
        out[sl, D_FFT + g * D_FFT_HEAD:D_FFT + (g + 1) * D_FFT_HEAD] = sc
    return jnp.asarray(out, BF16)


def _block_diag(w):
    g, c, e = w.shape
    out = jnp.zeros((g * c, g * e), w.dtype)
    for i in range(g):
        out = lax.dynamic_update_slice(out, w[i], (i * c, i * e))
    return out


POOL_PAD = 16


def _pool_kernel(u_ref, w_ref, sc_ref, o_ref, s0, s1, s2, s3, s4, *, seq_len):
    lp = o_ref.shape[1]
    r_tot = lp + 2 * POOL_PAD
    lo, hi = 8, r_tot - 8
    row = lax.broadcasted_iota(jnp.int32, (lp, 1), 0)
    live = row < seq_len
    x = jnp.where(live, u_ref[0].astype(F32), 0.0)

    zeros_pad = jnp.zeros((POOL_PAD, D_POOL), F32)
    s0[0:POOL_PAD, :] = zeros_pad
    s0[POOL_PAD:POOL_PAD + lp, :] = x
    s0[POOL_PAD + lp:r_tot, :] = zeros_pad
    for s in (s1, s2, s3, s4):
        s[0:POOL_PAD, :] = zeros_pad
        s[POOL_PAD + lp:r_tot, :] = zeros_pad
    s1[lo:hi, :] = s0[lo - 1:hi - 1, :] + s0[lo:hi, :]
    s2[lo:hi, :] = s1[lo - 1:hi - 1, :] + s1[lo + 1:hi + 1, :]
    s3[lo:hi, :] = s2[lo - 2:hi - 2, :] + s2[lo + 2:hi + 2, :]
    s4[lo:hi, :] = s3[lo - 4:hi - 4, :] + s3[lo + 4:hi + 4, :]

    lane = lax.broadcasted_iota(jnp.int32, (1, D_POOL), 1)
    group = jnp.right_shift(lane, 6)
    half = jnp.where(group == 0, 1, jnp.where(group == 1, 2, jnp.where(group == 2, 4, 8)))
    cnt = jnp.minimum(row + half, seq_len) - jnp.maximum(row - half, 0)
    cnt = jnp.maximum(cnt, 1).astype(F32)
    sl = slice(POOL_PAD, POOL_PAD + lp)
    wsum = jnp.where(group == 0, s1[sl, :],
                     jnp.where(group == 1, s2[sl, :], jnp.where(group == 2, s3[sl, :], s4[sl, :])))
    p = jnp.where(live, wsum / cnt - x, 0.0)
    y = jnp.dot(p.astype(BF16), w_ref[...], preferred_element_type=F32) * sc_ref[...]
    o_ref[0] = y.astype(o_ref.dtype)


def _pool(z3, w_bd, scale, *, seq_len):
    b, lp, d_in = z3.shape
    col_block = (d_in - D_POOL) // D_POOL
    scratch = [pltpu.VMEM((lp + 2 * POOL_PAD, D_POOL), F32) for _ in range(5)]
    return pl.pallas_call(
        functools.partial(_pool_kernel, seq_len=seq_len),
        out_shape=jax.ShapeDtypeStruct((b, lp, D_POOL), BF16),
        grid=(b,),
        in_specs=[pl.BlockSpec((1, lp, D_POOL), lambda bi: (bi, 0, col_block)),
                  pl.BlockSpec((D_POOL, D_POOL), lambda bi: (0, 0)),
                  pl.BlockSpec((1, D_POOL), lambda bi: (0, 0))],
        out_specs=pl.BlockSpec((1, lp, D_POOL), lambda bi: (bi, 0, 0)),
        scratch_shapes=scratch,
        compiler_params=_cparams(1),
        name="pool_mix",
    )(z3, w_bd, scale.reshape(1, D_POOL))


def _outproj_kernel(a_ref, f_ref, p_ref, h_ref, g_ref, w_ref, o_ref):
    an = _rms(a_ref[...].astype(F32), g_ref[:, 0:D_ATT]).astype(BF16)
    fn = _rms(f_ref[...].astype(F32), g_ref[:, D_ATT:D_ATT + D_FFT]).astype(BF16)
    pn = _rms(p_ref[...].astype(F32), g_ref[:, D_ATT + D_FFT:]).astype(BF16)
    acc = jnp.dot(an, w_ref[0:D_ATT, :], preferred_element_type=F32)
    acc += jnp.dot(fn, w_ref[D_ATT:D_ATT + D_FFT, :], preferred_element_type=F32)
    acc += jnp.dot(pn, w_ref[D_ATT + D_FFT:, :], preferred_element_type=F32)
    o_ref[...] = h_ref[...] + acc


def _outproj(a, f, p, h, g, w, *, tm=1024):
    n, d = h.shape
    row = lambda width: pl.BlockSpec((tm, width), lambda i: (i, 0))
    return pl.pallas_call(
        _outproj_kernel,
        out_shape=jax.ShapeDtypeStruct((n, d), F32),
        grid=(n // tm,),
        in_specs=[row(D_ATT), row(D_FFT), row(D_POOL), row(d),
                  pl.BlockSpec((1, d), lambda i: (0, 0)),
                  pl.BlockSpec((d, d), lambda i: (0, 0))],
        out_specs=row(d),
        compiler_params=_cparams(1),
        name="out_proj",
    )(a, f, p, h, g.reshape(1, d), w)


def _ffn_kernel(h_ref, g_ref, wg_ref, wu_ref, wd_ref, o_ref, xn_ref, acc_ref):
    c = pl.program_id(1)

    @pl.when(c == 0)
    def _():
        xn_ref[...] = _rms(h_ref[...], g_ref[...]).astype(BF16)

    xn = xn_ref[...]
    gate = jnp.dot(xn, wg_ref[...], preferred_element_type=F32)
    up = jnp.dot(xn, wu_ref[...], preferred_element_type=F32)
    hh = (gate * jax.nn.sigmoid(gate) * up).astype(BF16)
    part = jnp.dot(hh, wd_ref[...], preferred_element_type=F32)

    @pl.when(c == 0)
    def _():
        acc_ref[...] = part

    @pl.when(c > 0)
    def _():
        acc_ref[...] += part

    @pl.when(c == pl.num_programs(1) - 1)
    def _():
        o_ref[...] = h_ref[...] + acc_ref[...]


def _ffn(h, g, wg, wu, wd, *, tm=1024, n_chunks=2):
    n, d = h.shape
    d_ff = wg.shape[1]
    fc = d_ff // n_chunks
    return pl.pallas_call(
        _ffn_kernel,
        out_shape=jax.ShapeDtypeStruct((n, d), F32),
        grid=(n // tm, n_chunks),
        in_specs=[pl.BlockSpec((tm, d), lambda i, c: (i, 0)),
                  pl.BlockSpec((1, d), lambda i, c: (0, 0)),
                  pl.BlockSpec((d, fc), lambda i, c: (0, c)),
                  pl.BlockSpec((d, fc), lambda i, c: (0, c)),
                  pl.BlockSpec((fc, d), lambda i, c: (c, 0))],
        out_specs=pl.BlockSpec((tm, d), lambda i, c: (i, 0)),
        scratch_shapes=[pltpu.VMEM((tm, d), BF16), pltpu.VMEM((tm, d), F32)],
        compiler_params=_cparams(2),
        name="swiglu_ffn",
    )(h, g.reshape(1, d), wg, wu, wd)


MOE_TILE = 1024
EXPERT_ROWS = 512
R_GATE, R_EXPERT, R_RANK = 0, 2, 4


def _router_kernel(h_ref, g_ref, wr_ref, tri_ref, xn_ref, rec_ref, cnt_ref, *, seq_len, lp):
    t = h_ref.shape[0]
    y = _rms(h_ref[...], g_ref[...])
    xn_ref[...] = y
    logits = jnp.dot(y, wr_ref[...], preferred_element_type=F32, precision=lax.Precision.HIGHEST)
    lane = lax.broadcasted_iota(jnp.int32, (t, LANE), 1)
    lg = jnp.where(lane < N_EXPERTS, logits, -jnp.inf)
    m1 = jnp.max(lg, axis=-1, keepdims=True)
    i1 = jnp.min(jnp.where(lg == m1, lane, LANE), axis=-1, keepdims=True)
    lg2 = jnp.where(lane == i1, -jnp.inf, lg)
    m2 = jnp.max(lg2, axis=-1, keepdims=True)
    i2 = jnp.min(jnp.where(lg2 == m2, lane, LANE), axis=-1, keepdims=True)
    e2 = jnp.exp(m2 - m1)
    g1 = 1.0 / (1.0 + e2)
    g2 = e2 / (1.0 + e2)
    rowf = (pl.program_id(0) * t + lax.broadcasted_iota(jnp.int32, (t, 1), 0)).astype(F32)
    seq = jnp.floor((rowf + 0.5) * (1.0 / lp))
    live = (rowf - seq * lp) < seq_len
    first = lane == i1
    second = lane == i2
    member = jnp.where((first | second) & live, 1.0, 0.0)
    rank = jnp.dot(tri_ref[...], member.astype(BF16), preferred_element_type=F32)
    r1 = jnp.sum(jnp.where(first, rank, 0.0), axis=-1, keepdims=True)
    r2 = jnp.sum(jnp.where(second, rank, 0.0), axis=-1, keepdims=True)
    dead = jnp.logical_not(live)
    rec = jnp.zeros((t, LANE), F32)
    for ln, val in ((R_GATE, jnp.where(dead, 0.0, g1)), (R_GATE + 1, jnp.where(dead, 0.0, g2)),
                    (R_EXPERT, jnp.where(dead, -1.0, i1.astype(F32))),
                    (R_EXPERT + 1, jnp.where(dead, -1.0, i2.astype(F32))),
                    (R_RANK, r1), (R_RANK + 1, r2)):
        rec = jnp.where(lane == ln, val, rec)
    rec_ref[...] = rec
    cnt_ref[0] = jnp.broadcast_to(jnp.sum(member, axis=0, keepdims=True), (8, LANE))


def _router(h, g, w_router_pad, tri, *, seq_len, lp):
    n, d = h.shape
    t = MOE_TILE
    row = lambda width: pl.BlockSpec((t, width), lambda i: (i, 0))
    return pl.pallas_call(
        functools.partial(_router_kernel, seq_len=seq_len, lp=lp),
        out_shape=(jax.ShapeDtypeStruct((n, d), F32),
                   jax.ShapeDtypeStruct((n, LANE), F32),
                   jax.ShapeDtypeStruct((n // t, 8, LANE), F32)),
        grid=(n // t,),
        in_specs=[row(d),
                  pl.BlockSpec((1, d), lambda i: (0, 0)),
                  pl.BlockSpec((d, LANE), lambda i: (0, 0)),
                  pl.BlockSpec((t, t), lambda i: (0, 0))],
        out_specs=(row(d), row(LANE), pl.BlockSpec((1, 8, LANE), lambda i: (i, 0, 0))),
        compiler_params=_cparams(1),
        name="moe_router",
    )(h, g.reshape(1, d), w_router_pad, tri)


def _dispatch_plan(rec, counts, pad_token, *, n_row_tiles):
    n = rec.shape[0]
    n_tiles = counts.shape[0]
    cnt = counts[:, 0, :N_EXPERTS]
    before = jnp.cumsum(cnt, axis=0) - cnt
    total = jnp.sum(cnt, axis=0)
    padded = jnp.ceil(total / EXPERT_ROWS) * EXPERT_ROWS
    ends = jnp.cumsum(padded)
    start = ends - padded
    base = (start[None, :] + before)[:, None, :]
    pad_rank = np.cumsum(np.asarray(pad_token, np.int64)) - 1
    spill = n_row_tiles * EXPERT_ROWS + 2 * pad_rank.reshape(n_tiles, MOE_TILE)
    slots_out, slots_back = [], []
    for k in range(2):
        e = rec[:, R_EXPERT + k].reshape(n_tiles, MOE_TILE, 1)
        r = rec[:, R_RANK + k].reshape(n_tiles, MOE_TILE)
        hit = e == jnp.arange(N_EXPERTS, dtype=F32)[None, None, :]
        slot = (jnp.sum(jnp.where(hit, base, 0.0), axis=-1) + r).astype(jnp.int32)
        dead = e[..., 0] < 0
        slots_out.append(jnp.where(dead, jnp.asarray(spill + k, jnp.int32), slot))
        slots_back.append(jnp.where(dead, 0, slot))
    pos_out = jnp.stack(slots_out, axis=-1).reshape(n * 2)
    pos_back = jnp.stack(slots_back, axis=-1).reshape(n * 2)
    tile_row0 = jnp.arange(n_row_tiles, dtype=F32) * EXPERT_ROWS
    tile_expert = jnp.sum(tile_row0[:, None] >= ends[None, :], axis=-1)
    tile_live = (tile_row0 < ends[-1]).astype(jnp.int32)
    tile_expert = jnp.minimum(tile_expert, N_EXPERTS - 1).astype(jnp.int32)
    return pos_out, pos_back, tile_expert, tile_live


ROW_GROUP = 8


def _row_copy(src_ref, src_row, dst_ref, dst_row, sem):
    return pltpu.make_async_copy(src_ref.at[pl.ds(src_row, 1)], dst_ref.at[pl.ds(dst_row, 1)], sem)


def _dispatch_kernel(pos_ref, x_ref, zero_ref, xs_ref, sem):
    del zero_ref
    t = x_ref.shape[0]
    base = pl.program_id(0) * t * 2

    def issue(g, carry):
        row0 = pl.multiple_of(g * ROW_GROUP, ROW_GROUP)
        group = x_ref.at[pl.ds(row0, ROW_GROUP)]
        idx0 = base + 2 * row0
        for c in range(ROW_GROUP):
            for k in range(2):
                p = pos_ref[idx0 + (2 * c + k)]
                _row_copy(group, c, xs_ref, p, sem).start(priority=k)
        return carry

    def drain(g, carry):
        for _ in range(2 * ROW_GROUP):
            _row_copy(x_ref, 0, xs_ref, 0, sem).wait()
        return carry

    lax.fori_loop(0, t // ROW_GROUP, issue, 0)
    lax.fori_loop(0, t // ROW_GROUP, drain, 0)


def _dispatch(pos, xn, n_rows):
    n, d = xn.shape
    t = MOE_TILE
    grid_spec = pltpu.PrefetchScalarGridSpec(
        num_scalar_prefetch=1,
        grid=(n // t,),
        in_specs=[pl.BlockSpec((t, d), lambda i, pos: (i, 0)),
                  pl.BlockSpec(memory_space=pl.ANY)],
        out_specs=pl.BlockSpec(memory_space=pl.ANY),
        scratch_shapes=[pltpu.SemaphoreType.DMA],
    )
    return pl.pallas_call(
        _dispatch_kernel,
        out_shape=jax.ShapeDtypeStruct((n_rows, d), F32),
        grid_spec=grid_spec,
        input_output_aliases={2: 0},
        compiler_params=_cparams(1),
        name="moe_dispatch",
    )(pos, xn, jnp.zeros((n_rows, d), F32))


def _expert_kernel(te_ref, live_ref, x_ref, wg_ref, wu_ref, wd_ref, o_ref, xb_ref, acc_ref):
    del te_ref
    i, c = pl.program_id(0), pl.program_id(1)
    last = pl.num_programs(1) - 1
    live = live_ref[i] > 0

    @pl.when(live)
    def _():
        @pl.when(c == 0)
        def _():
            xb_ref[...] = x_ref[...].astype(BF16)

        xb = xb_ref[...]
        gate = jnp.dot(xb, wg_ref[0, 0], preferred_element_type=F32)
        up = jnp.dot(xb, wu_ref[0, 0], preferred_element_type=F32)
        hh = (gate * jax.nn.sigmoid(gate) * up).astype(BF16)
        part = jnp.dot(hh, wd_ref[0, 0], preferred_element_type=F32)

        @pl.when(c == 0)
        def _():
            acc_ref[...] = part

        @pl.when(c > 0)
        def _():
            acc_ref[...] += part

        @pl.when(c == last)
        def _():
            o_ref[...] = acc_ref[...]

    @pl.when(jnp.logical_not(live) & (c == last))
    def _():
        o_ref[...] = jnp.zeros(o_ref.shape, o_ref.dtype)


def _experts(tile_expert, tile_live, xs, n_row_tiles, layer, wg, wu, wd, *, n_chunks=2):
    d = xs.shape[1]
    tm = EXPERT_ROWS
    n_rows = n_row_tiles * tm
    fc = wg.shape[3] // n_chunks
    chunk = lambda i, c, te, lv: jnp.where(lv[i] > 0, c, n_chunks - 1)
    grid_spec = pltpu.PrefetchScalarGridSpec(
        num_scalar_prefetch=2,
        grid=(n_rows // tm, n_chunks),
        in_specs=[pl.BlockSpec((tm, d), lambda i, c, te, lv: (i, 0)),
                  pl.BlockSpec((1, 1, d, fc), lambda i, c, te, lv: (layer, te[i], 0, chunk(i, c, te, lv))),
                  pl.BlockSpec((1, 1, d, fc), lambda i, c, te, lv: (layer, te[i], 0, chunk(i, c, te, lv))),
                  pl.BlockSpec((1, 1, fc, d), lambda i, c, te, lv: (layer, te[i], chunk(i, c, te, lv), 0))],
        out_specs=pl.BlockSpec((tm, d), lambda i, c, te, lv: (i, 0)),
        scratch_shapes=[pltpu.VMEM((tm, d), BF16), pltpu.VMEM((tm, d), F32)],
    )
    return pl.pallas_call(
        _expert_kernel,
        out_shape=jax.ShapeDtypeStruct((n_rows, d), F32),
        grid_spec=grid_spec,
        compiler_params=_cparams(2),
        name="moe_experts",
    )(tile_expert, tile_live, xs, wg, wu, wd)


def _combine_kernel(pos_ref, h_ref, rec_ref, ys_ref, o_ref, buf0, buf1, sem):
    t = h_ref.shape[0]
    base = pl.program_id(0) * t * 2
    bufs = (buf0, buf1)

    def issue(g, carry):
        row0 = pl.multiple_of(g * ROW_GROUP, ROW_GROUP)
        groups = [buf.at[pl.ds(row0, ROW_GROUP)] for buf in bufs]
        idx0 = base + 2 * row0
        for c in range(ROW_GROUP):
            for k in range(2):
                p = pos_ref[idx0 + (2 * c + k)]
                _row_copy(ys_ref, p, groups[k], c, sem).start(priority=k)
        return carry

    def drain(g, carry):
        for _ in range(2 * ROW_GROUP):
            _row_copy(ys_ref, 0, buf0, 0, sem).wait()
        return carry

    lax.fori_loop(0, t // ROW_GROUP, issue, 0)
    lax.fori_loop(0, t // ROW_GROUP, drain, 0)
    rec = rec_ref[...]
    g1 = rec[:, R_GATE:R_GATE + 1]
    g2 = rec[:, R_GATE + 1:R_GATE + 2]
    o_ref[...] = h_ref[...] + g1 * buf0[...] + g2 * buf1[...]


def _combine(pos, h, rec, ys):
    n, d = h.shape
    t = MOE_TILE
    grid_spec = pltpu.PrefetchScalarGridSpec(
        num_scalar_prefetch=1,
        grid=(n // t,),
        in_specs=[pl.BlockSpec((t, d), lambda i, pos: (i, 0)),
                  pl.BlockSpec((t, LANE), lambda i, pos: (i, 0)),
                  pl.BlockSpec(memory_space=pl.ANY)],
        out_specs=pl.BlockSpec((t, d), lambda i, pos: (i, 0)),
        scratch_shapes=[pltpu.VMEM((t, d), F32), pltpu.VMEM((t, d), F32), pltpu.SemaphoreType.DMA],
    )
    return pl.pallas_call(
        _combine_kernel,
        out_shape=jax.ShapeDtypeStruct((n, d), F32),
        grid_spec=grid_spec,
        compiler_params=_cparams(1),
        name="moe_combine",
    )(pos, h, rec, ys)


def _norm_kernel(x_ref, tail_ref, g_ref, o_ref):
    tm = x_ref.shape[1]
    o_ref[0, 0:tm - N_META, :] = _rms(x_ref[0, N_META:tm, :], g_ref[...])
    o_ref[0, tm - N_META:tm, :] = _rms(tail_ref[0], g_ref[...])


def _final_norm(h3, g, *, t_len, tm=1024):
    b, lp, d = h3.shape
    assert t_len % tm == 0 and tm % N_META == 0
    return pl.pallas_call(
        _norm_kernel,
        out_shape=jax.ShapeDtypeStruct((b, t_len, d), F32),
        grid=(b, t_len // tm),
        in_specs=[pl.BlockSpec((1, tm, d), lambda bi, j: (bi, j, 0)),
                  pl.BlockSpec((1, N_META, d), lambda bi, j: (bi, (j + 1) * (tm // N_META), 0)),
                  pl.BlockSpec((1, d), lambda bi, j: (0, 0))],
        out_specs=pl.BlockSpec((1, tm, d), lambda bi, j: (bi, j, 0)),
        compiler_params=_cparams(2),
        name="final_norm",
    )(h3, h3, g.reshape(1, d))


def _fft_row_tile(lp):
    best = 16
    for tk in range(16, lp + 1, 16):
        if lp % tk == 0 and tk * lp * 2 <= 4608 * 1024:
            best = tk
    return best


def _trunk(x, meta_tokens, prm):
    b, t_len, d = x.shape
    depth = len(prm["w_in"])
    seq_len = N_META + t_len
    lp = -(-seq_len // LANE) * LANE
    n = b * lp
    rows = t_len // GRID_W
    assert t_len % GRID_W == 0 and rows >= NA_ROWS and n % MOE_TILE == 0

    meta = jnp.broadcast_to(meta_tokens.astype(x.dtype)[None], (b, N_META, d))
    h = jnp.concatenate([meta, x, jnp.zeros((b, lp - seq_len, d), x.dtype)], axis=1).reshape(n, d)

    cmat, smat = _twiddles(seq_len, lp)
    tk = _fft_row_tile(lp)
    tri = jnp.tril(jnp.ones((MOE_TILE, MOE_TILE), BF16), -1)

    for i in range(depth):
        z = _norm_matmul(h, prm["norm_mix"][i], prm["w_in"][i])
        z3 = z.reshape(b, lp, D_IN)
        ab = _matmul_cols(z, prm["chan_dft"], (3 * D_ATT) // D_FFT)
        a = _attention(z3, prm["attn_bias"][i], prm["attn_mbias"][i], rows=rows, seq_len=seq_len)
        f = _fourier(cmat, smat, ab.reshape(b, lp, 2 * D_FFT), prm["w_fft"][i], tk=tk)
        p = _pool(z3, prm["w_pool"][i], prm["pool_scale"][i], seq_len=seq_len)
        h = _outproj(a.reshape(n, D_ATT), f.reshape(n, D_FFT), p.reshape(n, D_POOL), h,
                     prm["norm_groups"][i], prm["w_out"][i])
        j = i // 2
        if i % 2 == 0:
            h = _ffn(h, prm["norm_ffn"][i], prm["w_ff_gate"][j], prm["w_ff_up"][j], prm["w_ff_down"][j])
        else:
            xn, rec, counts = _router(h, prm["norm_ffn"][i], prm["w_router"][j], tri,
                                      seq_len=seq_len, lp=lp)
            n_row_tiles = -(-2 * b * seq_len // EXPERT_ROWS) + N_EXPERTS
            pad_token = (np.arange(n) % lp) >= seq_len
            pos_out, pos_back, tile_expert, tile_live = _dispatch_plan(rec, counts, pad_token,
                                                                       n_row_tiles=n_row_tiles)
            n_spill = -(-2 * int(pad_token.sum()) // ROW_GROUP) * ROW_GROUP
            xs = _dispatch(pos_out, xn, n_row_tiles * EXPERT_ROWS + n_spill)
            ys = _experts(tile_expert, tile_live, xs, n_row_tiles, j,
                          prm["w_exp_gate"], prm["w_exp_up"], prm["w_exp_down"])
            h = _combine(pos_back, h, rec, ys)
    return _final_norm(h.reshape(b, lp, d), prm["norm_final"], t_len=t_len)


def kernel(x_prompt, x_sample, meta_tokens, norm_mix, w_in, w_fft, w_pool, pool_scale, rel_bias, meta_bias,
           norm_groups, w_out, norm_ffn, w_ff_gate, w_ff_up, w_ff_down, w_router, w_exp_gate, w_exp_up,
           w_exp_down, norm_final):
    depth = w_in.shape[0]
    tabs = [_attn_bias_tables(rel_bias[i], meta_bias[i]) for i in range(depth)]
    key_scale = np.ones((D_IN,), np.float32)
    key_scale[D_ATT:2 * D_ATT] = LOG2E
    per_layer = lambda w: [w[i].astype(BF16) for i in range(w.shape[0])]
    prm = {
        "norm_mix": norm_mix, "norm_groups": norm_groups, "norm_ffn": norm_ffn, "norm_final": norm_final,
        "pool_scale": pool_scale,
        "w_in": per_layer(w_in * key_scale), "w_out": per_layer(w_out),
        "w_ff_gate": per_layer(w_ff_gate), "w_ff_up": per_layer(w_ff_up), "w_ff_down": per_layer(w_ff_down),
        "w_exp_gate": w_exp_gate.astype(BF16), "w_exp_up": w_exp_up.astype(BF16),
        "w_exp_down": w_exp_down.astype(BF16),
        "w_router": jnp.pad(w_router, ((0, 0), (0, 0), (0, LANE - N_EXPERTS))),
        "w_fft": jnp.stack([_block_diag(w_fft[i]) for i in range(depth)]).astype(BF16),
        "w_pool": jnp.stack([_block_diag(w_pool[i]) for i in range(depth)]).astype(BF16),
        "chan_dft": _channel_dft_matrix(),
        "attn_bias": [tb[0] for tb in tabs], "attn_mbias": [tb[1] for tb in tabs],
    }
    y_prompt = _trunk(x_prompt, meta_tokens, prm)
    y_sample = _trunk(x_sample, meta_tokens, prm)
    return (y_prompt, y_sample)
```

```python
import functools
import math

import numpy as np
import jax
import jax.numpy as jnp
from jax import lax
from jax.experimental import pallas as pl
from jax.experimental.pallas import tpu as pltpu

D_MODEL = 1024
N_META = 16
GRID_W = 64
D_HEAD = 64
D_ATT = 512
N_ATT_HEADS = 8
D_FFT = 256
N_FFT_HEADS = 4
D_FFT_HEAD = 64
D_POOL = 256
POOL_WINDOWS = (2, 4, 8, 16)
D_POOL_GROUP = 64
D_IN = 2048
NA_ROWS = 8
NA_COLS = 16
N_EXPERTS = 8
EPS = 1e-6

LANE = 128
NEG_BIG = -1e30
LOG2E = math.log2(math.e)
VMEM_LIMIT = 56 * 1024 * 1024

F32 = jnp.float32
BF16 = jnp.bfloat16


def _cparams(n_axes, vmem=VMEM_LIMIT):
    return pltpu.CompilerParams(dimension_semantics=("arbitrary",) * n_axes,
                                vmem_limit_bytes=vmem)


def _rms(x, g):
    return x * lax.rsqrt(jnp.mean(x * x, axis=-1, keepdims=True) + EPS) * g


def _norm_matmul_kernel(x_ref, g_ref, w_ref, o_ref, xn_ref, *, tn):
    xn_ref[...] = _rms(x_ref[...], g_ref[...]).astype(BF16)
    for n0 in range(0, o_ref.shape[1], tn):
        o_ref[:, n0:n0 + tn] = jnp.dot(xn_ref[...], w_ref[:, n0:n0 + tn],
                                       preferred_element_type=F32).astype(o_ref.dtype)


def _norm_matmul(x, g, w, *, tm=1024, tn=512):
    n, d = x.shape
    n_out = w.shape[1]
    return pl.pallas_call(
        functools.partial(_norm_matmul_kernel, tn=tn),
        out_shape=jax.ShapeDtypeStruct((n, n_out), BF16),
        grid=(n // tm,),
        in_specs=[pl.BlockSpec((tm, d), lambda i: (i, 0)),
                  pl.BlockSpec((1, d), lambda i: (0, 0)),
                  pl.BlockSpec((d, n_out), lambda i: (0, 0))],
        out_specs=pl.BlockSpec((tm, n_out), lambda i: (i, 0)),
        scratch_shapes=[pltpu.VMEM((tm, d), BF16)],
        compiler_params=_cparams(1),
        name="norm_matmul",
    )(x, g.reshape(1, d), w)


def _matmul_kernel(x_ref, w_ref, o_ref):
    o_ref[...] = jnp.dot(x_ref[...], w_ref[...], preferred_element_type=F32).astype(o_ref.dtype)


def _matmul_cols(x, w, col_block, *, tm=1024):
    n = x.shape[0]
    k, n_out = w.shape
    return pl.pallas_call(
        _matmul_kernel,
        out_shape=jax.ShapeDtypeStruct((n, n_out), BF16),
        grid=(n // tm,),
        in_specs=[pl.BlockSpec((tm, k), lambda i: (i, col_block)),
                  pl.BlockSpec((k, n_out), lambda i: (0, 0))],
        out_specs=pl.BlockSpec((tm, n_out), lambda i: (i, 0)),
        compiler_params=_cparams(1),
        name="channel_dft",
    )(x, w)


def _attn_kernel(q_ref, k_ref, v_ref, bias_ref, mb_ref, o_ref, smq_ref, s_a, s_b, s_c, s_d,
                 p_a, p_b, p_c, p_d, l_a, l_b, l_c, l_d, *, rows, seq_len):
    lp = o_ref.shape[1]
    n_win = NA_ROWS * GRID_W
    lane = lax.broadcasted_iota(jnp.int32, (1, LANE), 1)
    head0 = lane < D_HEAD
    nt = (((1,), (1,)), ((), ()))

    km16 = k_ref[0, 0:N_META, :]
    zero16 = jnp.zeros_like(km16)
    km = jnp.concatenate([jnp.where(head0, km16, zero16), jnp.where(head0, zero16, km16),
                          k_ref[0, 2 * N_META:LANE, :]], axis=0)
    vm = jnp.concatenate([v_ref[0, 0:N_META, :], v_ref[0, 0:N_META, :], v_ref[0, 2 * N_META:LANE, :]], axis=0)
    mb = mb_ref[0]

    def stack(q):
        q = q * jnp.asarray(D_HEAD ** -0.5, q.dtype)
        zero = jnp.zeros_like(q)
        return jnp.concatenate([jnp.where(head0, q, zero), jnp.where(head0, zero, q)], axis=0)

    def unstack(o, n):
        return jnp.where(head0, o[0:n], o[n:2 * n])

    q_all = q_ref[0] * jnp.asarray(D_HEAD ** -0.5, q_ref.dtype)
    smq_ref[...] = lax.dot_general(q_all, km, nt, preferred_element_type=F32)

    def meta_scores(q0, n, bias):
        blk = smq_ref[pl.ds(q0, n), :]
        return jnp.concatenate([blk, blk], axis=0) + bias

    sm = meta_scores(0, N_META, jnp.concatenate([mb[0:N_META], mb[GRID_W:GRID_W + N_META]], axis=0))
    pm = jnp.exp2(sm - jnp.max(sm, axis=-1, keepdims=True))
    om = jnp.dot(pm.astype(BF16), vm, preferred_element_type=F32)
    om = om / jnp.sum(pm, axis=-1, keepdims=True)
    o_ref[0, 0:N_META, :] = unstack(om, N_META).astype(o_ref.dtype)

    def window(t):
        t = jnp.minimum(t, rows - 1)
        rs = jnp.clip(t - NA_ROWS // 2, 0, rows - NA_ROWS)
        q0 = pl.multiple_of(N_META + t * GRID_W, 16)
        k0 = pl.multiple_of(N_META + rs * GRID_W, 16)
        return q0, k0, t - rs

    def scores(t, s_ref):
        q0, k0, off = window(t)
        qs = stack(q_ref[0, pl.ds(q0, GRID_W), :])
        kw = k_ref[0, pl.ds(k0, n_win), :]
        s_ref[...] = lax.dot_general(qs, kw, nt, preferred_element_type=F32) + bias_ref[0, off]

    dyn_zero = pl.multiple_of(jnp.minimum(pl.program_id(0), 0), LANE)

    def lane_tiles(x):
        return [x[:, i:i + LANE] for i in range(0, x.shape[1], LANE)]

    def softmax(t, s_ref, p_ref, l_ref):
        q0, _, _ = window(t)
        s = s_ref[pl.ds(dyn_zero, LANE), :]
        sm = meta_scores(q0, GRID_W, mb)
        m = jnp.max(functools.reduce(jnp.maximum, lane_tiles(s) + [sm]), axis=-1, keepdims=True)
        p = jnp.exp2(s - m)
        pm = jnp.exp2(sm - m)
        l_ref[...] = jnp.sum(functools.reduce(jnp.add, lane_tiles(p) + [pm]), axis=-1, keepdims=True)
        p_ref[:, 0:n_win] = p.astype(BF16)
        p_ref[:, n_win:] = pm.astype(BF16)

    def values(t, p_ref, l_ref):
        q0, k0, _ = window(t)
        vw = v_ref[0, pl.ds(k0, n_win), :]
        o = (jnp.dot(p_ref[:, 0:n_win], vw, preferred_element_type=F32)
             + jnp.dot(p_ref[:, n_win:], vm, preferred_element_type=F32))
        o = o / l_ref[...]
        o_ref[0, pl.ds(q0, GRID_W), :] = unstack(o, GRID_W).astype(o_ref.dtype)

    def step(t, s_in, s_out, pl_in, pl_out):
        scores(t + 4, s_out[0])
        scores(t + 5, s_out[1])
        values(t, *pl_in[0])
        values(t + 1, *pl_in[1])
        softmax(t + 2, s_in[0], *pl_out[0])
        softmax(t + 3, s_in[1], *pl_out[1])

    set_0 = ((p_a, l_a), (p_b, l_b))
    set_1 = ((p_c, l_c), (p_d, l_d))
    scores(0, s_c)
    scores(1, s_d)
    softmax(0, s_c, *set_0[0])
    softmax(1, s_d, *set_0[1])
    scores(2, s_a)
    scores(3, s_b)

    def quad_body(u, carry):
        t = 4 * u
        step(t, (s_a, s_b), (s_c, s_d), set_0, set_1)
        step(t + 2, (s_c, s_d), (s_a, s_b), set_1, set_0)
        return carry

    lax.fori_loop(0, rows // 4, quad_body, 0)
    if lp > seq_len:
        o_ref[0, seq_len:lp, :] = jnp.zeros((lp - seq_len, LANE), o_ref.dtype)


def _attention(z3, bias_tab, mb_tab, *, rows, seq_len):
    b, lp, _ = z3.shape
    assert rows % 4 == 0
    n_pairs = N_ATT_HEADS // 2
    n_keys = NA_ROWS * GRID_W
    blk = lambda off: pl.BlockSpec((1, lp, LANE), lambda hp, bi: (bi, 0, off + hp))
    scratch = ([pltpu.VMEM((lp, LANE), F32)]
               + [pltpu.VMEM((LANE, n_keys), F32) for _ in range(4)]
               + [pltpu.VMEM((LANE, n_keys + LANE), BF16) for _ in range(4)]
               + [pltpu.VMEM((LANE, 1), F32) for _ in range(4)])
    return pl.pallas_call(
        functools.partial(_attn_kernel, rows=rows, seq_len=seq_len),
        out_shape=jax.ShapeDtypeStruct((b, lp, D_ATT), BF16),
        grid=(n_pairs, b),
        in_specs=[blk(0), blk(n_pairs), blk(2 * n_pairs),
                  pl.BlockSpec((1, NA_ROWS, LANE, n_keys), lambda hp, bi: (hp, 0, 0, 0)),
                  pl.BlockSpec((1, LANE, LANE), lambda hp, bi: (hp, 0, 0))],
        out_specs=pl.BlockSpec((1, lp, LANE), lambda hp, bi: (bi, 0, hp)),
        scratch_shapes=scratch,
        compiler_params=_cparams(2),
        name="nbr_attention",
    )(z3, z3, z3, bias_tab, mb_tab)


def _attn_bias_tables(rel_bias, meta_bias):
    h, n_dr, n_dc = rel_bias.shape
    c = np.arange(GRID_W)[:, None]
    j = np.arange(GRID_W)[None, :]
    cs = np.clip(c - NA_COLS // 2, 0, GRID_W - NA_COLS)
    valid = (j >= cs) & (j < cs + NA_COLS)
    ext = jnp.zeros((h, n_dr, 2 * GRID_W), F32)
    ext = lax.dynamic_update_slice(ext, rel_bias.astype(F32), (0, 0, GRID_W - NA_COLS))
    skew = jnp.tile(ext, (1, 1, GRID_W))[:, :, :GRID_W * (2 * GRID_W - 1)]
    toep = skew.reshape(h, n_dr, GRID_W, 2 * GRID_W - 1)[..., GRID_W - 1:]
    toep = jnp.where(valid[None, None], toep * LOG2E, NEG_BIG)
    t = jnp.stack([toep[:, NA_ROWS - 1 - oi:2 * NA_ROWS - 1 - oi] for oi in range(NA_ROWS)], axis=1)
    t = t.transpose(0, 1, 3, 2, 4)
    t = t.reshape(h // 2, 2, NA_ROWS, GRID_W, NA_ROWS * GRID_W)
    t = t.transpose(0, 2, 1, 3, 4).reshape(h // 2, NA_ROWS, 2 * GRID_W, NA_ROWS * GRID_W)
    mbp = (meta_bias.astype(F32) * LOG2E).reshape(h // 2, 2, N_META)
    neg = jnp.full((h // 2, N_META), NEG_BIG, F32)
    mb0 = jnp.concatenate([mbp[:, 0], neg], axis=-1)
    mb1 = jnp.concatenate([neg, mbp[:, 1]], axis=-1)
    mb = jnp.stack([mb0, mb1], axis=1)
    mb = jnp.pad(mb, ((0, 0), (0, 0), (0, LANE - 2 * N_META)), constant_values=NEG_BIG)
    mb = jnp.broadcast_to(mb[:, :, None, :], (h // 2, 2, GRID_W, LANE))
    return t, mb.reshape(h // 2, 2 * GRID_W, LANE)


def _fft_kernel(c_ref, s_ref, ab_ref, w_ref, o_ref):
    a = ab_ref[0, :, 0:D_FFT]
    b = ab_ref[0, :, D_FFT:2 * D_FFT]
    f = (jnp.dot(c_ref[...], a, preferred_element_type=F32)
         + jnp.dot(s_ref[...], b, preferred_element_type=F32))
    o_ref[0] = jnp.dot(f.astype(BF16), w_ref[...], preferred_element_type=F32).astype(o_ref.dtype)


def _fourier(cmat, smat, ab3, w_bd, *, tk):
    b, lp, _ = ab3.shape
    return pl.pallas_call(
        _fft_kernel,
        out_shape=jax.ShapeDtypeStruct((b, lp, D_FFT), BF16),
        grid=(lp // tk, b),
        in_specs=[pl.BlockSpec((tk, lp), lambda j, bi: (j, 0)),
                  pl.BlockSpec((tk, lp), lambda j, bi: (j, 0)),
                  pl.BlockSpec((1, lp, 2 * D_FFT), lambda j, bi: (bi, 0, 0)),
                  pl.BlockSpec((D_FFT, D_FFT), lambda j, bi: (0, 0))],
        out_specs=pl.BlockSpec((1, tk, D_FFT), lambda j, bi: (bi, j, 0)),
        compiler_params=_cparams(2),
        name="fourier_mix",
    )(cmat, smat, ab3, w_bd)


def _twiddles(seq_len, lp):
    theta = 2.0 * math.pi / seq_len
    k = jnp.arange(lp, dtype=jnp.int32)[:, None]
    t1 = (GRID_W * jnp.arange(lp // GRID_W, dtype=jnp.int32))[None, :]
    t0 = jnp.arange(GRID_W, dtype=jnp.int32)[None, :]
    ang_a = ((k * t1) % seq_len).astype(F32) * theta
    ang_b = ((k * t0) % seq_len).astype(F32) * theta
    ca, sa = jnp.cos(ang_a)[:, :, None], jnp.sin(ang_a)[:, :, None]
    cb, sb = jnp.cos(ang_b)[:, None, :], jnp.sin(ang_b)[:, None, :]
    cmat = (ca * cb - sa * sb).reshape(lp, lp)
    smat = (sa * cb + ca * sb).reshape(lp, lp)
    idx = jnp.arange(lp)
    valid = (idx[:, None] < seq_len) & (idx[None, :] < seq_len)
    scale = seq_len ** -0.5
    cmat = jnp.where(valid, cmat * scale, 0.0).astype(BF16)
    smat = jnp.where(valid, -smat * scale, 0.0).astype(BF16)
    return cmat, smat


def _channel_dft_matrix():
    c = np.arange(D_FFT_HEAD)
    ang = 2.0 * np.pi * ((c[:, None] * c[None, :]) % D_FFT_HEAD) / D_FFT_HEAD
    cc = np.cos(ang) / math.sqrt(D_FFT_HEAD)
    sc = np.sin(ang) / math.sqrt(D_FFT_HEAD)
    out = np.zeros((D_FFT, 2 * D_FFT), np.float32)
    for g in range(N_FFT_HEADS):
        sl = slice(g * D_FFT_HEAD, (g + 1) * D_FFT_HEAD)
        out[sl, sl] = cc
        out[sl, D_FFT + g * D_FFT_HEAD:D_FFT + (g + 1) * D_FFT_HEAD] = sc
    return jnp.asarray(out, BF16)


def _block_diag(w):
    g, c, e = w.shape
    out = jnp.zeros((g * c, g * e), w.dtype)
    for i in range(g):
        out = lax.dynamic_update_slice(out, w[i], (i * c, i * e))
    return out


POOL_PAD = 16


def _pool_kernel(u_ref, w_ref, sc_ref, o_ref, s0, s1, s2, s3, s4, *, seq_len):
    lp = o_ref.shape[1]
    r_tot = lp + 2 * POOL_PAD
    lo, hi = 8, r_tot - 8
    row = lax.broadcasted_iota(jnp.int32, (lp, 1), 0)
    live = row < seq_len
    x = jnp.where(live, u_ref[0].astype(F32), 0.0)

    zeros_pad = jnp.zeros((POOL_PAD, D_POOL), F32)
    s0[0:POOL_PAD, :] = zeros_pad
    s0[POOL_PAD:POOL_PAD + lp, :] = x
    s0[POOL_PAD + lp:r_tot, :] = zeros_pad
    for s in (s1, s2, s3, s4):
        s[0:POOL_PAD, :] = zeros_pad
        s[POOL_PAD + lp:r_tot, :] = zeros_pad
    s1[lo:hi, :] = s0[lo - 1:hi - 1, :] + s0[lo:hi, :]
    s2[lo:hi, :] = s1[lo - 1:hi - 1, :] + s1[lo + 1:hi + 1, :]
    s3[lo:hi, :] = s2[lo - 2:hi - 2, :] + s2[lo + 2:hi + 2, :]
    s4[lo:hi, :] = s3[lo - 4:hi - 4, :] + s3[lo + 4:hi + 4, :]

    lane = lax.broadcasted_iota(jnp.int32, (1, D_POOL), 1)
    group = jnp.right_shift(lane, 6)
    half = jnp.where(group == 0, 1, jnp.where(group == 1, 2, jnp.where(group == 2, 4, 8)))
    cnt = jnp.minimum(row + half, seq_len) - jnp.maximum(row - half, 0)
    cnt = jnp.maximum(cnt, 1).astype(F32)
    sl = slice(POOL_PAD, POOL_PAD + lp)
    wsum = jnp.where(group == 0, s1[sl, :],
                     jnp.where(group == 1, s2[sl, :], jnp.where(group == 2, s3[sl, :], s4[sl, :])))
    p = jnp.where(live, wsum / cnt - x, 0.0)
    y = jnp.dot(p.astype(BF16), w_ref[...], preferred_element_type=F32) * sc_ref[...]
    o_ref[0] = y.astype(o_ref.dtype)


def _pool(z3, w_bd, scale, *, seq_len):
    b, lp, d_in = z3.shape
    col_block = (d_in - D_POOL) // D_POOL
    scratch = [pltpu.VMEM((lp + 2 * POOL_PAD, D_POOL), F32) for _ in range(5)]
    return pl.pallas_call(
        functools.partial(_pool_kernel, seq_len=seq_len),
        out_shape=jax.ShapeDtypeStruct((b, lp, D_POOL), BF16),
        grid=(b,),
        in_specs=[pl.BlockSpec((1, lp, D_POOL), lambda bi: (bi, 0, col_block)),
                  pl.BlockSpec((D_POOL, D_POOL), lambda bi: (0, 0)),
                  pl.BlockSpec((1, D_POOL), lambda bi: (0, 0))],
        out_specs=pl.BlockSpec((1, lp, D_POOL), lambda bi: (bi, 0, 0)),
        scratch_shapes=scratch,
        compiler_params=_cparams(1),
        name="pool_mix",
    )(z3, w_bd, scale.reshape(1, D_POOL))


def _mix_project(a_ref, f_ref, p_ref, g_ref, w_ref):
    an = _rms(a_ref[...].astype(F32), g_ref[:, 0:D_ATT]).astype(BF16)
    fn = _rms(f_ref[...].astype(F32), g_ref[:, D_ATT:D_ATT + D_FFT]).astype(BF16)
    pn = _rms(p_ref[...].astype(F32), g_ref[:, D_ATT + D_FFT:]).astype(BF16)
    acc = jnp.dot(an, w_ref[0:D_ATT, :], preferred_element_type=F32)
    acc += jnp.dot(fn, w_ref[D_ATT:D_ATT + D_FFT, :], preferred_element_type=F32)
    acc += jnp.dot(pn, w_ref[D_ATT + D_FFT:, :], preferred_element_type=F32)
    return acc


def _outproj_kernel(a_ref, f_ref, p_ref, h_ref, g_ref, w_ref, o_ref):
    o_ref[...] = h_ref[...] + _mix_project(a_ref, f_ref, p_ref, g_ref, w_ref)


def _outproj(a, f, p, h, g, w, *, tm=1024):
    n, d = h.shape
    row = lambda width: pl.BlockSpec((tm, width), lambda i: (i, 0))
    return pl.pallas_call(
        _outproj_kernel,
        out_shape=jax.ShapeDtypeStruct((n, d), F32),
        grid=(n // tm,),
        in_specs=[row(D_ATT), row(D_FFT), row(D_POOL), row(d),
                  pl.BlockSpec((1, d), lambda i: (0, 0)),
                  pl.BlockSpec((d, d), lambda i: (0, 0))],
        out_specs=row(d),
        compiler_params=_cparams(1),
        name="out_proj",
    )(a, f, p, h, g.reshape(1, d), w)


def _ffn_kernel(h_ref, g_ref, wg_ref, wu_ref, wd_ref, o_ref, xn_ref, acc_ref):
    c = pl.program_id(1)

    @pl.when(c == 0)
    def _():
        xn_ref[...] = _rms(h_ref[...], g_ref[...]).astype(BF16)

    xn = xn_ref[...]
    gate = jnp.dot(xn, wg_ref[...], preferred_element_type=F32)
    up = jnp.dot(xn, wu_ref[...], preferred_element_type=F32)
    hh = (gate * jax.nn.sigmoid(gate) * up).astype(BF16)
    part = jnp.dot(hh, wd_ref[...], preferred_element_type=F32)

    @pl.when(c == 0)
    def _():
        acc_ref[...] = part

    @pl.when(c > 0)
    def _():
        acc_ref[...] += part

    @pl.when(c == pl.num_programs(1) - 1)
    def _():
        o_ref[...] = h_ref[...] + acc_ref[...]


def _ffn(h, g, wg, wu, wd, *, tm=1024, n_chunks=2):
    n, d = h.shape
    d_ff = wg.shape[1]
    fc = d_ff // n_chunks
    return pl.pallas_call(
        _ffn_kernel,
        out_shape=jax.ShapeDtypeStruct((n, d), F32),
        grid=(n // tm, n_chunks),
        in_specs=[pl.BlockSpec((tm, d), lambda i, c: (i, 0)),
                  pl.BlockSpec((1, d), lambda i, c: (0, 0)),
                  pl.BlockSpec((d, fc), lambda i, c: (0, c)),
                  pl.BlockSpec((d, fc), lambda i, c: (0, c)),
                  pl.BlockSpec((fc, d), lambda i, c: (c, 0))],
        out_specs=pl.BlockSpec((tm, d), lambda i, c: (i, 0)),
        scratch_shapes=[pltpu.VMEM((tm, d), BF16), pltpu.VMEM((tm, d), F32)],
        compiler_params=_cparams(2),
        name="swiglu_ffn",
    )(h, g.reshape(1, d), wg, wu, wd)


MOE_TILE = 1024
EXPERT_ROWS = 512
R_GATE, R_EXPERT, R_RANK = 0, 2, 4


def _router_kernel(a_ref, f_ref, p_ref, h_ref, gm_ref, wo_ref, g_ref, wr_ref, tri_ref,
                   hn_ref, xn_ref, rec_ref, rect_ref, cnt_ref, *, seq_len, lp):
    t = h_ref.shape[0]
    hn = h_ref[...] + _mix_project(a_ref, f_ref, p_ref, gm_ref, wo_ref)
    hn_ref[...] = hn
    y = _rms(hn, g_ref[...])
    xn_ref[...] = y
    logits = jnp.dot(y, wr_ref[...], preferred_element_type=F32, precision=lax.Precision.HIGHEST)
    lane = lax.broadcasted_iota(jnp.int32, (t, LANE), 1)
    lg = jnp.where(lane < N_EXPERTS, logits, -jnp.inf)
    m1 = jnp.max(lg, axis=-1, keepdims=True)
    i1 = jnp.min(jnp.where(lg == m1, lane, LANE), axis=-1, keepdims=True)
    lg2 = jnp.where(lane == i1, -jnp.inf, lg)
    m2 = jnp.max(lg2, axis=-1, keepdims=True)
    i2 = jnp.min(jnp.where(lg2 == m2, lane, LANE), axis=-1, keepdims=True)
    e2 = jnp.exp(m2 - m1)
    g1 = 1.0 / (1.0 + e2)
    g2 = e2 / (1.0 + e2)
    rowf = (pl.program_id(0) * t + lax.broadcasted_iota(jnp.int32, (t, 1), 0)).astype(F32)
    seq = jnp.floor((rowf + 0.5) * (1.0 / lp))
    live = (rowf - seq * lp) < seq_len
    first = lane == i1
    second = lane == i2
    member = jnp.where((first | second) & live, 1.0, 0.0)
    rank = jnp.dot(tri_ref[...], member.astype(BF16), preferred_element_type=F32)
    r1 = jnp.sum(jnp.where(first, rank, 0.0), axis=-1, keepdims=True)
    r2 = jnp.sum(jnp.where(second, rank, 0.0), axis=-1, keepdims=True)
    dead = jnp.logical_not(live)
    rec = jnp.zeros((t, LANE), F32)
    for ln, val in ((R_GATE, jnp.where(dead, 0.0, g1)), (R_GATE + 1, jnp.where(dead, 0.0, g2)),
                    (R_EXPERT, jnp.where(dead, -1.0, i1.astype(F32))),
                    (R_EXPERT + 1, jnp.where(dead, -1.0, i2.astype(F32))),
                    (R_RANK, r1), (R_RANK + 1, r2)):
        rec = jnp.where(lane == ln, val, rec)
    rec_ref[...] = rec
    rect_ref[0] = jnp.transpose(rec)[0:8, :]
    cnt_ref[0] = jnp.broadcast_to(jnp.sum(member, axis=0, keepdims=True), (8, LANE))


def _router(a, f, p, h, g_mix, w_out, g, w_router_pad, tri, *, seq_len, lp):
    n, d = h.shape
    t = MOE_TILE
    row = lambda width: pl.BlockSpec((t, width), lambda i: (i, 0))
    const = lambda r, c: pl.BlockSpec((r, c), lambda i: (0, 0))
    return pl.pallas_call(
        functools.partial(_router_kernel, seq_len=seq_len, lp=lp),
        out_shape=(jax.ShapeDtypeStruct((n, d), F32),
                   jax.ShapeDtypeStruct((n, d), F32),
                   jax.ShapeDtypeStruct((n, LANE), F32),
                   jax.ShapeDtypeStruct((n // t, 8, t), F32),
                   jax.ShapeDtypeStruct((n // t, 8, LANE), F32)),
        grid=(n // t,),
        in_specs=[row(D_ATT), row(D_FFT), row(D_POOL), row(d), const(1, d), const(d, d),
                  const(1, d), const(d, LANE), const(t, t)],
        out_specs=(row(d), row(d), row(LANE),
                   pl.BlockSpec((1, 8, t), lambda i: (i, 0, 0)),
                   pl.BlockSpec((1, 8, LANE), lambda i: (i, 0, 0))),
        compiler_params=_cparams(1),
        name="moe_router",
    )(a, f, p, h, g_mix.reshape(1, d), w_out, g.reshape(1, d), w_router_pad, tri)


def _dispatch_plan(rect, counts, pad_token, *, n_row_tiles):
    n_tiles = counts.shape[0]
    n = n_tiles * MOE_TILE
    cnt = counts[:, 0, :N_EXPERTS]
    before = jnp.cumsum(cnt, axis=0) - cnt
    total = jnp.sum(cnt, axis=0)
    padded = jnp.ceil(total / EXPERT_ROWS) * EXPERT_ROWS
    ends = jnp.cumsum(padded)
    start = ends - padded
    base = (start[None, :] + before)[:, None, :]
    pad_rank = np.cumsum(np.asarray(pad_token, np.int64)) - 1
    spill = n_row_tiles * EXPERT_ROWS + 2 * pad_rank.reshape(n_tiles, MOE_TILE)
    slots_out, slots_back = [], []
    for k in range(2):
        e = rect[:, R_EXPERT + k, :]
        r = rect[:, R_RANK + k, :]
        hit = e[..., None] == jnp.arange(N_EXPERTS, dtype=F32)[None, None, :]
        slot = (jnp.sum(jnp.where(hit, base, 0.0), axis=-1) + r).astype(jnp.int32)
        dead = e < 0
        slots_out.append(jnp.where(dead, jnp.asarray(spill + k, jnp.int32), slot))
        slots_back.append(jnp.where(dead, 0, slot))
    pos_out = jnp.stack(slots_out, axis=-1).reshape(n * 2)
    pos_back = jnp.stack(slots_back, axis=-1).reshape(n * 2)
    tile_row0 = jnp.arange(n_row_tiles, dtype=F32) * EXPERT_ROWS
    tile_expert = jnp.sum(tile_row0[:, None] >= ends[None, :], axis=-1)
    tile_live = (tile_row0 < ends[-1]).astype(jnp.int32)
    tile_expert = jnp.minimum(tile_expert, N_EXPERTS - 1).astype(jnp.int32)
    return pos_out, pos_back, tile_expert, tile_live


ROW_GROUP = 8


def _row_copy(src_ref, src_row, dst_ref, dst_row, sem):
    return pltpu.make_async_copy(src_ref.at[pl.ds(src_row, 1)], dst_ref.at[pl.ds(dst_row, 1)], sem)


def _dispatch_kernel(pos_ref, x_ref, zero_ref, xs_ref, sem):
    del zero_ref
    t = x_ref.shape[0]
    base = pl.program_id(0) * t * 2

    def issue(g, carry):
        row0 = pl.multiple_of(g * ROW_GROUP, ROW_GROUP)
        group = x_ref.at[pl.ds(row0, ROW_GROUP)]
        idx0 = base + 2 * row0
        for c in range(ROW_GROUP):
            for k in range(2):
                p = pos_ref[idx0 + (2 * c + k)]
                _row_copy(group, c, xs_ref, p, sem).start(priority=k)
        return carry

    def drain(g, carry):
        for _ in range(2 * ROW_GROUP):
            _row_copy(x_ref, 0, xs_ref, 0, sem).wait()
        return carry

    lax.fori_loop(0, t // ROW_GROUP, issue, 0)
    lax.fori_loop(0, t // ROW_GROUP, drain, 0)


def _dispatch(pos, xn, n_rows):
    n, d = xn.shape
    t = MOE_TILE
    grid_spec = pltpu.PrefetchScalarGridSpec(
        num_scalar_prefetch=1,
        grid=(n // t,),
        in_specs=[pl.BlockSpec((t, d), lambda i, pos: (i, 0)),
                  pl.BlockSpec(memory_space=pl.ANY)],
        out_specs=pl.BlockSpec(memory_space=pl.ANY),
        scratch_shapes=[pltpu.SemaphoreType.DMA],
    )
    return pl.pallas_call(
        _dispatch_kernel,
        out_shape=jax.ShapeDtypeStruct((n_rows, d), F32),
        grid_spec=grid_spec,
        input_output_aliases={2: 0},
        compiler_params=_cparams(1),
        name="moe_dispatch",
    )(pos, xn, jnp.zeros((n_rows, d), F32))


def _expert_kernel(te_ref, live_ref, x_ref, wg_ref, wu_ref, wd_ref, o_ref, xb_ref, acc_ref):
    del te_ref
    i, c = pl.program_id(0), pl.program_id(1)
    last = pl.num_programs(1) - 1
    live = live_ref[i] > 0

    @pl.when(live)
    def _():
        @pl.when(c == 0)
        def _():
            xb_ref[...] = x_ref[...].astype(BF16)

        xb = xb_ref[...]
        gate = jnp.dot(xb, wg_ref[0, 0], preferred_element_type=F32)
        up = jnp.dot(xb, wu_ref[0, 0], preferred_element_type=F32)
        hh = (gate * jax.nn.sigmoid(gate) * up).astype(BF16)
        part = jnp.dot(hh, wd_ref[0, 0], preferred_element_type=F32)

        @pl.when(c == 0)
        def _():
            acc_ref[...] = part

        @pl.when(c > 0)
        def _():
            acc_ref[...] += part

        @pl.when(c == last)
        def _():
            o_ref[...] = acc_ref[...]

    @pl.when(jnp.logical_not(live) & (c == last))
    def _():
        o_ref[...] = jnp.zeros(o_ref.shape, o_ref.dtype)


def _experts(tile_expert, tile_live, xs, n_row_tiles, layer, wg, wu, wd, *, n_chunks=2):
    d = xs.shape[1]
    tm = EXPERT_ROWS
    n_rows = n_row_tiles * tm
    fc = wg.shape[3] // n_chunks
    chunk = lambda i, c, te, lv: jnp.where(lv[i] > 0, c, n_chunks - 1)
    grid_spec = pltpu.PrefetchScalarGridSpec(
        num_scalar_prefetch=2,
        grid=(n_rows // tm, n_chunks),
        in_specs=[pl.BlockSpec((tm, d), lambda i, c, te, lv: (i, 0)),
                  pl.BlockSpec((1, 1, d, fc), lambda i, c, te, lv: (layer, te[i], 0, chunk(i, c, te, lv))),
                  pl.BlockSpec((1, 1, d, fc), lambda i, c, te, lv: (layer, te[i], 0, chunk(i, c, te, lv))),
                  pl.BlockSpec((1, 1, fc, d), lambda i, c, te, lv: (layer, te[i], chunk(i, c, te, lv), 0))],
        out_specs=pl.BlockSpec((tm, d), lambda i, c, te, lv: (i, 0)),
        scratch_shapes=[pltpu.VMEM((tm, d), BF16), pltpu.VMEM((tm, d), F32)],
    )
    return pl.pallas_call(
        _expert_kernel,
        out_shape=jax.ShapeDtypeStruct((n_rows, d), F32),
        grid_spec=grid_spec,
        compiler_params=_cparams(2),
        name="moe_experts",
    )(tile_expert, tile_live, xs, wg, wu, wd)


def _combine_kernel(pos_ref, h_ref, rec_ref, ys_ref, o_ref, buf0, buf1, sem):
    t = h_ref.shape[0]
    base = pl.program_id(0) * t * 2
    bufs = (buf0, buf1)

    def issue(g, carry):
        row0 = pl.multiple_of(g * ROW_GROUP, ROW_GROUP)
        groups = [buf.at[pl.ds(row0, ROW_GROUP)] for buf in bufs]
        idx0 = base + 2 * row0
        for c in range(ROW_GROUP):
            for k in range(2):
                p = pos_ref[idx0 + (2 * c + k)]
                _row_copy(ys_ref, p, groups[k], c, sem).start(priority=k)
        return carry

    def drain(g, carry):
        for _ in range(2 * ROW_GROUP):
            _row_copy(ys_ref, 0, buf0, 0, sem).wait()
        return carry

    lax.fori_loop(0, t // ROW_GROUP, issue, 0)
    lax.fori_loop(0, t // ROW_GROUP, drain, 0)
    rec = rec_ref[...]
    g1 = rec[:, R_GATE:R_GATE + 1]
    g2 = rec[:, R_GATE + 1:R_GATE + 2]
    o_ref[...] = h_ref[...] + g1 * buf0[...] + g2 * buf1[...]


def _combine(pos, h, rec, ys):
    n, d = h.shape
    t = MOE_TILE
    grid_spec = pltpu.PrefetchScalarGridSpec(
        num_scalar_prefetch=1,
        grid=(n // t,),
        in_specs=[pl.BlockSpec((t, d), lambda i, pos: (i, 0)),
                  pl.BlockSpec((t, LANE), lambda i, pos: (i, 0)),
                  pl.BlockSpec(memory_space=pl.ANY)],
        out_specs=pl.BlockSpec((t, d), lambda i, pos: (i, 0)),
        scratch_shapes=[pltpu.VMEM((t, d), F32), pltpu.VMEM((t, d), F32), pltpu.SemaphoreType.DMA],
    )
    return pl.pallas_call(
        _combine_kernel,
        out_shape=jax.ShapeDtypeStruct((n, d), F32),
        grid_spec=grid_spec,
        compiler_params=_cparams(1),
        name="moe_combine",
    )(pos, h, rec, ys)


def _norm_kernel(x_ref, tail_ref, g_ref, o_ref):
    tm = x_ref.shape[1]
    o_ref[0, 0:tm - N_META, :] = _rms(x_ref[0, N_META:tm, :], g_ref[...])
    o_ref[0, tm - N_META:tm, :] = _rms(tail_ref[0], g_ref[...])


def _final_norm(h3, g, *, t_len, tm=1024):
    b, lp, d = h3.shape
    assert t_len % tm == 0 and tm % N_META == 0
    return pl.pallas_call(
        _norm_kernel,
        out_shape=jax.ShapeDtypeStruct((b, t_len, d), F32),
        grid=(b, t_len // tm),
        in_specs=[pl.BlockSpec((1, tm, d), lambda bi, j: (bi, j, 0)),
                  pl.BlockSpec((1, N_META, d), lambda bi, j: (bi, (j + 1) * (tm // N_META), 0)),
                  pl.BlockSpec((1, d), lambda bi, j: (0, 0))],
        out_specs=pl.BlockSpec((1, tm, d), lambda bi, j: (bi, j, 0)),
        compiler_params=_cparams(2),
        name="final_norm",
    )(h3, h3, g.reshape(1, d))


def _fft_row_tile(lp):
    best = 16
    for tk in range(16, lp + 1, 16):
        if lp % tk == 0 and tk * lp * 2 <= 4608 * 1024:
            best = tk
    return best


def _trunk(x, meta_tokens, prm):
    b, t_len, d = x.shape
    depth = len(prm["w_in"])
    seq_len = N_META + t_len
    lp = -(-seq_len // LANE) * LANE
    n = b * lp
    rows = t_len // GRID_W
    assert t_len % GRID_W == 0 and rows >= NA_ROWS and n % MOE_TILE == 0

    meta = jnp.broadcast_to(meta_tokens.astype(x.dtype)[None], (b, N_META, d))
    h = jnp.concatenate([meta, x, jnp.zeros((b, lp - seq_len, d), x.dtype)], axis=1).reshape(n, d)

    cmat, smat = _twiddles(seq_len, lp)
    tk = _fft_row_tile(lp)
    tri = jnp.tril(jnp.ones((MOE_TILE, MOE_TILE), BF16), -1)

    for i in range(depth):
        z = _norm_matmul(h, prm["norm_mix"][i], prm["w_in"][i])
        z3 = z.reshape(b, lp, D_IN)
        ab = _matmul_cols(z, prm["chan_dft"], (3 * D_ATT) // D_FFT)
        a = _attention(z3, prm["attn_bias"][i], prm["attn_mbias"][i], rows=rows, seq_len=seq_len)
        f = _fourier(cmat, smat, ab.reshape(b, lp, 2 * D_FFT), prm["w_fft"][i], tk=tk)
        p = _pool(z3, prm["w_pool"][i], prm["pool_scale"][i], seq_len=seq_len)
        mixed = (a.reshape(n, D_ATT), f.reshape(n, D_FFT), p.reshape(n, D_POOL), h,
                 prm["norm_groups"][i], prm["w_out"][i])
        j = i // 2
        if i % 2 == 0:
            h = _outproj(*mixed)
            h = _ffn(h, prm["norm_ffn"][i], prm["w_ff_gate"][j], prm["w_ff_up"][j], prm["w_ff_down"][j])
        else:
            h, xn, rec, rect, counts = _router(*mixed, prm["norm_ffn"][i], prm["w_router"][j], tri,
                                               seq_len=seq_len, lp=lp)
            n_row_tiles = -(-2 * b * seq_len // EXPERT_ROWS) + N_EXPERTS
            pad_token = (np.arange(n) % lp) >= seq_len
            pos_out, pos_back, tile_expert, tile_live = _dispatch_plan(rect, counts, pad_token,
                                                                       n_row_tiles=n_row_tiles)
            n_spill = -(-2 * int(pad_token.sum()) // ROW_GROUP) * ROW_GROUP
            xs = _dispatch(pos_out, xn, n_row_tiles * EXPERT_ROWS + n_spill)
            ys = _experts(tile_expert, tile_live, xs, n_row_tiles, j,
                          prm["w_exp_gate"], prm["w_exp_up"], prm["w_exp_down"])
            h = _combine(pos_back, h, rec, ys)
    return _final_norm(h.reshape(b, lp, d), prm["norm_final"], t_len=t_len)


def kernel(x_prompt, x_sample, meta_tokens, norm_mix, w_in, w_fft, w_pool, pool_scale, rel_bias, meta_bias,
           norm_groups, w_out, norm_ffn, w_ff_gate, w_ff_up, w_ff_down, w_router, w_exp_gate, w_exp_up,
           w_exp_down, norm_final):
    depth = w_in.shape[0]
    tabs = [_attn_bias_tables(rel_bias[i], meta_bias[i]) for i in range(depth)]
    key_scale = np.ones((D_IN,), np.float32)
    key_scale[D_ATT:2 * D_ATT] = LOG2E
    per_layer = lambda w: [w[i].astype(BF16) for i in range(w.shape[0])]
    prm = {
        "norm_mix": norm_mix, "norm_groups": norm_groups, "norm_ffn": norm_ffn, "norm_final": norm_final,
        "pool_scale": pool_scale,
        "w_in": per_layer(w_in * key_scale), "w_out": per_layer(w_out),
        "w_ff_gate": per_layer(w_ff_gate), "w_ff_up": per_layer(w_ff_up), "w_ff_down": per_layer(w_ff_down),
        "w_exp_gate": w_exp_gate.astype(BF16), "w_exp_up": w_exp_up.astype(BF16),
        "w_exp_down": w_exp_down.astype(BF16),
        "w_router": jnp.pad(w_router, ((0, 0), (0, 0), (0, LANE - N_EXPERTS))),
        "w_fft": jnp.stack([_block_diag(w_fft[i]) for i in range(depth)]).astype(BF16),
        "w_pool": jnp.stack([_block_diag(w_pool[i]) for i in range(depth)]).astype(BF16),
        "chan_dft": _channel_dft_matrix(),
        "attn_bias": [tb[0] for tb in tabs], "attn_mbias": [tb[1] for tb in tabs],
    }
    y_prompt = _trunk(x_prompt, meta_tokens, prm)
    y_sample = _trunk(x_sample, meta_tokens, prm)
    return (y_prompt, y_sample)
```

```python
import functools
import math

import numpy as np
import jax
import jax.numpy as jnp
from jax import lax
from jax.experimental import pallas as pl
from jax.experimental.pallas import tpu as pltpu

D_MODEL = 1024
N_META = 16
GRID_W = 64
D_HEAD = 64
D_ATT = 512
N_ATT_HEADS = 8
D_FFT = 256
N_FFT_HEADS = 4
D_FFT_HEAD = 64
D_POOL = 256
POOL_WINDOWS = (2, 4, 8, 16)
D_POOL_GROUP = 64
D_IN = 2048
NA_ROWS = 8
NA_COLS = 16
N_EXPERTS = 8
EPS = 1e-6

LANE = 128
NEG_BIG = -1e30
LOG2E = math.log2(math.e)
VMEM_LIMIT = 56 * 1024 * 1024

F32 = jnp.float32
BF16 = jnp.bfloat16


def _cparams(n_axes, vmem=VMEM_LIMIT):
    return pltpu.CompilerParams(dimension_semantics=("arbitrary",) * n_axes,
                                vmem_limit_bytes=vmem)


def _rms(x, g):
    return x * lax.rsqrt(jnp.mean(x * x, axis=-1, keepdims=True) + EPS) * g


def _norm_matmul_kernel(x_ref, g_ref, w_ref, o_ref, xn_ref, *, tn):
    xn_ref[...] = _rms(x_ref[...], g_ref[...]).astype(BF16)
    for n0 in range(0, o_ref.shape[1], tn):
        o_ref[:, n0:n0 + tn] = jnp.dot(xn_ref[...], w_ref[:, n0:n0 + tn],
                                       preferred_element_type=F32).astype(o_ref.dtype)


def _norm_matmul(x, g, w, *, tm=1024, tn=512):
    n, d = x.shape
    n_out = w.shape[1]
    return pl.pallas_call(
        functools.partial(_norm_matmul_kernel, tn=tn),
        out_shape=jax.ShapeDtypeStruct((n, n_out), BF16),
        grid=(n // tm,),
        in_specs=[pl.BlockSpec((tm, d), lambda i: (i, 0)),
                  pl.BlockSpec((1, d), lambda i: (0, 0)),
                  pl.BlockSpec((d, n_out), lambda i: (0, 0))],
        out_specs=pl.BlockSpec((tm, n_out), lambda i: (i, 0)),
        scratch_shapes=[pltpu.VMEM((tm, d), BF16)],
        compiler_params=_cparams(1),
        name="norm_matmul",
    )(x, g.reshape(1, d), w)


def _matmul_kernel(x_ref, w_ref, o_ref):
    o_ref[...] = jnp.dot(x_ref[...], w_ref[...], preferred_element_type=F32).astype(o_ref.dtype)


def _matmul_cols(x, w, col_block, *, tm=1024):
    n = x.shape[0]
    k, n_out = w.shape
    return pl.pallas_call(
        _matmul_kernel,
        out_shape=jax.ShapeDtypeStruct((n, n_out), BF16),
        grid=(n // tm,),
        in_specs=[pl.BlockSpec((tm, k), lambda i: (i, col_block)),
                  pl.BlockSpec((k, n_out), lambda i: (0, 0))],
        out_specs=pl.BlockSpec((tm, n_out), lambda i: (i, 0)),
        compiler_params=_cparams(1),
        name="channel_dft",
    )(x, w)


def _attn_kernel(q_ref, k_ref, v_ref, bias_ref, mb_ref, o_ref, smq_ref, s_a, s_b, s_c, s_d,
                 p_a, p_b, p_c, p_d, l_a, l_b, l_c, l_d, *, rows, seq_len):
    lp = o_ref.shape[1]
    n_win = NA_ROWS * GRID_W
    lane = lax.broadcasted_iota(jnp.int32, (1, LANE), 1)
    head0 = lane < D_HEAD
    nt = (((1,), (1,)), ((), ()))

    km16 = k_ref[0, 0:N_META, :]
    zero16 = jnp.zeros_like(km16)
    km = jnp.concatenate([jnp.where(head0, km16, zero16), jnp.where(head0, zero16, km16),
                          k_ref[0, 2 * N_META:LANE, :]], axis=0)
    vm = jnp.concatenate([v_ref[0, 0:N_META, :], v_ref[0, 0:N_META, :], v_ref[0, 2 * N_META:LANE, :]], axis=0)
    mb = mb_ref[0]

    def stack(q):
        q = q * jnp.asarray(D_HEAD ** -0.5, q.dtype)
        zero = jnp.zeros_like(q)
        return jnp.concatenate([jnp.where(head0, q, zero), jnp.where(head0, zero, q)], axis=0)

    def unstack(o, n):
        return jnp.where(head0, o[0:n], o[n:2 * n])

    q_all = q_ref[0] * jnp.asarray(D_HEAD ** -0.5, q_ref.dtype)
    smq_ref[...] = lax.dot_general(q_all, km, nt, preferred_element_type=F32)

    def meta_scores(q0, n, bias):
        blk = smq_ref[pl.ds(q0, n), :]
        return jnp.concatenate([blk, blk], axis=0) + bias

    sm = meta_scores(0, N_META, jnp.concatenate([mb[0:N_META], mb[GRID_W:GRID_W + N_META]], axis=0))
    pm = jnp.exp2(sm - jnp.max(sm, axis=-1, keepdims=True))
    om = jnp.dot(pm.astype(BF16), vm, preferred_element_type=F32)
    om = om / jnp.sum(pm, axis=-1, keepdims=True)
    o_ref[0, 0:N_META, :] = unstack(om, N_META).astype(o_ref.dtype)

    def window(t):
        t = jnp.minimum(t, rows - 1)
        rs = jnp.clip(t - NA_ROWS // 2, 0, rows - NA_ROWS)
        q0 = pl.multiple_of(N_META + t * GRID_W, 16)
        k0 = pl.multiple_of(N_META + rs * GRID_W, 16)
        return q0, k0, t - rs

    def scores(t, s_ref):
        q0, k0, off = window(t)
        qs = stack(q_ref[0, pl.ds(q0, GRID_W), :])
        kw = k_ref[0, pl.ds(k0, n_win), :]
        s_ref[...] = lax.dot_general(qs, kw, nt, preferred_element_type=F32) + bias_ref[0, off]

    dyn_zero = pl.multiple_of(jnp.minimum(pl.program_id(0), 0), LANE)

    def lane_tiles(x):
        return [x[:, i:i + LANE] for i in range(0, x.shape[1], LANE)]

    def softmax(t, s_ref, p_ref, l_ref):
        q0, _, _ = window(t)
        s = s_ref[pl.ds(dyn_zero, LANE), :]
        sm = meta_scores(q0, GRID_W, mb)
        m = jnp.max(functools.reduce(jnp.maximum, lane_tiles(s) + [sm]), axis=-1, keepdims=True)
        p = jnp.exp2(s - m)
        pm = jnp.exp2(sm - m)
        l_ref[...] = jnp.sum(functools.reduce(jnp.add, lane_tiles(p) + [pm]), axis=-1, keepdims=True)
        p_ref[:, 0:n_win] = p.astype(BF16)
        p_ref[:, n_win:] = pm.astype(BF16)

    def values(t, p_ref, l_ref):
        q0, k0, _ = window(t)
        vw = v_ref[0, pl.ds(k0, n_win), :]
        o = (jnp.dot(p_ref[:, 0:n_win], vw, preferred_element_type=F32)
             + jnp.dot(p_ref[:, n_win:], vm, preferred_element_type=F32))
        o = o / l_ref[...]
        o_ref[0, pl.ds(q0, GRID_W), :] = unstack(o, GRID_W).astype(o_ref.dtype)

    def step(t, s_in, s_out, pl_in, pl_out):
        scores(t + 4, s_out[0])
        scores(t + 5, s_out[1])
        values(t, *pl_in[0])
        values(t + 1, *pl_in[1])
        softmax(t + 2, s_in[0], *pl_out[0])
        softmax(t + 3, s_in[1], *pl_out[1])

    set_0 = ((p_a, l_a), (p_b, l_b))
    set_1 = ((p_c, l_c), (p_d, l_d))
    scores(0, s_c)
    scores(1, s_d)
    softmax(0, s_c, *set_0[0])
    softmax(1, s_d, *set_0[1])
    scores(2, s_a)
    scores(3, s_b)

    def quad_body(u, carry):
        t = 4 * u
        step(t, (s_a, s_b), (s_c, s_d), set_0, set_1)
        step(t + 2, (s_c, s_d), (s_a, s_b), set_1, set_0)
        return carry

    lax.fori_loop(0, rows // 4, quad_body, 0)
    if lp > seq_len:
        o_ref[0, seq_len:lp, :] = jnp.zeros((lp - seq_len, LANE), o_ref.dtype)


def _attention(z3, bias_tab, mb_tab, *, rows, seq_len):
    b, lp, _ = z3.shape
    assert rows % 4 == 0
    n_pairs = N_ATT_HEADS // 2
    n_keys = NA_ROWS * GRID_W
    blk = lambda off: pl.BlockSpec((1, lp, LANE), lambda hp, bi: (bi, 0, off + hp))
    scratch = ([pltpu.VMEM((lp, LANE), F32)]
               + [pltpu.VMEM((LANE, n_keys), F32) for _ in range(4)]
               + [pltpu.VMEM((LANE, n_keys + LANE), BF16) for _ in range(4)]
               + [pltpu.VMEM((LANE, 1), F32) for _ in range(4)])
    return pl.pallas_call(
        functools.partial(_attn_kernel, rows=rows, seq_len=seq_len),
        out_shape=jax.ShapeDtypeStruct((b, lp, D_ATT), BF16),
        grid=(n_pairs, b),
        in_specs=[blk(0), blk(n_pairs), blk(2 * n_pairs),
                  pl.BlockSpec((1, NA_ROWS, LANE, n_keys), lambda hp, bi: (hp, 0, 0, 0)),
                  pl.BlockSpec((1, LANE, LANE), lambda hp, bi: (hp, 0, 0))],
        out_specs=pl.BlockSpec((1, lp, LANE), lambda hp, bi: (bi, 0, hp)),
        scratch_shapes=scratch,
        compiler_params=_cparams(2),
        name="nbr_attention",
    )(z3, z3, z3, bias_tab, mb_tab)


def _attn_bias_tables(rel_bias, meta_bias):
    h, n_dr, n_dc = rel_bias.shape
    c = np.arange(GRID_W)[:, None]
    j = np.arange(GRID_W)[None, :]
    cs = np.clip(c - NA_COLS // 2, 0, GRID_W - NA_COLS)
    valid = (j >= cs) & (j < cs + NA_COLS)
    ext = jnp.zeros((h, n_dr, 2 * GRID_W), F32)
    ext = lax.dynamic_update_slice(ext, rel_bias.astype(F32), (0, 0, GRID_W - NA_COLS))
    skew = jnp.tile(ext, (1, 1, GRID_W))[:, :, :GRID_W * (2 * GRID_W - 1)]
    toep = skew.reshape(h, n_dr, GRID_W, 2 * GRID_W - 1)[..., GRID_W - 1:]
    toep = jnp.where(valid[None, None], toep * LOG2E, NEG_BIG)
    t = jnp.stack([toep[:, NA_ROWS - 1 - oi:2 * NA_ROWS - 1 - oi] for oi in range(NA_ROWS)], axis=1)
    t = t.transpose(0, 1, 3, 2, 4)
    t = t.reshape(h // 2, 2, NA_ROWS, GRID_W, NA_ROWS * GRID_W)
    t = t.transpose(0, 2, 1, 3, 4).reshape(h // 2, NA_ROWS, 2 * GRID_W, NA_ROWS * GRID_W)
    mbp = (meta_bias.astype(F32) * LOG2E).reshape(h // 2, 2, N_META)
    neg = jnp.full((h // 2, N_META), NEG_BIG, F32)
    mb0 = jnp.concatenate([mbp[:, 0], neg], axis=-1)
    mb1 = jnp.concatenate([neg, mbp[:, 1]], axis=-1)
    mb = jnp.stack([mb0, mb1], axis=1)
    mb = jnp.pad(mb, ((0, 0), (0, 0), (0, LANE - 2 * N_META)), constant_values=NEG_BIG)
    mb = jnp.broadcast_to(mb[:, :, None, :], (h // 2, 2, GRID_W, LANE))
    return t, mb.reshape(h // 2, 2 * GRID_W, LANE)


def _fft_kernel(c_ref, s_ref, ab_ref, w_ref, o_ref):
    a = ab_ref[0, :, 0:D_FFT]
    b = ab_ref[0, :, D_FFT:2 * D_FFT]
    f = (jnp.dot(c_ref[...], a, preferred_element_type=F32)
         + jnp.dot(s_ref[...], b, preferred_element_type=F32))
    o_ref[0] = jnp.dot(f.astype(BF16), w_ref[...], preferred_element_type=F32).astype(o_ref.dtype)


def _fourier(cmat, smat, ab3, w_bd, *, tk):
    b, lp, _ = ab3.shape
    return pl.pallas_call(
        _fft_kernel,
        out_shape=jax.ShapeDtypeStruct((b, lp, D_FFT), BF16),
        grid=(lp // tk, b),
        in_specs=[pl.BlockSpec((tk, lp), lambda j, bi: (j, 0)),
                  pl.BlockSpec((tk, lp), lambda j, bi: (j, 0)),
                  pl.BlockSpec((1, lp, 2 * D_FFT), lambda j, bi: (bi, 0, 0)),
                  pl.BlockSpec((D_FFT, D_FFT), lambda j, bi: (0, 0))],
        out_specs=pl.BlockSpec((1, tk, D_FFT), lambda j, bi: (bi, j, 0)),
        compiler_params=_cparams(2),
        name="fourier_mix",
    )(cmat, smat, ab3, w_bd)


def _twiddles(seq_len, lp):
    theta = 2.0 * math.pi / seq_len
    k = jnp.arange(lp, dtype=jnp.int32)[:, None]
    t1 = (GRID_W * jnp.arange(lp // GRID_W, dtype=jnp.int32))[None, :]
    t0 = jnp.arange(GRID_W, dtype=jnp.int32)[None, :]
    ang_a = ((k * t1) % seq_len).astype(F32) * theta
    ang_b = ((k * t0) % seq_len).astype(F32) * theta
    ca, sa = jnp.cos(ang_a)[:, :, None], jnp.sin(ang_a)[:, :, None]
    cb, sb = jnp.cos(ang_b)[:, None, :], jnp.sin(ang_b)[:, None, :]
    cmat = (ca * cb - sa * sb).reshape(lp, lp)
    smat = (sa * cb + ca * sb).reshape(lp, lp)
    idx = jnp.arange(lp)
    valid = (idx[:, None] < seq_len) & (idx[None, :] < seq_len)
    scale = seq_len ** -0.5
    cmat = jnp.where(valid, cmat * scale, 0.0).astype(BF16)
    smat = jnp.where(valid, -smat * scale, 0.0).astype(BF16)
    return cmat, smat


def _channel_dft_matrix():
    c = np.arange(D_FFT_HEAD)
    ang = 2.0 * np.pi * ((c[:, None] * c[None, :]) % D_FFT_HEAD) / D_FFT_HEAD
    cc = np.cos(ang) / math.sqrt(D_FFT_HEAD)
    sc = np.sin(ang) / math.sqrt(D_FFT_HEAD)
    out = np.zeros((D_FFT, 2 * D_FFT), np.float32)
    for g in range(N_FFT_HEADS):
        sl = slice(g * D_FFT_HEAD, (g + 1) * D_FFT_HEAD)
        out[sl, sl] = cc
        out[sl, D_FFT + g * D_FFT_HEAD:D_FFT + (g + 1) * D_FFT_HEAD] = sc
    return jnp.asarray(out, BF16)


def _block_diag(w):
    g, c, e = w.shape
    out = jnp.zeros((g * c, g * e), w.dtype)
    for i in range(g):
        out = lax.dynamic_update_slice(out, w[i], (i * c, i * e))
    return out


POOL_PAD = 16


def _pool_kernel(u_ref, w_ref, sc_ref, o_ref, s0, s1, s2, s3, s4, *, seq_len):
    lp = o_ref.shape[1]
    r_tot = lp + 2 * POOL_PAD
    lo, hi = 8, r_tot - 8
    row = lax.broadcasted_iota(jnp.int32, (lp, 1), 0)
    live = row < seq_len
    x = jnp.where(live, u_ref[0].astype(F32), 0.0)

    zeros_pad = jnp.zeros((POOL_PAD, D_POOL), F32)
    s0[0:POOL_PAD, :] = zeros_pad
    s0[POOL_PAD:POOL_PAD + lp, :] = x
    s0[POOL_PAD + lp:r_tot, :] = zeros_pad
    for s in (s1, s2, s3, s4):
        s[0:POOL_PAD, :] = zeros_pad
        s[POOL_PAD + lp:r_tot, :] = zeros_pad
    s1[lo:hi, :] = s0[lo - 1:hi - 1, :] + s0[lo:hi, :]
    s2[lo:hi, :] = s1[lo - 1:hi - 1, :] + s1[lo + 1:hi + 1, :]
    s3[lo:hi, :] = s2[lo - 2:hi - 2, :] + s2[lo + 2:hi + 2, :]
    s4[lo:hi, :] = s3[lo - 4:hi - 4, :] + s3[lo + 4:hi + 4, :]

    lane = lax.broadcasted_iota(jnp.int32, (1, D_POOL), 1)
    group = jnp.right_shift(lane, 6)
    half = jnp.where(group == 0, 1, jnp.where(group == 1, 2, jnp.where(group == 2, 4, 8)))
    cnt = jnp.minimum(row + half, seq_len) - jnp.maximum(row - half, 0)
    cnt = jnp.maximum(cnt, 1).astype(F32)
    sl = slice(POOL_PAD, POOL_PAD + lp)
    wsum = jnp.where(group == 0, s1[sl, :],
                     jnp.where(group == 1, s2[sl, :], jnp.where(group == 2, s3[sl, :], s4[sl, :])))
    p = jnp.where(live, wsum / cnt - x, 0.0)
    y = jnp.dot(p.astype(BF16), w_ref[...], preferred_element_type=F32) * sc_ref[...]
    o_ref[0] = y.astype(o_ref.dtype)


def _pool(z3, w_bd, scale, *, seq_len):
    b, lp, d_in = z3.shape
    col_block = (d_in - D_POOL) // D_POOL
    scratch = [pltpu.VMEM((lp + 2 * POOL_PAD, D_POOL), F32) for _ in range(5)]
    return pl.pallas_call(
        functools.partial(_pool_kernel, seq_len=seq_len),
        out_shape=jax.ShapeDtypeStruct((b, lp, D_POOL), BF16),
        grid=(b,),
        in_specs=[pl.BlockSpec((1, lp, D_POOL), lambda bi: (bi, 0, col_block)),
                  pl.BlockSpec((D_POOL, D_POOL), lambda bi: (0, 0)),
                  pl.BlockSpec((1, D_POOL), lambda bi: (0, 0))],
        out_specs=pl.BlockSpec((1, lp, D_POOL), lambda bi: (bi, 0, 0)),
        scratch_shapes=scratch,
        compiler_params=_cparams(1),
        name="pool_mix",
    )(z3, w_bd, scale.reshape(1, D_POOL))


def _mix_project(a_ref, f_ref, p_ref, g_ref, w_ref):
    an = _rms(a_ref[...].astype(F32), g_ref[:, 0:D_ATT]).astype(BF16)
    fn = _rms(f_ref[...].astype(F32), g_ref[:, D_ATT:D_ATT + D_FFT]).astype(BF16)
    pn = _rms(p_ref[...].astype(F32), g_ref[:, D_ATT + D_FFT:]).astype(BF16)
    acc = jnp.dot(an, w_ref[0:D_ATT, :], preferred_element_type=F32)
    acc += jnp.dot(fn, w_ref[D_ATT:D_ATT + D_FFT, :], preferred_element_type=F32)
    acc += jnp.dot(pn, w_ref[D_ATT + D_FFT:, :], preferred_element_type=F32)
    return acc


def _outproj_kernel(a_ref, f_ref, p_ref, h_ref, g_ref, w_ref, o_ref):
    o_ref[...] = h_ref[...] + _mix_project(a_ref, f_ref, p_ref, g_ref, w_ref)


def _outproj(a, f, p, h, g, w, *, tm=1024):
    n, d = h.shape
    row = lambda width: pl.BlockSpec((tm, width), lambda i: (i, 0))
    return pl.pallas_call(
        _outproj_kernel,
        out_shape=jax.ShapeDtypeStruct((n, d), F32),
        grid=(n // tm,),
        in_specs=[row(D_ATT), row(D_FFT), row(D_POOL), row(d),
                  pl.BlockSpec((1, d), lambda i: (0, 0)),
                  pl.BlockSpec((d, d), lambda i: (0, 0))],
        out_specs=row(d),
        compiler_params=_cparams(1),
        name="out_proj",
    )(a, f, p, h, g.reshape(1, d), w)


def _ffn_kernel(h_ref, g_ref, wg_ref, wu_ref, wd_ref, o_ref, xn_ref, acc_ref):
    c = pl.program_id(1)

    @pl.when(c == 0)
    def _():
        xn_ref[...] = _rms(h_ref[...], g_ref[...]).astype(BF16)

    xn = xn_ref[...]
    gate = jnp.dot(xn, wg_ref[...], preferred_element_type=F32)
    up = jnp.dot(xn, wu_ref[...], preferred_element_type=F32)
    hh = (gate * jax.nn.sigmoid(gate) * up).astype(BF16)
    part = jnp.dot(hh, wd_ref[...], preferred_element_type=F32)

    @pl.when(c == 0)
    def _():
        acc_ref[...] = part

    @pl.when(c > 0)
    def _():
        acc_ref[...] += part

    @pl.when(c == pl.num_programs(1) - 1)
    def _():
        o_ref[...] = h_ref[...] + acc_ref[...]


def _ffn(h, g, wg, wu, wd, *, tm=1024, n_chunks=2):
    n, d = h.shape
    d_ff = wg.shape[1]
    fc = d_ff // n_chunks
    return pl.pallas_call(
        _ffn_kernel,
        out_shape=jax.ShapeDtypeStruct((n, d), F32),
        grid=(n // tm, n_chunks),
        in_specs=[pl.BlockSpec((tm, d), lambda i, c: (i, 0)),
                  pl.BlockSpec((1, d), lambda i, c: (0, 0)),
                  pl.BlockSpec((d, fc), lambda i, c: (0, c)),
                  pl.BlockSpec((d, fc), lambda i, c: (0, c)),
                  pl.BlockSpec((fc, d), lambda i, c: (c, 0))],
        out_specs=pl.BlockSpec((tm, d), lambda i, c: (i, 0)),
        scratch_shapes=[pltpu.VMEM((tm, d), BF16), pltpu.VMEM((tm, d), F32)],
        compiler_params=_cparams(2),
        name="swiglu_ffn",
    )(h, g.reshape(1, d), wg, wu, wd)


MOE_TILE = 1024
EXPERT_ROWS = 512
R_GATE, R_EXPERT, R_RANK = 0, 2, 4


def _router_kernel(h_ref, g_ref, wr_ref, tri_ref, xn_ref, rec_ref, cnt_ref, *, seq_len, lp):
    t = h_ref.shape[0]
    y = _rms(h_ref[...], g_ref[...])
    xn_ref[...] = y
    logits = jnp.dot(y, wr_ref[...], preferred_element_type=F32, precision=lax.Precision.HIGHEST)
    lane = lax.broadcasted_iota(jnp.int32, (t, LANE), 1)
    lg = jnp.where(lane < N_EXPERTS, logits, -jnp.inf)
    m1 = jnp.max(lg, axis=-1, keepdims=True)
    i1 = jnp.min(jnp.where(lg == m1, lane, LANE), axis=-1, keepdims=True)
    lg2 = jnp.where(lane == i1, -jnp.inf, lg)
    m2 = jnp.max(lg2, axis=-1, keepdims=True)
    i2 = jnp.min(jnp.where(lg2 == m2, lane, LANE), axis=-1, keepdims=True)
    e2 = jnp.exp(m2 - m1)
    g1 = 1.0 / (1.0 + e2)
    g2 = e2 / (1.0 + e2)
    rowf = (pl.program_id(0) * t + lax.broadcasted_iota(jnp.int32, (t, 1), 0)).astype(F32)
    seq = jnp.floor((rowf + 0.5) * (1.0 / lp))
    live = (rowf - seq * lp) < seq_len
    first = lane == i1
    second = lane == i2
    member = jnp.where((first | second) & live, 1.0, 0.0)
    rank = jnp.dot(tri_ref[...], member.astype(BF16), preferred_element_type=F32)
    r1 = jnp.sum(jnp.where(first, rank, 0.0), axis=-1, keepdims=True)
    r2 = jnp.sum(jnp.where(second, rank, 0.0), axis=-1, keepdims=True)
    dead = jnp.logical_not(live)
    rec = jnp.zeros((t, LANE), F32)
    for ln, val in ((R_GATE, jnp.where(dead, 0.0, g1)), (R_GATE + 1, jnp.where(dead, 0.0, g2)),
                    (R_EXPERT, jnp.where(dead, -1.0, i1.astype(F32))),
                    (R_EXPERT + 1, jnp.where(dead, -1.0, i2.astype(F32))),
                    (R_RANK, r1), (R_RANK + 1, r2)):
        rec = jnp.where(lane == ln, val, rec)
    rec_ref[...] = rec
    cnt_ref[0] = jnp.broadcast_to(jnp.sum(member, axis=0, keepdims=True), (8, LANE))


def _router(h, g, w_router_pad, tri, *, seq_len, lp):
    n, d = h.shape
    t = MOE_TILE
    row = lambda width: pl.BlockSpec((t, width), lambda i: (i, 0))
    return pl.pallas_call(
        functools.partial(_router_kernel, seq_len=seq_len, lp=lp),
        out_shape=(jax.ShapeDtypeStruct((n, d), F32),
                   jax.ShapeDtypeStruct((n, LANE), F32),
                   jax.ShapeDtypeStruct((n // t, 8, LANE), F32)),
        grid=(n // t,),
        in_specs=[row(d),
                  pl.BlockSpec((1, d), lambda i: (0, 0)),
                  pl.BlockSpec((d, LANE), lambda i: (0, 0)),
                  pl.BlockSpec((t, t), lambda i: (0, 0))],
        out_specs=(row(d), row(LANE), pl.BlockSpec((1, 8, LANE), lambda i: (i, 0, 0))),
        compiler_params=_cparams(1),
        name="moe_router",
    )(h, g.reshape(1, d), w_router_pad, tri)


def _dispatch_plan(rec, counts, pad_token, *, n_row_tiles):
    n = rec.shape[0]
    n_tiles = counts.shape[0]
    cnt = counts[:, 0, :N_EXPERTS]
    before = jnp.cumsum(cnt, axis=0) - cnt
    total = jnp.sum(cnt, axis=0)
    padded = jnp.ceil(total / EXPERT_ROWS) * EXPERT_ROWS
    ends = jnp.cumsum(padded)
    start = ends - padded
    base = (start[None, :] + before)[:, None, :]
    pad_rank = np.cumsum(np.asarray(pad_token, np.int64)) - 1
    spill = n_row_tiles * EXPERT_ROWS + 2 * pad_rank.reshape(n_tiles, MOE_TILE)
    slots_out, slots_back = [], []
    for k in range(2):
        e = rec[:, R_EXPERT + k].reshape(n_tiles, MOE_TILE, 1)
        r = rec[:, R_RANK + k].reshape(n_tiles, MOE_TILE)
        hit = e == jnp.arange(N_EXPERTS, dtype=F32)[None, None, :]
        slot = (jnp.sum(jnp.where(hit, base, 0.0), axis=-1) + r).astype(jnp.int32)
        dead = e[..., 0] < 0
        slots_out.append(jnp.where(dead, jnp.asarray(spill + k, jnp.int32), slot))
        slots_back.append(jnp.where(dead, 0, slot))
    pos_out = jnp.stack(slots_out, axis=-1).reshape(n * 2)
    pos_back = jnp.stack(slots_back, axis=-1).reshape(n * 2)
    tile_row0 = jnp.arange(n_row_tiles, dtype=F32) * EXPERT_ROWS
    tile_expert = jnp.sum(tile_row0[:, None] >= ends[None, :], axis=-1)
    tile_live = (tile_row0 < ends[-1]).astype(jnp.int32)
    tile_expert = jnp.minimum(tile_expert, N_EXPERTS - 1).astype(jnp.int32)
    return pos_out, pos_back, tile_expert, tile_live


ROW_GROUP = 8


def _row_copy(src_ref, src_row, dst_ref, dst_row, sem):
    return pltpu.make_async_copy(src_ref.at[pl.ds(src_row, 1)], dst_ref.at[pl.ds(dst_row, 1)], sem)


def _dispatch_kernel(pos_ref, x_ref, zero_ref, xs_ref, sem):
    del zero_ref
    t = x_ref.shape[0]
    base = pl.program_id(0) * t * 2

    def issue(g, carry):
        row0 = pl.multiple_of(g * ROW_GROUP, ROW_GROUP)
        group = x_ref.at[pl.ds(row0, ROW_GROUP)]
        idx0 = base + 2 * row0
        for c in range(ROW_GROUP):
            for k in range(2):
                p = pos_ref[idx0 + (2 * c + k)]
                _row_copy(group, c, xs_ref, p, sem).start(priority=k)
        return carry

    def drain(g, carry):
        for _ in range(2 * ROW_GROUP):
            _row_copy(x_ref, 0, xs_ref, 0, sem).wait()
        return carry

    lax.fori_loop(0, t // ROW_GROUP, issue, 0)
    lax.fori_loop(0, t // ROW_GROUP, drain, 0)


def _dispatch(pos, xn, n_rows):
    n, d = xn.shape
    t = MOE_TILE
    grid_spec = pltpu.PrefetchScalarGridSpec(
        num_scalar_prefetch=1,
        grid=(n // t,),
        in_specs=[pl.BlockSpec((t, d), lambda i, pos: (i, 0)),
                  pl.BlockSpec(memory_space=pl.ANY)],
        out_specs=pl.BlockSpec(memory_space=pl.ANY),
        scratch_shapes=[pltpu.SemaphoreType.DMA],
    )
    return pl.pallas_call(
        _dispatch_kernel,
        out_shape=jax.ShapeDtypeStruct((n_rows, d), F32),
        grid_spec=grid_spec,
        input_output_aliases={2: 0},
        compiler_params=_cparams(1),
        name="moe_dispatch",
    )(pos, xn, jnp.zeros((n_rows, d), F32))


def _expert_kernel(te_ref, live_ref, x_ref, wg_ref, wu_ref, wd_ref, o_ref, xb_ref, acc_ref):
    del te_ref
    i, c = pl.program_id(0), pl.program_id(1)
    last = pl.num_programs(1) - 1
    live = live_ref[i] > 0

    @pl.when(live)
    def _():
        @pl.when(c == 0)
        def _():
            xb_ref[...] = x_ref[...].astype(BF16)

        xb = xb_ref[...]
        gate = jnp.dot(xb, wg_ref[0, 0], preferred_element_type=F32)
        up = jnp.dot(xb, wu_ref[0, 0], preferred_element_type=F32)
        hh = (gate * jax.nn.sigmoid(gate) * up).astype(BF16)
        part = jnp.dot(hh, wd_ref[0, 0], preferred_element_type=F32)

        @pl.when(c == 0)
        def _():
            acc_ref[...] = part

        @pl.when(c > 0)
        def _():
            acc_ref[...] += part

        @pl.when(c == last)
        def _():
            o_ref[...] = acc_ref[...]

    @pl.when(jnp.logical_not(live) & (c == last))
    def _():
        o_ref[...] = jnp.zeros(o_ref.shape, o_ref.dtype)


def _experts(tile_expert, tile_live, xs, n_row_tiles, layer, wg, wu, wd, *, n_chunks=2):
    d = xs.shape[1]
    tm = EXPERT_ROWS
    n_rows = n_row_tiles * tm
    fc = wg.shape[3] // n_chunks
    chunk = lambda i, c, te, lv: jnp.where(lv[i] > 0, c, n_chunks - 1)
    grid_spec = pltpu.PrefetchScalarGridSpec(
        num_scalar_prefetch=2,
        grid=(n_rows // tm, n_chunks),
        in_specs=[pl.BlockSpec((tm, d), lambda i, c, te, lv: (i, 0)),
                  pl.BlockSpec((1, 1, d, fc), lambda i, c, te, lv: (layer, te[i], 0, chunk(i, c, te, lv))),
                  pl.BlockSpec((1, 1, d, fc), lambda i, c, te, lv: (layer, te[i], 0, chunk(i, c, te, lv))),
                  pl.BlockSpec((1, 1, fc, d), lambda i, c, te, lv: (layer, te[i], chunk(i, c, te, lv), 0))],
        out_specs=pl.BlockSpec((tm, d), lambda i, c, te, lv: (i, 0)),
        scratch_shapes=[pltpu.VMEM((tm, d), BF16), pltpu.VMEM((tm, d), F32)],
    )
    return pl.pallas_call(
        _expert_kernel,
        out_shape=jax.ShapeDtypeStruct((n_rows, d), F32),
        grid_spec=grid_spec,
        compiler_params=_cparams(2),
        name="moe_experts",
    )(tile_expert, tile_live, xs, wg, wu, wd)


def _combine_kernel(pos_ref, h_ref, rec_ref, ys_ref, o_ref, a0, a1, b0, b1, sem_a, sem_b):
    t = h_ref.shape[0]
    step = pl.program_id(0)
    n_steps = pl.num_programs(0)

    def request(tile, bufs, sem):
        base = tile * t * 2

        def issue(g, carry):
            row0 = pl.multiple_of(g * ROW_GROUP, ROW_GROUP)
            groups = [buf.at[pl.ds(row0, ROW_GROUP)] for buf in bufs]
            idx0 = base + 2 * row0
            for c in range(ROW_GROUP):
                for k in range(2):
                    p = pos_ref[idx0 + (2 * c + k)]
                    _row_copy(ys_ref, p, groups[k], c, sem).start(priority=k)
            return carry

        lax.fori_loop(0, t // ROW_GROUP, issue, 0)

    def finish(bufs, sem):
        def drain(g, carry):
            for _ in range(2 * ROW_GROUP):
                _row_copy(ys_ref, 0, bufs[0], 0, sem).wait()
            return carry

        lax.fori_loop(0, t // ROW_GROUP, drain, 0)
        rec = rec_ref[...]
        g1 = rec[:, R_GATE:R_GATE + 1]
        g2 = rec[:, R_GATE + 1:R_GATE + 2]
        o_ref[...] = h_ref[...] + g1 * bufs[0][...] + g2 * bufs[1][...]

    set_a, set_b = ((a0, a1), sem_a), ((b0, b1), sem_b)

    @pl.when(step == 0)
    def _():
        request(0, *set_a)

    for parity, mine, other in ((0, set_a, set_b), (1, set_b, set_a)):
        @pl.when(step % 2 == parity)
        def _():
            @pl.when(step + 1 < n_steps)
            def _():
                request(step + 1, *other)

            finish(*mine)


def _combine(pos, h, rec, ys):
    n, d = h.shape
    t = MOE_TILE
    grid_spec = pltpu.PrefetchScalarGridSpec(
        num_scalar_prefetch=1,
        grid=(n // t,),
        in_specs=[pl.BlockSpec((t, d), lambda i, pos: (i, 0)),
                  pl.BlockSpec((t, LANE), lambda i, pos: (i, 0)),
                  pl.BlockSpec(memory_space=pl.ANY)],
        out_specs=pl.BlockSpec((t, d), lambda i, pos: (i, 0)),
        scratch_shapes=[pltpu.VMEM((t, d), F32) for _ in range(4)]
        + [pltpu.SemaphoreType.DMA, pltpu.SemaphoreType.DMA],
    )
    return pl.pallas_call(
        _combine_kernel,
        out_shape=jax.ShapeDtypeStruct((n, d), F32),
        grid_spec=grid_spec,
        compiler_params=_cparams(1),
        name="moe_combine",
    )(pos, h, rec, ys)


def _norm_kernel(x_ref, tail_ref, g_ref, o_ref):
    tm = x_ref.shape[1]
    o_ref[0, 0:tm - N_META, :] = _rms(x_ref[0, N_META:tm, :], g_ref[...])
    o_ref[0, tm - N_META:tm, :] = _rms(tail_ref[0], g_ref[...])


def _final_norm(h3, g, *, t_len, tm=1024):
    b, lp, d = h3.shape
    assert t_len % tm == 0 and tm % N_META == 0
    return pl.pallas_call(
        _norm_kernel,
        out_shape=jax.ShapeDtypeStruct((b, t_len, d), F32),
        grid=(b, t_len // tm),
        in_specs=[pl.BlockSpec((1, tm, d), lambda bi, j: (bi, j, 0)),
                  pl.BlockSpec((1, N_META, d), lambda bi, j: (bi, (j + 1) * (tm // N_META), 0)),
                  pl.BlockSpec((1, d), lambda bi, j: (0, 0))],
        out_specs=pl.BlockSpec((1, tm, d), lambda bi, j: (bi, j, 0)),
        compiler_params=_cparams(2),
        name="final_norm",
    )(h3, h3, g.reshape(1, d))


def _fft_row_tile(lp):
    best = 16
    for tk in range(16, lp + 1, 16):
        if lp % tk == 0 and tk * lp * 2 <= 4608 * 1024:
            best = tk
    return best


def _trunk(x, meta_tokens, prm):
    b, t_len, d = x.shape
    depth = len(prm["w_in"])
    seq_len = N_META + t_len
    lp = -(-seq_len // LANE) * LANE
    n = b * lp
    rows = t_len // GRID_W
    assert t_len % GRID_W == 0 and rows >= NA_ROWS and n % MOE_TILE == 0

    meta = jnp.broadcast_to(meta_tokens.astype(x.dtype)[None], (b, N_META, d))
    h = jnp.concatenate([meta, x, jnp.zeros((b, lp - seq_len, d), x.dtype)], axis=1).reshape(n, d)

    cmat, smat = _twiddles(seq_len, lp)
    tk = _fft_row_tile(lp)
    tri = jnp.tril(jnp.ones((MOE_TILE, MOE_TILE), BF16), -1)

    for i in range(depth):
        z = _norm_matmul(h, prm["norm_mix"][i], prm["w_in"][i])
        z3 = z.reshape(b, lp, D_IN)
        ab = _matmul_cols(z, prm["chan_dft"], (3 * D_ATT) // D_FFT)
        a = _attention(z3, prm["attn_bias"][i], prm["attn_mbias"][i], rows=rows, seq_len=seq_len)
        f = _fourier(cmat, smat, ab.reshape(b, lp, 2 * D_FFT), prm["w_fft"][i], tk=tk)
        p = _pool(z3, prm["w_pool"][i], prm["pool_scale"][i], seq_len=seq_len)
        h = _outproj(a.reshape(n, D_ATT), f.reshape(n, D_FFT), p.reshape(n, D_POOL), h,
                     prm["norm_groups"][i], prm["w_out"][i])
        j = i // 2
        if i % 2 == 0:
            h = _ffn(h, prm["norm_ffn"][i], prm["w_ff_gate"][j], prm["w_ff_up"][j], prm["w_ff_down"][j])
        else:
            xn, rec, counts = _router(h, prm["norm_ffn"][i], prm["w_router"][j], tri,
                                      seq_len=seq_len, lp=lp)
            n_row_tiles = -(-2 * b * seq_len // EXPERT_ROWS) + N_EXPERTS
            pad_token = (np.arange(n) % lp) >= seq_len
            pos_out, pos_back, tile_expert, tile_live = _dispatch_plan(rec, counts, pad_token,
                                                                       n_row_tiles=n_row_tiles)
            n_spill = -(-2 * int(pad_token.sum()) // ROW_GROUP) * ROW_GROUP
            xs = _dispatch(pos_out, xn, n_row_tiles * EXPERT_ROWS + n_spill)
            ys = _experts(tile_expert, tile_live, xs, n_row_tiles, j,
                          prm["w_exp_gate"], prm["w_exp_up"], prm["w_exp_down"])
            h = _combine(pos_back, h, rec, ys)
    return _final_norm(h.reshape(b, lp, d), prm["norm_final"], t_len=t_len)


def kernel(x_prompt, x_sample, meta_tokens, norm_mix, w_in, w_fft, w_pool, pool_scale, rel_bias, meta_bias,
           norm_groups, w_out, norm_ffn, w_ff_gate, w_ff_up, w_ff_down, w_router, w_exp_gate, w_exp_up,
           w_exp_down, norm_final):
    depth = w_in.shape[0]
    tabs = [_attn_bias_tables(rel_bias[i], meta_bias[i]) for i in range(depth)]
    key_scale = np.ones((D_IN,), np.float32)
    key_scale[D_ATT:2 * D_ATT] = LOG2E
    per_layer = lambda w: [w[i].astype(BF16) for i in range(w.shape[0])]
    prm = {
        "norm_mix": norm_mix, "norm_groups": norm_groups, "norm_ffn": norm_ffn, "norm_final": norm_final,
        "pool_scale": pool_scale,
        "w_in": per_layer(w_in * key_scale), "w_out": per_layer(w_out),
        "w_ff_gate": per_layer(w_ff_gate), "w_ff_up": per_layer(w_ff_up), "w_ff_down": per_layer(w_ff_down),
        "w_exp_gate": w_exp_gate.astype(BF16), "w_exp_up": w_exp_up.astype(BF16),
        "w_exp_down": w_exp_down.astype(BF16),
        "w_router": jnp.pad(w_router, ((0, 0), (0, 0), (0, LANE - N_EXPERTS))),
        "w_fft": jnp.stack([_block_diag(w_fft[i]) for i in range(depth)]).astype(BF16),
        "w_pool": jnp.stack([_block_diag(w_pool[i]) for i in range(depth)]).astype(BF16),
        "chan_dft": _channel_dft_matrix(),
        "attn_bias": [tb[0] for tb in tabs], "attn_mbias": [tb[1] for tb in tabs],
    }
    y_prompt = _trunk(x_prompt, meta_tokens, prm)
    y_sample = _trunk(x_sample, meta_tokens, prm)
    return (y_prompt, y_sample)
```

```python
import functools
import math

import numpy as np
import jax
import jax.numpy as jnp
from jax import lax
from jax.experimental import pallas as pl
from jax.experimental.pallas import tpu as pltpu

D_MODEL = 1024
N_META = 16
GRID_W = 64
D_HEAD = 64
D_ATT = 512
N_ATT_HEADS = 8
D_FFT = 256
N_FFT_HEADS = 4
D_FFT_HEAD = 64
D_POOL = 256
POOL_WINDOWS = (2, 4, 8, 16)
D_POOL_GROUP = 64
D_IN = 2048
NA_ROWS = 8
NA_COLS = 16
N_EXPERTS = 8
EPS = 1e-6

LANE = 128
NEG_BIG = -1e30
LOG2E = math.log2(math.e)
VMEM_LIMIT = 56 * 1024 * 1024

F32 = jnp.float32
BF16 = jnp.bfloat16


def _cparams(n_axes, vmem=VMEM_LIMIT):
    return pltpu.CompilerParams(dimension_semantics=("arbitrary",) * n_axes,
                                vmem_limit_bytes=vmem)


def _rms(x, g):
    return x * lax.rsqrt(jnp.mean(x * x, axis=-1, keepdims=True) + EPS) * g


def _norm_matmul_kernel(x_ref, g_ref, w_ref, cdft_ref, o_ref, ab_ref, xn_ref, *, tn, fft_col):
    xn_ref[...] = _rms(x_ref[...], g_ref[...]).astype(BF16)
    for n0 in range(0, o_ref.shape[1], tn):
        zc = jnp.dot(xn_ref[...], w_ref[:, n0:n0 + tn], preferred_element_type=F32).astype(o_ref.dtype)
        o_ref[:, n0:n0 + tn] = zc
        if n0 <= fft_col and fft_col + D_FFT <= n0 + tn:
            ab_ref[...] = jnp.dot(zc[:, fft_col - n0:fft_col - n0 + D_FFT], cdft_ref[...],
                                  preferred_element_type=F32).astype(ab_ref.dtype)


def _norm_matmul(x, g, w, cdft, *, tm=1024, tn=512):
    n, d = x.shape
    n_out = w.shape[1]
    fft_col = 3 * D_ATT
    assert fft_col % tn + D_FFT <= tn
    return pl.pallas_call(
        functools.partial(_norm_matmul_kernel, tn=tn, fft_col=fft_col),
        out_shape=(jax.ShapeDtypeStruct((n, n_out), BF16), jax.ShapeDtypeStruct((n, 2 * D_FFT), BF16)),
        grid=(n // tm,),
        in_specs=[pl.BlockSpec((tm, d), lambda i: (i, 0)),
                  pl.BlockSpec((1, d), lambda i: (0, 0)),
                  pl.BlockSpec((d, n_out), lambda i: (0, 0)),
                  pl.BlockSpec((D_FFT, 2 * D_FFT), lambda i: (0, 0))],
        out_specs=(pl.BlockSpec((tm, n_out), lambda i: (i, 0)),
                   pl.BlockSpec((tm, 2 * D_FFT), lambda i: (i, 0))),
        scratch_shapes=[pltpu.VMEM((tm, d), BF16)],
        compiler_params=_cparams(1),
        name="norm_matmul",
    )(x, g.reshape(1, d), w, cdft)


def _attn_kernel(q_ref, k_ref, v_ref, bias_ref, mb_ref, o_ref, smq_ref, s_a, s_b, s_c, s_d,
                 p_a, p_b, p_c, p_d, l_a, l_b, l_c, l_d, *, rows, seq_len):
    lp = o_ref.shape[1]
    n_win = NA_ROWS * GRID_W
    lane = lax.broadcasted_iota(jnp.int32, (1, LANE), 1)
    head0 = lane < D_HEAD
    nt = (((1,), (1,)), ((), ()))

    km16 = k_ref[0, 0:N_META, :]
    zero16 = jnp.zeros_like(km16)
    km = jnp.concatenate([jnp.where(head0, km16, zero16), jnp.where(head0, zero16, km16),
                          k_ref[0, 2 * N_META:LANE, :]], axis=0)
    vm = jnp.concatenate([v_ref[0, 0:N_META, :], v_ref[0, 0:N_META, :], v_ref[0, 2 * N_META:LANE, :]], axis=0)
    mb = mb_ref[0]

    def stack(q):
        q = q * jnp.asarray(D_HEAD ** -0.5, q.dtype)
        zero = jnp.zeros_like(q)
        return jnp.concatenate([jnp.where(head0, q, zero), jnp.where(head0, zero, q)], axis=0)

    def unstack(o, n):
        return jnp.where(head0, o[0:n], o[n:2 * n])

    q_all = q_ref[0] * jnp.asarray(D_HEAD ** -0.5, q_ref.dtype)
    smq_ref[...] = lax.dot_general(q_all, km, nt, preferred_element_type=F32)

    def meta_scores(q0, n, bias):
        blk = smq_ref[pl.ds(q0, n), :]
        return jnp.concatenate([blk, blk], axis=0) + bias

    sm = meta_scores(0, N_META, jnp.concatenate([mb[0:N_META], mb[GRID_W:GRID_W + N_META]], axis=0))
    pm = jnp.exp2(sm - jnp.max(sm, axis=-1, keepdims=True))
    om = jnp.dot(pm.astype(BF16), vm, preferred_element_type=F32)
    om = om / jnp.sum(pm, axis=-1, keepdims=True)
    o_ref[0, 0:N_META, :] = unstack(om, N_META).astype(o_ref.dtype)

    def window(t):
        t = jnp.minimum(t, rows - 1)
        rs = jnp.clip(t - NA_ROWS // 2, 0, rows - NA_ROWS)
        q0 = pl.multiple_of(N_META + t * GRID_W, 16)
        k0 = pl.multiple_of(N_META + rs * GRID_W, 16)
        return q0, k0, t - rs

    def scores(t, s_ref):
        q0, k0, off = window(t)
        qs = stack(q_ref[0, pl.ds(q0, GRID_W), :])
        kw = k_ref[0, pl.ds(k0, n_win), :]
        s_ref[...] = lax.dot_general(qs, kw, nt, preferred_element_type=F32) + bias_ref[0, off]

    dyn_zero = pl.multiple_of(jnp.minimum(pl.program_id(0), 0), LANE)

    def lane_tiles(x):
        return [x[:, i:i + LANE] for i in range(0, x.shape[1], LANE)]

    def softmax(t, s_ref, p_ref, l_ref):
        q0, _, _ = window(t)
        s = s_ref[pl.ds(dyn_zero, LANE), :]
        sm = meta_scores(q0, GRID_W, mb)
        m = jnp.max(functools.reduce(jnp.maximum, lane_tiles(s) + [sm]), axis=-1, keepdims=True)
        p = jnp.exp2(s - m)
        pm = jnp.exp2(sm - m)
        l_ref[...] = jnp.sum(functools.reduce(jnp.add, lane_tiles(p) + [pm]), axis=-1, keepdims=True)
        p_ref[:, 0:n_win] = p.astype(BF16)
        p_ref[:, n_win:] = pm.astype(BF16)

    def values(t, p_ref, l_ref):
        q0, k0, _ = window(t)
        vw = v_ref[0, pl.ds(k0, n_win), :]
        o = (jnp.dot(p_ref[:, 0:n_win], vw, preferred_element_type=F32)
             + jnp.dot(p_ref[:, n_win:], vm, preferred_element_type=F32))
        o = o / l_ref[...]
        o_ref[0, pl.ds(q0, GRID_W), :] = unstack(o, GRID_W).astype(o_ref.dtype)

    def step(t, s_in, s_out, pl_in, pl_out):
        scores(t + 4, s_out[0])
        scores(t + 5, s_out[1])
        values(t, *pl_in[0])
        values(t + 1, *pl_in[1])
        softmax(t + 2, s_in[0], *pl_out[0])
        softmax(t + 3, s_in[1], *pl_out[1])

    set_0 = ((p_a, l_a), (p_b, l_b))
    set_1 = ((p_c, l_c), (p_d, l_d))
    scores(0, s_c)
    scores(1, s_d)
    softmax(0, s_c, *set_0[0])
    softmax(1, s_d, *set_0[1])
    scores(2, s_a)
    scores(3, s_b)

    def quad_body(u, carry):
        t = 4 * u
        step(t, (s_a, s_b), (s_c, s_d), set_0, set_1)
        step(t + 2, (s_c, s_d), (s_a, s_b), set_1, set_0)
        return carry

    lax.fori_loop(0, rows // 4, quad_body, 0, unroll=2)
    if lp > seq_len:
        o_ref[0, seq_len:lp, :] = jnp.zeros((lp - seq_len, LANE), o_ref.dtype)


def _attention(z3, bias_tab, mb_tab, *, rows, seq_len):
    b, lp, _ = z3.shape
    assert rows % 4 == 0
    n_pairs = N_ATT_HEADS // 2
    n_keys = NA_ROWS * GRID_W
    blk = lambda off: pl.BlockSpec((1, lp, LANE), lambda hp, bi: (bi, 0, off + hp))
    scratch = ([pltpu.VMEM((lp, LANE), F32)]
               + [pltpu.VMEM((LANE, n_keys), F32) for _ in range(4)]
               + [pltpu.VMEM((LANE, n_keys + LANE), BF16) for _ in range(4)]
               + [pltpu.VMEM((LANE, 1), F32) for _ in range(4)])
    return pl.pallas_call(
        functools.partial(_attn_kernel, rows=rows, seq_len=seq_len),
        out_shape=jax.ShapeDtypeStruct((b, lp, D_ATT), BF16),
        grid=(n_pairs, b),
        in_specs=[blk(0), blk(n_pairs), blk(2 * n_pairs),
                  pl.BlockSpec((1, NA_ROWS, LANE, n_keys), lambda hp, bi: (hp, 0, 0, 0)),
                  pl.BlockSpec((1, LANE, LANE), lambda hp, bi: (hp, 0, 0))],
        out_specs=pl.BlockSpec((1, lp, LANE), lambda hp, bi: (bi, 0, hp)),
        scratch_shapes=scratch,
        compiler_params=_cparams(2),
        name="nbr_attention",
    )(z3, z3, z3, bias_tab, mb_tab)


def _attn_bias_tables(rel_bias, meta_bias):
    h, n_dr, n_dc = rel_bias.shape
    c = np.arange(GRID_W)[:, None]
    j = np.arange(GRID_W)[None, :]
    cs = np.clip(c - NA_COLS // 2, 0, GRID_W - NA_COLS)
    valid = (j >= cs) & (j < cs + NA_COLS)
    ext = jnp.zeros((h, n_dr, 2 * GRID_W), F32)
    ext = lax.dynamic_update_slice(ext, rel_bias.astype(F32), (0, 0, GRID_W - NA_COLS))
    skew = jnp.tile(ext, (1, 1, GRID_W))[:, :, :GRID_W * (2 * GRID_W - 1)]
    toep = skew.reshape(h, n_dr, GRID_W, 2 * GRID_W - 1)[..., GRID_W - 1:]
    toep = jnp.where(valid[None, None], toep * LOG2E, NEG_BIG)
    t = jnp.stack([toep[:, NA_ROWS - 1 - oi:2 * NA_ROWS - 1 - oi] for oi in range(NA_ROWS)], axis=1)
    t = t.transpose(0, 1, 3, 2, 4)
    t = t.reshape(h // 2, 2, NA_ROWS, GRID_W, NA_ROWS * GRID_W)
    t = t.transpose(0, 2, 1, 3, 4).reshape(h // 2, NA_ROWS, 2 * GRID_W, NA_ROWS * GRID_W)
    mbp = (meta_bias.astype(F32) * LOG2E).reshape(h // 2, 2, N_META)
    neg = jnp.full((h // 2, N_META), NEG_BIG, F32)
    mb0 = jnp.concatenate([mbp[:, 0], neg], axis=-1)
    mb1 = jnp.concatenate([neg, mbp[:, 1]], axis=-1)
    mb = jnp.stack([mb0, mb1], axis=1)
    mb = jnp.pad(mb, ((0, 0), (0, 0), (0, LANE - 2 * N_META)), constant_values=NEG_BIG)
    mb = jnp.broadcast_to(mb[:, :, None, :], (h // 2, 2, GRID_W, LANE))
    return t, mb.reshape(h // 2, 2 * GRID_W, LANE)


def _fft_kernel(c_ref, s_ref, ab_ref, w_ref, o_ref):
    a = ab_ref[0, :, 0:D_FFT]
    b = ab_ref[0, :, D_FFT:2 * D_FFT]
    f = (jnp.dot(c_ref[...], a, preferred_element_type=F32)
         + jnp.dot(s_ref[...], b, preferred_element_type=F32))
    o_ref[0] = jnp.dot(f.astype(BF16), w_ref[...], preferred_element_type=F32).astype(o_ref.dtype)


def _fourier(cmat, smat, ab3, w_bd, *, tk):
    b, lp, _ = ab3.shape
    return pl.pallas_call(
        _fft_kernel,
        out_shape=jax.ShapeDtypeStruct((b, lp, D_FFT), BF16),
        grid=(lp // tk, b),
        in_specs=[pl.BlockSpec((tk, lp), lambda j, bi: (j, 0)),
                  pl.BlockSpec((tk, lp), lambda j, bi: (j, 0)),
                  pl.BlockSpec((1, lp, 2 * D_FFT), lambda j, bi: (bi, 0, 0)),
                  pl.BlockSpec((D_FFT, D_FFT), lambda j, bi: (0, 0))],
        out_specs=pl.BlockSpec((1, tk, D_FFT), lambda j, bi: (bi, j, 0)),
        compiler_params=_cparams(2),
        name="fourier_mix",
    )(cmat, smat, ab3, w_bd)


def _twiddles(seq_len, lp):
    theta = 2.0 * math.pi / seq_len
    k = jnp.arange(lp, dtype=jnp.int32)[:, None]
    t1 = (GRID_W * jnp.arange(lp // GRID_W, dtype=jnp.int32))[None, :]
    t0 = jnp.arange(GRID_W, dtype=jnp.int32)[None, :]
    ang_a = ((k * t1) % seq_len).astype(F32) * theta
    ang_b = ((k * t0) % seq_len).astype(F32) * theta
    ca, sa = jnp.cos(ang_a)[:, :, None], jnp.sin(ang_a)[:, :, None]
    cb, sb = jnp.cos(ang_b)[:, None, :], jnp.sin(ang_b)[:, None, :]
    cmat = (ca * cb - sa * sb).reshape(lp, lp)
    smat = (sa * cb + ca * sb).reshape(lp, lp)
    idx = jnp.arange(lp)
    valid = (idx[:, None] < seq_len) & (idx[None, :] < seq_len)
    scale = seq_len ** -0.5
    cmat = jnp.where(valid, cmat * scale, 0.0).astype(BF16)
    smat = jnp.where(valid, -smat * scale, 0.0).astype(BF16)
    return cmat, smat


def _channel_dft_matrix():
    c = np.arange(D_FFT_HEAD)
    ang = 2.0 * np.pi * ((c[:, None] * c[None, :]) % D_FFT_HEAD) / D_FFT_HEAD
    cc = np.cos(ang) / math.sqrt(D_FFT_HEAD)
    sc = np.sin(ang) / math.sqrt(D_FFT_HEAD)
    out = np.zeros((D_FFT, 2 * D_FFT), np.float32)
    for g in range(N_FFT_HEADS):
        sl = slice(g * D_FFT_HEAD, (g + 1) * D_FFT_HEAD)
        out[sl, sl] = cc
        out[sl, D_FFT + g * D_FFT_HEAD:D_FFT + (g + 1) * D_FFT_HEAD] = sc
    return jnp.asarray(out, BF16)


def _block_diag(w):
    g, c, e = w.shape
    out = jnp.zeros((g * c, g * e), w.dtype)
    for i in range(g):
        out = lax.dynamic_update_slice(out, w[i], (i * c, i * e))
    return out


POOL_PAD = 16


def _pool_kernel(u_ref, w_ref, sc_ref, o_ref, s0, s1, s2, s3, s4, *, seq_len):
    lp = o_ref.shape[1]
    r_tot = lp + 2 * POOL_PAD
    lo, hi = 8, r_tot - 8
    row = lax.broadcasted_iota(jnp.int32, (lp, 1), 0)
    live = row < seq_len
    x = jnp.where(live, u_ref[0].astype(F32), 0.0)

    zeros_pad = jnp.zeros((POOL_PAD, D_POOL), F32)
    s0[0:POOL_PAD, :] = zeros_pad
    s0[POOL_PAD:POOL_PAD + lp, :] = x
    s0[POOL_PAD + lp:r_tot, :] = zeros_pad
    for s in (s1, s2, s3, s4):
        s[0:POOL_PAD, :] = zeros_pad
        s[POOL_PAD + lp:r_tot, :] = zeros_pad
    s1[lo:hi, :] = s0[lo - 1:hi - 1, :] + s0[lo:hi, :]
    s2[lo:hi, :] = s1[lo - 1:hi - 1, :] + s1[lo + 1:hi + 1, :]
    s3[lo:hi, :] = s2[lo - 2:hi - 2, :] + s2[lo + 2:hi + 2, :]
    s4[lo:hi, :] = s3[lo - 4:hi - 4, :] + s3[lo + 4:hi + 4, :]

    lane = lax.broadcasted_iota(jnp.int32, (1, D_POOL), 1)
    group = jnp.right_shift(lane, 6)
    half = jnp.where(group == 0, 1, jnp.where(group == 1, 2, jnp.where(group == 2, 4, 8)))
    cnt = jnp.minimum(row + half, seq_len) - jnp.maximum(row - half, 0)
    cnt = jnp.maximum(cnt, 1).astype(F32)
    sl = slice(POOL_PAD, POOL_PAD + lp)
    wsum = jnp.where(group == 0, s1[sl, :],
                     jnp.where(group == 1, s2[sl, :], jnp.where(group == 2, s3[sl, :], s4[sl, :])))
    p = jnp.where(live, wsum / cnt - x, 0.0)
    y = jnp.dot(p.astype(BF16), w_ref[...], preferred_element_type=F32) * sc_ref[...]
    o_ref[0] = y.astype(o_ref.dtype)


def _pool(z3, w_bd, scale, *, seq_len):
    b, lp, d_in = z3.shape
    col_block = (d_in - D_POOL) // D_POOL
    scratch = [pltpu.VMEM((lp + 2 * POOL_PAD, D_POOL), F32) for _ in range(5)]
    return pl.pallas_call(
        functools.partial(_pool_kernel, seq_len=seq_len),
        out_shape=jax.ShapeDtypeStruct((b, lp, D_POOL), BF16),
        grid=(b,),
        in_specs=[pl.BlockSpec((1, lp, D_POOL), lambda bi: (bi, 0, col_block)),
                  pl.BlockSpec((D_POOL, D_POOL), lambda bi: (0, 0)),
                  pl.BlockSpec((1, D_POOL), lambda bi: (0, 0))],
        out_specs=pl.BlockSpec((1, lp, D_POOL), lambda bi: (bi, 0, 0)),
        scratch_shapes=scratch,
        compiler_params=_cparams(1),
        name="pool_mix",
    )(z3, w_bd, scale.reshape(1, D_POOL))


def _mix_project(a_ref, f_ref, p_ref, g_ref, w_ref):
    an = _rms(a_ref[...].astype(F32), g_ref[:, 0:D_ATT]).astype(BF16)
    fn = _rms(f_ref[...].astype(F32), g_ref[:, D_ATT:D_ATT + D_FFT]).astype(BF16)
    pn = _rms(p_ref[...].astype(F32), g_ref[:, D_ATT + D_FFT:]).astype(BF16)
    acc = jnp.dot(an, w_ref[0:D_ATT, :], preferred_element_type=F32)
    acc += jnp.dot(fn, w_ref[D_ATT:D_ATT + D_FFT, :], preferred_element_type=F32)
    acc += jnp.dot(pn, w_ref[D_ATT + D_FFT:, :], preferred_element_type=F32)
    return acc


def _outproj_kernel(a_ref, f_ref, p_ref, h_ref, g_ref, w_ref, o_ref):
    o_ref[...] = h_ref[...] + _mix_project(a_ref, f_ref, p_ref, g_ref, w_ref)


def _outproj(a, f, p, h, g, w, *, tm=1024):
    n, d = h.shape
    row = lambda width: pl.BlockSpec((tm, width), lambda i: (i, 0))
    return pl.pallas_call(
        _outproj_kernel,
        out_shape=jax.ShapeDtypeStruct((n, d), F32),
        grid=(n // tm,),
        in_specs=[row(D_ATT), row(D_FFT), row(D_POOL), row(d),
                  pl.BlockSpec((1, d), lambda i: (0, 0)),
                  pl.BlockSpec((d, d), lambda i: (0, 0))],
        out_specs=row(d),
        compiler_params=_cparams(1),
        name="out_proj",
    )(a, f, p, h, g.reshape(1, d), w)


def _ffn_kernel(h_ref, g_ref, wg_ref, wu_ref, wd_ref, o_ref, xn_ref, acc_ref):
    c = pl.program_id(1)

    @pl.when(c == 0)
    def _():
        xn_ref[...] = _rms(h_ref[...], g_ref[...]).astype(BF16)

    xn = xn_ref[...]
    gate = jnp.dot(xn, wg_ref[...], preferred_element_type=F32)
    up = jnp.dot(xn, wu_ref[...], preferred_element_type=F32)
    hh = (gate * jax.nn.sigmoid(gate) * up).astype(BF16)
    part = jnp.dot(hh, wd_ref[...], preferred_element_type=F32)

    @pl.when(c == 0)
    def _():
        acc_ref[...] = part

    @pl.when(c > 0)
    def _():
        acc_ref[...] += part

    @pl.when(c == pl.num_programs(1) - 1)
    def _():
        o_ref[...] = h_ref[...] + acc_ref[...]


def _ffn(h, g, wg, wu, wd, *, tm=1024, n_chunks=2):
    n, d = h.shape
    d_ff = wg.shape[1]
    fc = d_ff // n_chunks
    return pl.pallas_call(
        _ffn_kernel,
        out_shape=jax.ShapeDtypeStruct((n, d), F32),
        grid=(n // tm, n_chunks),
        in_specs=[pl.BlockSpec((tm, d), lambda i, c: (i, 0)),
                  pl.BlockSpec((1, d), lambda i, c: (0, 0)),
                  pl.BlockSpec((d, fc), lambda i, c: (0, c)),
                  pl.BlockSpec((d, fc), lambda i, c: (0, c)),
                  pl.BlockSpec((fc, d), lambda i, c: (c, 0))],
        out_specs=pl.BlockSpec((tm, d), lambda i, c: (i, 0)),
        scratch_shapes=[pltpu.VMEM((tm, d), BF16), pltpu.VMEM((tm, d), F32)],
        compiler_params=_cparams(2),
        name="swiglu_ffn",
    )(h, g.reshape(1, d), wg, wu, wd)


MOE_TILE = 1024
EXPERT_ROWS = 512
R_GATE, R_EXPERT, R_RANK = 0, 2, 4


def _router_kernel(h_ref, g_ref, wr_ref, tri_ref, xn_ref, rec_ref, cnt_ref, *, seq_len, lp):
    t = h_ref.shape[0]
    y = _rms(h_ref[...], g_ref[...])
    xn_ref[...] = y
    logits = jnp.dot(y, wr_ref[...], preferred_element_type=F32, precision=lax.Precision.HIGHEST)
    lane = lax.broadcasted_iota(jnp.int32, (t, LANE), 1)
    lg = jnp.where(lane < N_EXPERTS, logits, -jnp.inf)
    m1 = jnp.max(lg, axis=-1, keepdims=True)
    i1 = jnp.min(jnp.where(lg == m1, lane, LANE), axis=-1, keepdims=True)
    lg2 = jnp.where(lane == i1, -jnp.inf, lg)
    m2 = jnp.max(lg2, axis=-1, keepdims=True)
    i2 = jnp.min(jnp.where(lg2 == m2, lane, LANE), axis=-1, keepdims=True)
    e2 = jnp.exp(m2 - m1)
    g1 = 1.0 / (1.0 + e2)
    g2 = e2 / (1.0 + e2)
    rowf = (pl.program_id(0) * t + lax.broadcasted_iota(jnp.int32, (t, 1), 0)).astype(F32)
    seq = jnp.floor((rowf + 0.5) * (1.0 / lp))
    live = (rowf - seq * lp) < seq_len
    first = lane == i1
    second = lane == i2
    member = jnp.where((first | second) & live, 1.0, 0.0)
    rank = jnp.dot(tri_ref[...], member.astype(BF16), preferred_element_type=F32)
    r1 = jnp.sum(jnp.where(first, rank, 0.0), axis=-1, keepdims=True)
    r2 = jnp.sum(jnp.where(second, rank, 0.0), axis=-1, keepdims=True)
    dead = jnp.logical_not(live)
    rec = jnp.zeros((t, LANE), F32)
    for ln, val in ((R_GATE, jnp.where(dead, 0.0, g1)), (R_GATE + 1, jnp.where(dead, 0.0, g2)),
                    (R_EXPERT, jnp.where(dead, -1.0, i1.astype(F32))),
                    (R_EXPERT + 1, jnp.where(dead, -1.0, i2.astype(F32))),
                    (R_RANK, r1), (R_RANK + 1, r2)):
        rec = jnp.where(lane == ln, val, rec)
    rec_ref[...] = rec
    cnt_ref[0] = jnp.broadcast_to(jnp.sum(member, axis=0, keepdims=True), (8, LANE))


def _router(h, g, w_router_pad, tri, *, seq_len, lp):
    n, d = h.shape
    t = MOE_TILE
    row = lambda width: pl.BlockSpec((t, width), lambda i: (i, 0))
    return pl.pallas_call(
        functools.partial(_router_kernel, seq_len=seq_len, lp=lp),
        out_shape=(jax.ShapeDtypeStruct((n, d), F32),
                   jax.ShapeDtypeStruct((n, LANE), F32),
                   jax.ShapeDtypeStruct((n // t, 8, LANE), F32)),
        grid=(n // t,),
        in_specs=[row(d),
                  pl.BlockSpec((1, d), lambda i: (0, 0)),
                  pl.BlockSpec((d, LANE), lambda i: (0, 0)),
                  pl.BlockSpec((t, t), lambda i: (0, 0))],
        out_specs=(row(d), row(LANE), pl.BlockSpec((1, 8, LANE), lambda i: (i, 0, 0))),
        compiler_params=_cparams(1),
        name="moe_router",
    )(h, g.reshape(1, d), w_router_pad, tri)


def _dispatch_plan(rec, counts, pad_token, *, n_row_tiles):
    n = rec.shape[0]
    n_tiles = counts.shape[0]
    cnt = counts[:, 0, :N_EXPERTS]
    before = jnp.cumsum(cnt, axis=0) - cnt
    total = jnp.sum(cnt, axis=0)
    padded = jnp.ceil(total / EXPERT_ROWS) * EXPERT_ROWS
    ends = jnp.cumsum(padded)
    start = ends - padded
    base = (start[None, :] + before)[:, None, :]
    pad_rank = np.cumsum(np.asarray(pad_token, np.int64)) - 1
    spill = n_row_tiles * EXPERT_ROWS + 2 * pad_rank.reshape(n_tiles, MOE_TILE)
    slots_out, slots_back = [], []
    for k in range(2):
        e = rec[:, R_EXPERT + k].reshape(n_tiles, MOE_TILE, 1)
        r = rec[:, R_RANK + k].reshape(n_tiles, MOE_TILE)
        hit = e == jnp.arange(N_EXPERTS, dtype=F32)[None, None, :]
        slot = (jnp.sum(jnp.where(hit, base, 0.0), axis=-1) + r).astype(jnp.int32)
        dead = e[..., 0] < 0
        slots_out.append(jnp.where(dead, jnp.asarray(spill + k, jnp.int32), slot))
        slots_back.append(jnp.where(dead, 0, slot))
    pos_out = jnp.stack(slots_out, axis=-1).reshape(n * 2)
    pos_back = jnp.stack(slots_back, axis=-1).reshape(n * 2)
    tile_row0 = jnp.arange(n_row_tiles, dtype=F32) * EXPERT_ROWS
    tile_expert = jnp.sum(tile_row0[:, None] >= ends[None, :], axis=-1)
    tile_live = (tile_row0 < ends[-1]).astype(jnp.int32)
    tile_expert = jnp.minimum(tile_expert, N_EXPERTS - 1).astype(jnp.int32)
    return pos_out, pos_back, tile_expert, tile_live


ROW_GROUP = 8


def _row_copy(src_ref, src_row, dst_ref, dst_row, sem):
    return pltpu.make_async_copy(src_ref.at[pl.ds(src_row, 1)], dst_ref.at[pl.ds(dst_row, 1)], sem)


def _dispatch_kernel(pos_ref, x_ref, zero_ref, xs_ref, sem):
    del zero_ref
    t = x_ref.shape[0]
    base = pl.program_id(0) * t * 2

    def issue(g, carry):
        row0 = pl.multiple_of(g * ROW_GROUP, ROW_GROUP)
        group = x_ref.at[pl.ds(row0, ROW_GROUP)]
        idx0 = base + 2 * row0
        for c in range(ROW_GROUP):
            for k in range(2):
                p = pos_ref[idx0 + (2 * c + k)]
                _row_copy(group, c, xs_ref, p, sem).start(priority=k)
        return carry

    def drain(g, carry):
        for _ in range(2 * ROW_GROUP):
            _row_copy(x_ref, 0, xs_ref, 0, sem).wait()
        return carry

    lax.fori_loop(0, t // ROW_GROUP, issue, 0)
    lax.fori_loop(0, t // ROW_GROUP, drain, 0)


def _dispatch(pos, xn, n_rows):
    n, d = xn.shape
    t = MOE_TILE
    grid_spec = pltpu.PrefetchScalarGridSpec(
        num_scalar_prefetch=1,
        grid=(n // t,),
        in_specs=[pl.BlockSpec((t, d), lambda i, pos: (i, 0)),
                  pl.BlockSpec(memory_space=pl.ANY)],
        out_specs=pl.BlockSpec(memory_space=pl.ANY),
        scratch_shapes=[pltpu.SemaphoreType.DMA],
    )
    return pl.pallas_call(
        _dispatch_kernel,
        out_shape=jax.ShapeDtypeStruct((n_rows, d), F32),
        grid_spec=grid_spec,
        input_output_aliases={2: 0},
        compiler_params=_cparams(1),
        name="moe_dispatch",
    )(pos, xn, jnp.zeros((n_rows, d), F32))


def _expert_kernel(te_ref, live_ref, x_ref, wg_ref, wu_ref, wd_ref, o_ref, xb_ref, acc_ref):
    del te_ref
    i, c = pl.program_id(0), pl.program_id(1)
    last = pl.num_programs(1) - 1
    live = live_ref[i] > 0

    @pl.when(live)
    def _():
        @pl.when(c == 0)
        def _():
            xb_ref[...] = x_ref[...].astype(BF16)

        xb = xb_ref[...]
        gate = jnp.dot(xb, wg_ref[0, 0], preferred_element_type=F32)
        up = jnp.dot(xb, wu_ref[0, 0], preferred_element_type=F32)
        hh = (gate * jax.nn.sigmoid(gate) * up).astype(BF16)
        part = jnp.dot(hh, wd_ref[0, 0], preferred_element_type=F32)

        @pl.when(c == 0)
        def _():
            acc_ref[...] = part

        @pl.when(c > 0)
        def _():
            acc_ref[...] += part

        @pl.when(c == last)
        def _():
            o_ref[...] = acc_ref[...]

    @pl.when(jnp.logical_not(live) & (c == last))
    def _():
        o_ref[...] = jnp.zeros(o_ref.shape, o_ref.dtype)


def _experts(tile_expert, tile_live, xs, n_row_tiles, layer, wg, wu, wd, *, n_chunks=2):
    d = xs.shape[1]
    tm = EXPERT_ROWS
    n_rows = n_row_tiles * tm
    fc = wg.shape[3] // n_chunks
    chunk = lambda i, c, te, lv: jnp.where(lv[i] > 0, c, n_chunks - 1)
    grid_spec = pltpu.PrefetchScalarGridSpec(
        num_scalar_prefetch=2,
        grid=(n_rows // tm, n_chunks),
        in_specs=[pl.BlockSpec((tm, d), lambda i, c, te, lv: (i, 0)),
                  pl.BlockSpec((1, 1, d, fc), lambda i, c, te, lv: (layer, te[i], 0, chunk(i, c, te, lv))),
                  pl.BlockSpec((1, 1, d, fc), lambda i, c, te, lv: (layer, te[i], 0, chunk(i, c, te, lv))),
                  pl.BlockSpec((1, 1, fc, d), lambda i, c, te, lv: (layer, te[i], chunk(i, c, te, lv), 0))],
        out_specs=pl.BlockSpec((tm, d), lambda i, c, te, lv: (i, 0)),
        scratch_shapes=[pltpu.VMEM((tm, d), BF16), pltpu.VMEM((tm, d), F32)],
    )
    return pl.pallas_call(
        _expert_kernel,
        out_shape=jax.ShapeDtypeStruct((n_rows, d), F32),
        grid_spec=grid_spec,
        compiler_params=_cparams(2),
        name="moe_experts",
    )(tile_expert, tile_live, xs, wg, wu, wd)


def _combine_kernel(pos_ref, h_ref, rec_ref, ys_ref, o_ref, a0, a1, b0, b1, sem_a, sem_b):
    t = h_ref.shape[0]
    step = pl.program_id(0)
    n_steps = pl.num_programs(0)

    def request(tile, bufs, sem):
        base = tile * t * 2

        def issue(g, carry):
            row0 = pl.multiple_of(g * ROW_GROUP, ROW_GROUP)
            groups = [buf.at[pl.ds(row0, ROW_GROUP)] for buf in bufs]
            idx0 = base + 2 * row0
            for c in range(ROW_GROUP):
                for k in range(2):
                    p = pos_ref[idx0 + (2 * c + k)]
                    _row_copy(ys_ref, p, groups[k], c, sem).start(priority=k)
            return carry

        lax.fori_loop(0, t // ROW_GROUP, issue, 0)

    def finish(bufs, sem):
        def drain(g, carry):
            for _ in range(2 * ROW_GROUP):
                _row_copy(ys_ref, 0, bufs[0], 0, sem).wait()
            return carry

        lax.fori_loop(0, t // ROW_GROUP, drain, 0)
        rec = rec_ref[...]
        g1 = rec[:, R_GATE:R_GATE + 1]
        g2 = rec[:, R_GATE + 1:R_GATE + 2]
        o_ref[...] = h_ref[...] + g1 * bufs[0][...] + g2 * bufs[1][...]

    set_a, set_b = ((a0, a1), sem_a), ((b0, b1), sem_b)

    @pl.when(step == 0)
    def _():
        request(0, *set_a)

    for parity, mine, other in ((0, set_a, set_b), (1, set_b, set_a)):
        @pl.when(step % 2 == parity)
        def _():
            @pl.when(step + 1 < n_steps)
            def _():
                request(step + 1, *other)

            finish(*mine)


def _combine(pos, h, rec, ys):
    n, d = h.shape
    t = MOE_TILE
    grid_spec = pltpu.PrefetchScalarGridSpec(
        num_scalar_prefetch=1,
        grid=(n // t,),
        in_specs=[pl.BlockSpec((t, d), lambda i, pos: (i, 0)),
                  pl.BlockSpec((t, LANE), lambda i, pos: (i, 0)),
                  pl.BlockSpec(memory_space=pl.ANY)],
        out_specs=pl.BlockSpec((t, d), lambda i, pos: (i, 0)),
        scratch_shapes=[pltpu.VMEM((t, d), F32) for _ in range(4)]
        + [pltpu.SemaphoreType.DMA, pltpu.SemaphoreType.DMA],
    )
    return pl.pallas_call(
        _combine_kernel,
        out_shape=jax.ShapeDtypeStruct((n, d), F32),
        grid_spec=grid_spec,
        compiler_params=_cparams(1),
        name="moe_combine",
    )(pos, h, rec, ys)


def _norm_kernel(x_ref, tail_ref, g_ref, o_ref):
    tm = x_ref.shape[1]
    o_ref[0, 0:tm - N_META, :] = _rms(x_ref[0, N_META:tm, :], g_ref[...])
    o_ref[0, tm - N_META:tm, :] = _rms(tail_ref[0], g_ref[...])


def _final_norm(h3, g, *, t_len, tm=1024):
    b, lp, d = h3.shape
    assert t_len % tm == 0 and tm % N_META == 0
    return pl.pallas_call(
        _norm_kernel,
        out_shape=jax.ShapeDtypeStruct((b, t_len, d), F32),
        grid=(b, t_len // tm),
        in_specs=[pl.BlockSpec((1, tm, d), lambda bi, j: (bi, j, 0)),
                  pl.BlockSpec((1, N_META, d), lambda bi, j: (bi, (j + 1) * (tm // N_META), 0)),
                  pl.BlockSpec((1, d), lambda bi, j: (0, 0))],
        out_specs=pl.BlockSpec((1, tm, d), lambda bi, j: (bi, j, 0)),
        compiler_params=_cparams(2),
        name="final_norm",
    )(h3, h3, g.reshape(1, d))


def _fft_row_tile(lp):
    best = 16
    for tk in range(16, lp + 1, 16):
        if lp % tk == 0 and tk * lp * 2 <= 4608 * 1024:
            best = tk
    return best


def _trunk(x, meta_tokens, prm):
    b, t_len, d = x.shape
    depth = len(prm["w_in"])
    seq_len = N_META + t_len
    lp = -(-seq_len // LANE) * LANE
    n = b * lp
    rows = t_len // GRID_W
    assert t_len % GRID_W == 0 and rows >= NA_ROWS and n % MOE_TILE == 0

    meta = jnp.broadcast_to(meta_tokens.astype(x.dtype)[None], (b, N_META, d))
    h = jnp.concatenate([meta, x, jnp.zeros((b, lp - seq_len, d), x.dtype)], axis=1).reshape(n, d)

    cmat, smat = _twiddles(seq_len, lp)
    tk = _fft_row_tile(lp)
    tri = jnp.tril(jnp.ones((MOE_TILE, MOE_TILE), BF16), -1)

    for i in range(depth):
        z, ab = _norm_matmul(h, prm["norm_mix"][i], prm["w_in"][i], prm["chan_dft"])
        z3 = z.reshape(b, lp, D_IN)
        a = _attention(z3, prm["attn_bias"][i], prm["attn_mbias"][i], rows=rows, seq_len=seq_len)
        f = _fourier(cmat, smat, ab.reshape(b, lp, 2 * D_FFT), prm["w_fft"][i], tk=tk)
        p = _pool(z3, prm["w_pool"][i], prm["pool_scale"][i], seq_len=seq_len)
        h = _outproj(a.reshape(n, D_ATT), f.reshape(n, D_FFT), p.reshape(n, D_POOL), h,
                     prm["norm_groups"][i], prm["w_out"][i])
        j = i // 2
        if i % 2 == 0:
            h = _ffn(h, prm["norm_ffn"][i], prm["w_ff_gate"][j], prm["w_ff_up"][j], prm["w_ff_down"][j])
        else:
            xn, rec, counts = _router(h, prm["norm_ffn"][i], prm["w_router"][j], tri,
                                      seq_len=seq_len, lp=lp)
            n_row_tiles = -(-2 * b * seq_len // EXPERT_ROWS) + N_EXPERTS
            pad_token = (np.arange(n) % lp) >= seq_len
            pos_out, pos_back, tile_expert, tile_live = _dispatch_plan(rec, counts, pad_token,
                                                                       n_row_tiles=n_row_tiles)
            n_spill = -(-2 * int(pad_token.sum()) // ROW_GROUP) * ROW_GROUP
            xs = _dispatch(pos_out, xn, n_row_tiles * EXPERT_ROWS + n_spill)
            ys = _experts(tile_expert, tile_live, xs, n_row_tiles, j,
                          prm["w_exp_gate"], prm["w_exp_up"], prm["w_exp_down"])
            h = _combine(pos_back, h, rec, ys)
    return _final_norm(h.reshape(b, lp, d), prm["norm_final"], t_len=t_len)


def kernel(x_prompt, x_sample, meta_tokens, norm_mix, w_in, w_fft, w_pool, pool_scale, rel_bias, meta_bias,
           norm_groups, w_out, norm_ffn, w_ff_gate, w_ff_up, w_ff_down, w_router, w_exp_gate, w_exp_up,
           w_exp_down, norm_final):
    depth = w_in.shape[0]
    tabs = [_attn_bias_tables(rel_bias[i], meta_bias[i]) for i in range(depth)]
    key_scale = np.ones((D_IN,), np.float32)
    key_scale[D_ATT:2 * D_ATT] = LOG2E
    per_layer = lambda w: [w[i].astype(BF16) for i in range(w.shape[0])]
    prm = {
        "norm_mix": norm_mix, "norm_groups": norm_groups, "norm_ffn": norm_ffn, "norm_final": norm_final,
        "pool_scale": pool_scale,
        "w_in": per_layer(w_in * key_scale), "w_out": per_layer(w_out),
        "w_ff_gate": per_layer(w_ff_gate), "w_ff_up": per_layer(w_ff_up), "w_ff_down": per_layer(w_ff_down),
        "w_exp_gate": w_exp_gate.astype(BF16), "w_exp_up": w_exp_up.astype(BF16),
        "w_exp_down": w_exp_down.astype(BF16),
        "w_router": jnp.pad(w_router, ((0, 0), (0, 0), (0, LANE - N_EXPERTS))),
        "w_fft": jnp.stack([_block_diag(w_fft[i]) for i in range(depth)]).astype(BF16),
        "w_pool": jnp.stack([_block_diag(w_pool[i]) for i in range(depth)]).astype(BF16),
        "chan_dft": _channel_dft_matrix(),
        "attn_bias": [tb[0] for tb in tabs], "attn_mbias": [tb[1] for tb in tabs],
    }
    y_prompt = _trunk(x_prompt, meta_tokens, prm)
    y_sample = _trunk(x_sample, meta_tokens, prm)
    return (y_prompt, y_sample)
```

```python
import functools
import math

import numpy as np
import jax
import jax.numpy as jnp
from jax import lax
from jax.experimental import pallas as pl
from jax.experimental.pallas import tpu as pltpu

D_MODEL = 1024
N_META = 16
GRID_W = 64
D_HEAD = 64
D_ATT = 512
N_ATT_HEADS = 8
D_FFT = 256
N_FFT_HEADS = 4
D_FFT_HEAD = 64
D_POOL = 256
POOL_WINDOWS = (2, 4, 8, 16)
D_POOL_GROUP = 64
D_IN = 2048
NA_ROWS = 8
NA_COLS = 16
N_EXPERTS = 8
EPS = 1e-6

LANE = 128
NEG_BIG = -1e30
LOG2E = math.log2(math.e)
VMEM_LIMIT = 56 * 1024 * 1024

F32 = jnp.float32
BF16 = jnp.bfloat16


def _cparams(n_axes, vmem=VMEM_LIMIT):
    return pltpu.CompilerParams(dimension_semantics=("arbitrary",) * n_axes,
                                vmem_limit_bytes=vmem)


def _rms(x, g):
    return x * lax.rsqrt(jnp.mean(x * x, axis=-1, keepdims=True) + EPS) * g


def _norm_matmul_kernel(x_ref, g_ref, w_ref, cdft_ref, o_ref, ab_ref, xn_ref, *, tn, fft_col):
    xn_ref[...] = _rms(x_ref[...], g_ref[...]).astype(BF16)
    for n0 in range(0, o_ref.shape[1], tn):
        zc = jnp.dot(xn_ref[...], w_ref[:, n0:n0 + tn], preferred_element_type=F32).astype(o_ref.dtype)
        o_ref[:, n0:n0 + tn] = zc
        if n0 <= fft_col and fft_col + D_FFT <= n0 + tn:
            ab_ref[...] = jnp.dot(zc[:, fft_col - n0:fft_col - n0 + D_FFT], cdft_ref[...],
                                  preferred_element_type=F32).astype(ab_ref.dtype)


def _norm_matmul(x, g, w, cdft, *, tm=1024, tn=512):
    n, d = x.shape
    n_out = w.shape[1]
    fft_col = 3 * D_ATT
    assert fft_col % tn + D_FFT <= tn
    return pl.pallas_call(
        functools.partial(_norm_matmul_kernel, tn=tn, fft_col=fft_col),
        out_shape=(jax.ShapeDtypeStruct((n, n_out), BF16), jax.ShapeDtypeStruct((n, 2 * D_FFT), BF16)),
        grid=(n // tm,),
        in_specs=[pl.BlockSpec((tm, d), lambda i: (i, 0)),
                  pl.BlockSpec((1, d), lambda i: (0, 0)),
                  pl.BlockSpec((d, n_out), lambda i: (0, 0)),
                  pl.BlockSpec((D_FFT, 2 * D_FFT), lambda i: (0, 0))],
        out_specs=(pl.BlockSpec((tm, n_out), lambda i: (i, 0)),
                   pl.BlockSpec((tm, 2 * D_FFT), lambda i: (i, 0))),
        scratch_shapes=[pltpu.VMEM((tm, d), BF16)],
        compiler_params=_cparams(1),
        name="norm_matmul",
    )(x, g.reshape(1, d), w, cdft)


def _attn_kernel(q_ref, k_ref, v_ref, bias_ref, mb_ref, o_ref, smq_ref, s_a, s_b, s_c, s_d,
                 p_a, p_b, p_c, p_d, l_a, l_b, l_c, l_d, *, rows, seq_len):
    lp = o_ref.shape[1]
    n_win = NA_ROWS * GRID_W
    lane = lax.broadcasted_iota(jnp.int32, (1, LANE), 1)
    head0 = lane < D_HEAD
    nt = (((1,), (1,)), ((), ()))

    km16 = k_ref[0, 0:N_META, :]
    zero16 = jnp.zeros_like(km16)
    km = jnp.concatenate([jnp.where(head0, km16, zero16), jnp.where(head0, zero16, km16),
                          k_ref[0, 2 * N_META:LANE, :]], axis=0)
    vm = jnp.concatenate([v_ref[0, 0:N_META, :], v_ref[0, 0:N_META, :], v_ref[0, 2 * N_META:LANE, :]], axis=0)
    mb = mb_ref[0]

    def stack(q):
        q = q * jnp.asarray(D_HEAD ** -0.5, q.dtype)
        zero = jnp.zeros_like(q)
        return jnp.concatenate([jnp.where(head0, q, zero), jnp.where(head0, zero, q)], axis=0)

    def unstack(o, n):
        return jnp.where(head0, o[0:n], o[n:2 * n])

    q_all = q_ref[0] * jnp.asarray(D_HEAD ** -0.5, q_ref.dtype)
    smq_ref[...] = lax.dot_general(q_all, km, nt, preferred_element_type=F32)

    def meta_scores(q0, n, bias):
        blk = smq_ref[pl.ds(q0, n), :]
        return jnp.concatenate([blk, blk], axis=0) + bias

    sm = meta_scores(0, N_META, jnp.concatenate([mb[0:N_META], mb[GRID_W:GRID_W + N_META]], axis=0))
    pm = jnp.exp2(sm - jnp.max(sm, axis=-1, keepdims=True))
    om = jnp.dot(pm.astype(BF16), vm, preferred_element_type=F32)
    om = om / jnp.sum(pm, axis=-1, keepdims=True)
    o_ref[0, 0:N_META, :] = unstack(om, N_META).astype(o_ref.dtype)

    def window(t):
        t = jnp.minimum(t, rows - 1)
        rs = jnp.clip(t - NA_ROWS // 2, 0, rows - NA_ROWS)
        q0 = pl.multiple_of(N_META + t * GRID_W, 16)
        k0 = pl.multiple_of(N_META + rs * GRID_W, 16)
        return q0, k0, t - rs

    def scores(t, s_ref):
        q0, k0, off = window(t)
        qs = stack(q_ref[0, pl.ds(q0, GRID_W), :])
        kw = k_ref[0, pl.ds(k0, n_win), :]
        s_ref[...] = lax.dot_general(qs, kw, nt, preferred_element_type=F32) + bias_ref[0, off]

    dyn_zero = pl.multiple_of(jnp.minimum(pl.program_id(0), 0), LANE)

    def lane_tiles(x):
        return [x[:, i:i + LANE] for i in range(0, x.shape[1], LANE)]

    def softmax(t, s_ref, p_ref, l_ref):
        q0, _, _ = window(t)
        s = s_ref[pl.ds(dyn_zero, LANE), :]
        sm = meta_scores(q0, GRID_W, mb)
        m = jnp.max(functools.reduce(jnp.maximum, lane_tiles(s) + [sm]), axis=-1, keepdims=True)
        p = jnp.exp2(s - m)
        pm = jnp.exp2(sm - m)
        l_ref[...] = jnp.sum(functools.reduce(jnp.add, lane_tiles(p) + [pm]), axis=-1, keepdims=True)
        p_ref[:, 0:n_win] = p.astype(BF16)
        p_ref[:, n_win:] = pm.astype(BF16)

    def values(t, p_ref, l_ref):
        q0, k0, _ = window(t)
        vw = v_ref[0, pl.ds(k0, n_win), :]
        o = (jnp.dot(p_ref[:, 0:n_win], vw, preferred_element_type=F32)
             + jnp.dot(p_ref[:, n_win:], vm, preferred_element_type=F32))
        o = o / l_ref[...]
        o_ref[0, pl.ds(q0, GRID_W), :] = unstack(o, GRID_W).astype(o_ref.dtype)

    def step(t, s_in, s_out, pl_in, pl_out):
        scores(t + 4, s_out[0])
        scores(t + 5, s_out[1])
        values(t, *pl_in[0])
        values(t + 1, *pl_in[1])
        softmax(t + 2, s_in[0], *pl_out[0])
        softmax(t + 3, s_in[1], *pl_out[1])

    set_0 = ((p_a, l_a), (p_b, l_b))
    set_1 = ((p_c, l_c), (p_d, l_d))
    scores(0, s_c)
    scores(1, s_d)
    softmax(0, s_c, *set_0[0])
    softmax(1, s_d, *set_0[1])
    scores(2, s_a)
    scores(3, s_b)

    def quad_body(u, carry):
        t = 4 * u
        step(t, (s_a, s_b), (s_c, s_d), set_0, set_1)
        step(t + 2, (s_c, s_d), (s_a, s_b), set_1, set_0)
        return carry

    lax.fori_loop(0, rows // 4, quad_body, 0, unroll=2)
    if lp > seq_len:
        o_ref[0, seq_len:lp, :] = jnp.zeros((lp - seq_len, LANE), o_ref.dtype)


def _attention(z3, bias_tab, mb_tab, *, rows, seq_len):
    b, lp, _ = z3.shape
    assert rows % 4 == 0
    n_pairs = N_ATT_HEADS // 2
    n_keys = NA_ROWS * GRID_W
    blk = lambda off: pl.BlockSpec((1, lp, LANE), lambda hp, bi: (bi, 0, off + hp))
    scratch = ([pltpu.VMEM((lp, LANE), F32)]
               + [pltpu.VMEM((LANE, n_keys), F32) for _ in range(4)]
               + [pltpu.VMEM((LANE, n_keys + LANE), BF16) for _ in range(4)]
               + [pltpu.VMEM((LANE, 1), F32) for _ in range(4)])
    return pl.pallas_call(
        functools.partial(_attn_kernel, rows=rows, seq_len=seq_len),
        out_shape=jax.ShapeDtypeStruct((b, lp, D_ATT), BF16),
        grid=(n_pairs, b),
        in_specs=[blk(0), blk(n_pairs), blk(2 * n_pairs),
                  pl.BlockSpec((1, NA_ROWS, LANE, n_keys), lambda hp, bi: (hp, 0, 0, 0)),
                  pl.BlockSpec((1, LANE, LANE), lambda hp, bi: (hp, 0, 0))],
        out_specs=pl.BlockSpec((1, lp, LANE), lambda hp, bi: (bi, 0, hp)),
        scratch_shapes=scratch,
        compiler_params=_cparams(2),
        name="nbr_attention",
    )(z3, z3, z3, bias_tab, mb_tab)


def _attn_bias_tables(rel_bias, meta_bias):
    h, n_dr, n_dc = rel_bias.shape
    c = np.arange(GRID_W)[:, None]
    j = np.arange(GRID_W)[None, :]
    cs = np.clip(c - NA_COLS // 2, 0, GRID_W - NA_COLS)
    valid = (j >= cs) & (j < cs + NA_COLS)
    ext = jnp.zeros((h, n_dr, 2 * GRID_W), F32)
    ext = lax.dynamic_update_slice(ext, rel_bias.astype(F32), (0, 0, GRID_W - NA_COLS))
    skew = jnp.tile(ext, (1, 1, GRID_W))[:, :, :GRID_W * (2 * GRID_W - 1)]
    toep = skew.reshape(h, n_dr, GRID_W, 2 * GRID_W - 1)[..., GRID_W - 1:]
    toep = jnp.where(valid[None, None], toep * LOG2E, NEG_BIG)
    t = jnp.stack([toep[:, NA_ROWS - 1 - oi:2 * NA_ROWS - 1 - oi] for oi in range(NA_ROWS)], axis=1)
    t = t.transpose(0, 1, 3, 2, 4)
    t = t.reshape(h // 2, 2, NA_ROWS, GRID_W, NA_ROWS * GRID_W)
    t = t.transpose(0, 2, 1, 3, 4).reshape(h // 2, NA_ROWS, 2 * GRID_W, NA_ROWS * GRID_W)
    mbp = (meta_bias.astype(F32) * LOG2E).reshape(h // 2, 2, N_META)
    neg = jnp.full((h // 2, N_META), NEG_BIG, F32)
    mb0 = jnp.concatenate([mbp[:, 0], neg], axis=-1)
    mb1 = jnp.concatenate([neg, mbp[:, 1]], axis=-1)
    mb = jnp.stack([mb0, mb1], axis=1)
    mb = jnp.pad(mb, ((0, 0), (0, 0), (0, LANE - 2 * N_META)), constant_values=NEG_BIG)
    mb = jnp.broadcast_to(mb[:, :, None, :], (h // 2, 2, GRID_W, LANE))
    return t, mb.reshape(h // 2, 2 * GRID_W, LANE)


def _fft_kernel(c_ref, s_ref, ab_ref, w_ref, o_ref):
    a = ab_ref[0, :, 0:D_FFT]
    b = ab_ref[0, :, D_FFT:2 * D_FFT]
    f = (jnp.dot(c_ref[...], a, preferred_element_type=F32)
         + jnp.dot(s_ref[...], b, preferred_element_type=F32))
    o_ref[0] = jnp.dot(f.astype(BF16), w_ref[...], preferred_element_type=F32).astype(o_ref.dtype)


def _fourier(cmat, smat, ab3, w_bd, *, tk):
    b, lp, _ = ab3.shape
    return pl.pallas_call(
        _fft_kernel,
        out_shape=jax.ShapeDtypeStruct((b, lp, D_FFT), BF16),
        grid=(lp // tk, b),
        in_specs=[pl.BlockSpec((tk, lp), lambda j, bi: (j, 0)),
                  pl.BlockSpec((tk, lp), lambda j, bi: (j, 0)),
                  pl.BlockSpec((1, lp, 2 * D_FFT), lambda j, bi: (bi, 0, 0)),
                  pl.BlockSpec((D_FFT, D_FFT), lambda j, bi: (0, 0))],
        out_specs=pl.BlockSpec((1, tk, D_FFT), lambda j, bi: (bi, j, 0)),
        compiler_params=_cparams(2),
        name="fourier_mix",
    )(cmat, smat, ab3, w_bd)


def _twiddles(seq_len, lp):
    theta = 2.0 * math.pi / seq_len
    k = jnp.arange(lp, dtype=jnp.int32)[:, None]
    t1 = (GRID_W * jnp.arange(lp // GRID_W, dtype=jnp.int32))[None, :]
    t0 = jnp.arange(GRID_W, dtype=jnp.int32)[None, :]
    ang_a = ((k * t1) % seq_len).astype(F32) * theta
    ang_b = ((k * t0) % seq_len).astype(F32) * theta
    ca, sa = jnp.cos(ang_a)[:, :, None], jnp.sin(ang_a)[:, :, None]
    cb, sb = jnp.cos(ang_b)[:, None, :], jnp.sin(ang_b)[:, None, :]
    cmat = (ca * cb - sa * sb).reshape(lp, lp)
    smat = (sa * cb + ca * sb).reshape(lp, lp)
    idx = jnp.arange(lp)
    valid = (idx[:, None] < seq_len) & (idx[None, :] < seq_len)
    scale = seq_len ** -0.5
    cmat = jnp.where(valid, cmat * scale, 0.0).astype(BF16)
    smat = jnp.where(valid, -smat * scale, 0.0).astype(BF16)
    return cmat, smat


def _channel_dft_matrix():
    c = np.arange(D_FFT_HEAD)
    ang = 2.0 * np.pi * ((c[:, None] * c[None, :]) % D_FFT_HEAD) / D_FFT_HEAD
    cc = np.cos(ang) / math.sqrt(D_FFT_HEAD)
    sc = np.sin(ang) / math.sqrt(D_FFT_HEAD)
    out = np.zeros((D_FFT, 2 * D_FFT), np.float32)
    for g in range(N_FFT_HEADS):
        sl = slice(g * D_FFT_HEAD, (g + 1) * D_FFT_HEAD)
        out[sl, sl] = cc
        out[sl, D_FFT + g * D_FFT_HEAD:D_FFT + (g + 1) * D_FFT_HEAD] = sc
    return jnp.asarray(out, BF16)


def _block_diag(w):
    g, c, e = w.shape
    out = jnp.zeros((g * c, g * e), w.dtype)
    for i in range(g):
        out = lax.dynamic_update_slice(out, w[i], (i * c, i * e))
    return out


POOL_PAD = 16


def _pool_kernel(u_ref, w_ref, sc_ref, o_ref, s0, s1, s2, s3, s4, *, seq_len):
    lp = o_ref.shape[1]
    r_tot = lp + 2 * POOL_PAD
    lo, hi = 8, r_tot - 8
    row = lax.broadcasted_iota(jnp.int32, (lp, 1), 0)
    live = row < seq_len
    x = jnp.where(live, u_ref[0].astype(F32), 0.0)

    zeros_pad = jnp.zeros((POOL_PAD, D_POOL), F32)
    s0[0:POOL_PAD, :] = zeros_pad
    s0[POOL_PAD:POOL_PAD + lp, :] = x
    s0[POOL_PAD + lp:r_tot, :] = zeros_pad
    for s in (s1, s2, s3, s4):
        s[0:POOL_PAD, :] = zeros_pad
        s[POOL_PAD + lp:r_tot, :] = zeros_pad
    s1[lo:hi, :] = s0[lo - 1:hi - 1, :] + s0[lo:hi, :]
    s2[lo:hi, :] = s1[lo - 1:hi - 1, :] + s1[lo + 1:hi + 1, :]
    s3[lo:hi, :] = s2[lo - 2:hi - 2, :] + s2[lo + 2:hi + 2, :]
    s4[lo:hi, :] = s3[lo - 4:hi - 4, :] + s3[lo + 4:hi + 4, :]

    lane = lax.broadcasted_iota(jnp.int32, (1, D_POOL), 1)
    group = jnp.right_shift(lane, 6)
    half = jnp.where(group == 0, 1, jnp.where(group == 1, 2, jnp.where(group == 2, 4, 8)))
    cnt = jnp.minimum(row + half, seq_len) - jnp.maximum(row - half, 0)
    cnt = jnp.maximum(cnt, 1).astype(F32)
    sl = slice(POOL_PAD, POOL_PAD + lp)
    wsum = jnp.where(group == 0, s1[sl, :],
                     jnp.where(group == 1, s2[sl, :], jnp.where(group == 2, s3[sl, :], s4[sl, :])))
    p = jnp.where(live, wsum / cnt - x, 0.0)
    y = jnp.dot(p.astype(BF16), w_ref[...], preferred_element_type=F32) * sc_ref[...]
    o_ref[0] = y.astype(o_ref.dtype)


def _pool(z3, w_bd, scale, *, seq_len):
    b, lp, d_in = z3.shape
    col_block = (d_in - D_POOL) // D_POOL
    scratch = [pltpu.VMEM((lp + 2 * POOL_PAD, D_POOL), F32) for _ in range(5)]
    return pl.pallas_call(
        functools.partial(_pool_kernel, seq_len=seq_len),
        out_shape=jax.ShapeDtypeStruct((b, lp, D_POOL), BF16),
        grid=(b,),
        in_specs=[pl.BlockSpec((1, lp, D_POOL), lambda bi: (bi, 0, col_block)),
                  pl.BlockSpec((D_POOL, D_POOL), lambda bi: (0, 0)),
                  pl.BlockSpec((1, D_POOL), lambda bi: (0, 0))],
        out_specs=pl.BlockSpec((1, lp, D_POOL), lambda bi: (bi, 0, 0)),
        scratch_shapes=scratch,
        compiler_params=_cparams(1),
        name="pool_mix",
    )(z3, w_bd, scale.reshape(1, D_POOL))


def _mix_project(a_ref, f_ref, p_ref, g_ref, w_ref):
    an = _rms(a_ref[...].astype(F32), g_ref[:, 0:D_ATT]).astype(BF16)
    fn = _rms(f_ref[...].astype(F32), g_ref[:, D_ATT:D_ATT + D_FFT]).astype(BF16)
    pn = _rms(p_ref[...].astype(F32), g_ref[:, D_ATT + D_FFT:]).astype(BF16)
    acc = jnp.dot(an, w_ref[0:D_ATT, :], preferred_element_type=F32)
    acc += jnp.dot(fn, w_ref[D_ATT:D_ATT + D_FFT, :], preferred_element_type=F32)
    acc += jnp.dot(pn, w_ref[D_ATT + D_FFT:, :], preferred_element_type=F32)
    return acc


def _outproj_kernel(a_ref, f_ref, p_ref, h_ref, g_ref, w_ref, o_ref):
    o_ref[...] = h_ref[...] + _mix_project(a_ref, f_ref, p_ref, g_ref, w_ref)


def _outproj(a, f, p, h, g, w, *, tm=1024):
    n, d = h.shape
    row = lambda width: pl.BlockSpec((tm, width), lambda i: (i, 0))
    return pl.pallas_call(
        _outproj_kernel,
        out_shape=jax.ShapeDtypeStruct((n, d), F32),
        grid=(n // tm,),
        in_specs=[row(D_ATT), row(D_FFT), row(D_POOL), row(d),
                  pl.BlockSpec((1, d), lambda i: (0, 0)),
                  pl.BlockSpec((d, d), lambda i: (0, 0))],
        out_specs=row(d),
        compiler_params=_cparams(1),
        name="out_proj",
    )(a, f, p, h, g.reshape(1, d), w)


def _ffn_kernel(h_ref, g_ref, wg_ref, wu_ref, wd_ref, o_ref, xn_ref, acc_ref):
    c = pl.program_id(1)

    @pl.when(c == 0)
    def _():
        xn_ref[...] = _rms(h_ref[...], g_ref[...]).astype(BF16)

    xn = xn_ref[...]
    gate = jnp.dot(xn, wg_ref[...], preferred_element_type=F32)
    up = jnp.dot(xn, wu_ref[...], preferred_element_type=F32)
    hh = (gate * jax.nn.sigmoid(gate) * up).astype(BF16)
    part = jnp.dot(hh, wd_ref[...], preferred_element_type=F32)

    @pl.when(c == 0)
    def _():
        acc_ref[...] = part

    @pl.when(c > 0)
    def _():
        acc_ref[...] += part

    @pl.when(c == pl.num_programs(1) - 1)
    def _():
        o_ref[...] = h_ref[...] + acc_ref[...]


def _ffn(h, g, wg, wu, wd, *, tm=1024, n_chunks=2):
    n, d = h.shape
    d_ff = wg.shape[1]
    fc = d_ff // n_chunks
    return pl.pallas_call(
        _ffn_kernel,
        out_shape=jax.ShapeDtypeStruct((n, d), F32),
        grid=(n // tm, n_chunks),
        in_specs=[pl.BlockSpec((tm, d), lambda i, c: (i, 0)),
                  pl.BlockSpec((1, d), lambda i, c: (0, 0)),
                  pl.BlockSpec((d, fc), lambda i, c: (0, c)),
                  pl.BlockSpec((d, fc), lambda i, c: (0, c)),
                  pl.BlockSpec((fc, d), lambda i, c: (c, 0))],
        out_specs=pl.BlockSpec((tm, d), lambda i, c: (i, 0)),
        scratch_shapes=[pltpu.VMEM((tm, d), BF16), pltpu.VMEM((tm, d), F32)],
        compiler_params=_cparams(2),
        name="swiglu_ffn",
    )(h, g.reshape(1, d), wg, wu, wd)


MOE_TILE = 1024
EXPERT_ROWS = 512
R_GATE, R_EXPERT, R_RANK = 0, 2, 4


def _router_kernel(h_ref, g_ref, wr_ref, tri_ref, xn_ref, rec_ref, cnt_ref, *, seq_len, lp):
    t = h_ref.shape[0]
    y = _rms(h_ref[...], g_ref[...])
    xn_ref[...] = y
    logits = jnp.dot(y, wr_ref[...], preferred_element_type=F32, precision=lax.Precision.HIGHEST)
    lane = lax.broadcasted_iota(jnp.int32, (t, LANE), 1)
    lg = jnp.where(lane < N_EXPERTS, logits, -jnp.inf)
    m1 = jnp.max(lg, axis=-1, keepdims=True)
    i1 = jnp.min(jnp.where(lg == m1, lane, LANE), axis=-1, keepdims=True)
    lg2 = jnp.where(lane == i1, -jnp.inf, lg)
    m2 = jnp.max(lg2, axis=-1, keepdims=True)
    i2 = jnp.min(jnp.where(lg2 == m2, lane, LANE), axis=-1, keepdims=True)
    e2 = jnp.exp(m2 - m1)
    g1 = 1.0 / (1.0 + e2)
    g2 = e2 / (1.0 + e2)
    rowf = (pl.program_id(0) * t + lax.broadcasted_iota(jnp.int32, (t, 1), 0)).astype(F32)
    seq = jnp.floor((rowf + 0.5) * (1.0 / lp))
    live = (rowf - seq * lp) < seq_len
    first = lane == i1
    second = lane == i2
    member = jnp.where((first | second) & live, 1.0, 0.0)
    rank = jnp.dot(tri_ref[...], member.astype(BF16), preferred_element_type=F32)
    r1 = jnp.sum(jnp.where(first, rank, 0.0), axis=-1, keepdims=True)
    r2 = jnp.sum(jnp.where(second, rank, 0.0), axis=-1, keepdims=True)
    dead = jnp.logical_not(live)
    rec = jnp.zeros((t, LANE), F32)
    for ln, val in ((R_GATE, jnp.where(dead, 0.0, g1)), (R_GATE + 1, jnp.where(dead, 0.0, g2)),
                    (R_EXPERT, jnp.where(dead, -1.0, i1.astype(F32))),
                    (R_EXPERT + 1, jnp.where(dead, -1.0, i2.astype(F32))),
                    (R_RANK, r1), (R_RANK + 1, r2)):
        rec = jnp.where(lane == ln, val, rec)
    rec_ref[...] = rec
    cnt_ref[0] = jnp.broadcast_to(jnp.sum(member, axis=0, keepdims=True), (8, LANE))


def _router(h, g, w_router_pad, tri, *, seq_len, lp):
    n, d = h.shape
    t = MOE_TILE
    row = lambda width: pl.BlockSpec((t, width), lambda i: (i, 0))
    return pl.pallas_call(
        functools.partial(_router_kernel, seq_len=seq_len, lp=lp),
        out_shape=(jax.ShapeDtypeStruct((n, d), F32),
                   jax.ShapeDtypeStruct((n, LANE), F32),
                   jax.ShapeDtypeStruct((n // t, 8, LANE), F32)),
        grid=(n // t,),
        in_specs=[row(d),
                  pl.BlockSpec((1, d), lambda i: (0, 0)),
                  pl.BlockSpec((d, LANE), lambda i: (0, 0)),
                  pl.BlockSpec((t, t), lambda i: (0, 0))],
        out_specs=(row(d), row(LANE), pl.BlockSpec((1, 8, LANE), lambda i: (i, 0, 0))),
        compiler_params=_cparams(1),
        name="moe_router",
    )(h, g.reshape(1, d), w_router_pad, tri)


def _dispatch_plan(rec, counts, pad_token, *, n_row_tiles):
    n = rec.shape[0]
    n_tiles = counts.shape[0]
    cnt = counts[:, 0, :N_EXPERTS]
    before = jnp.cumsum(cnt, axis=0) - cnt
    total = jnp.sum(cnt, axis=0)
    padded = jnp.ceil(total / EXPERT_ROWS) * EXPERT_ROWS
    ends = jnp.cumsum(padded)
    start = ends - padded
    base = (start[None, :] + before)[:, None, :]
    pad_rank = np.cumsum(np.asarray(pad_token, np.int64)) - 1
    spill = n_row_tiles * EXPERT_ROWS + 2 * pad_rank.reshape(n_tiles, MOE_TILE)
    slots_out, slots_back = [], []
    for k in range(2):
        e = rec[:, R_EXPERT + k].reshape(n_tiles, MOE_TILE, 1)
        r = rec[:, R_RANK + k].reshape(n_tiles, MOE_TILE)
        hit = e == jnp.arange(N_EXPERTS, dtype=F32)[None, None, :]
        slot = (jnp.sum(jnp.where(hit, base, 0.0), axis=-1) + r).astype(jnp.int32)
        dead = e[..., 0] < 0
        slots_out.append(jnp.where(dead, jnp.asarray(spill + k, jnp.int32), slot))
        slots_back.append(jnp.where(dead, 0, slot))
    pos_out = jnp.stack(slots_out, axis=-1).reshape(n * 2)
    pos_back = jnp.stack(slots_back, axis=-1).reshape(n * 2)
    tile_row0 = jnp.arange(n_row_tiles, dtype=F32) * EXPERT_ROWS
    tile_expert = jnp.sum(tile_row0[:, None] >= ends[None, :], axis=-1)
    tile_live = (tile_row0 < ends[-1]).astype(jnp.int32)
    tile_expert = jnp.minimum(tile_expert, N_EXPERTS - 1).astype(jnp.int32)
    last_tile = jnp.where(padded > 0, ends / EXPERT_ROWS - 1, -1.0)
    unused = ends[-1] / EXPERT_ROWS + jnp.arange(N_EXPERTS, dtype=F32)
    unused = jnp.where(unused < n_row_tiles, unused, -1.0)
    zero_tiles = jnp.concatenate([last_tile, unused]).astype(jnp.int32)
    return pos_out, pos_back, tile_expert, tile_live, zero_tiles


ROW_GROUP = 8


def _row_copy(src_ref, src_row, dst_ref, dst_row, sem):
    return pltpu.make_async_copy(src_ref.at[pl.ds(src_row, 1)], dst_ref.at[pl.ds(dst_row, 1)], sem)


def _dispatch_kernel(pos_ref, zt_ref, x_ref, xs_ref, zero_buf, sem, zero_sem):
    t = x_ref.shape[0]
    base = pl.program_id(0) * t * 2

    @pl.when(pl.program_id(0) == 0)
    def _():
        zero_buf[...] = jnp.zeros(zero_buf.shape, zero_buf.dtype)

        def tile_copy(tix):
            row0 = pl.multiple_of(tix * EXPERT_ROWS, EXPERT_ROWS)
            return pltpu.make_async_copy(zero_buf, xs_ref.at[pl.ds(row0, EXPERT_ROWS)], zero_sem)

        for j in range(zt_ref.shape[0]):
            @pl.when(zt_ref[j] >= 0)
            def _():
                tile_copy(zt_ref[j]).start()

        for j in range(zt_ref.shape[0]):
            @pl.when(zt_ref[j] >= 0)
            def _():
                tile_copy(zt_ref[j]).wait()

    def issue(g, carry):
        row0 = pl.multiple_of(g * ROW_GROUP, ROW_GROUP)
        group = x_ref.at[pl.ds(row0, ROW_GROUP)]
        idx0 = base + 2 * row0
        for c in range(ROW_GROUP):
            for k in range(2):
                p = pos_ref[idx0 + (2 * c + k)]
                _row_copy(group, c, xs_ref, p, sem).start(priority=k)
        return carry

    def drain(g, carry):
        for _ in range(2 * ROW_GROUP):
            _row_copy(x_ref, 0, xs_ref, 0, sem).wait()
        return carry

    lax.fori_loop(0, t // ROW_GROUP, issue, 0)
    lax.fori_loop(0, t // ROW_GROUP, drain, 0)


def _dispatch(pos, zero_tiles, xn, n_rows):
    n, d = xn.shape
    t = MOE_TILE
    grid_spec = pltpu.PrefetchScalarGridSpec(
        num_scalar_prefetch=2,
        grid=(n // t,),
        in_specs=[pl.BlockSpec((t, d), lambda i, pos, zt: (i, 0))],
        out_specs=pl.BlockSpec(memory_space=pl.ANY),
        scratch_shapes=[pltpu.VMEM((EXPERT_ROWS, d), F32), pltpu.SemaphoreType.DMA, pltpu.SemaphoreType.DMA],
    )
    return pl.pallas_call(
        _dispatch_kernel,
        out_shape=jax.ShapeDtypeStruct((n_rows, d), F32),
        grid_spec=grid_spec,
        compiler_params=_cparams(1),
        name="moe_dispatch",
    )(pos, zero_tiles, xn)


def _expert_kernel(te_ref, live_ref, x_ref, wg_ref, wu_ref, wd_ref, o_ref, xb_ref, acc_ref):
    del te_ref
    i, c = pl.program_id(0), pl.program_id(1)
    last = pl.num_programs(1) - 1
    live = live_ref[i] > 0

    @pl.when(live)
    def _():
        @pl.when(c == 0)
        def _():
            xb_ref[...] = x_ref[...].astype(BF16)

        xb = xb_ref[...]
        gate = jnp.dot(xb, wg_ref[0, 0], preferred_element_type=F32)
        up = jnp.dot(xb, wu_ref[0, 0], preferred_element_type=F32)
        hh = (gate * jax.nn.sigmoid(gate) * up).astype(BF16)
        part = jnp.dot(hh, wd_ref[0, 0], preferred_element_type=F32)

        @pl.when(c == 0)
        def _():
            acc_ref[...] = part

        @pl.when(c > 0)
        def _():
            acc_ref[...] += part

        @pl.when(c == last)
        def _():
            o_ref[...] = acc_ref[...]

    @pl.when(jnp.logical_not(live) & (c == last))
    def _():
        o_ref[...] = jnp.zeros(o_ref.shape, o_ref.dtype)


def _experts(tile_expert, tile_live, xs, n_row_tiles, layer, wg, wu, wd, *, n_chunks=2):
    d = xs.shape[1]
    tm = EXPERT_ROWS
    n_rows = n_row_tiles * tm
    fc = wg.shape[3] // n_chunks
    chunk = lambda i, c, te, lv: jnp.where(lv[i] > 0, c, n_chunks - 1)
    grid_spec = pltpu.PrefetchScalarGridSpec(
        num_scalar_prefetch=2,
        grid=(n_rows // tm, n_chunks),
        in_specs=[pl.BlockSpec((tm, d), lambda i, c, te, lv: (i, 0)),
                  pl.BlockSpec((1, 1, d, fc), lambda i, c, te, lv: (layer, te[i], 0, chunk(i, c, te, lv))),
                  pl.BlockSpec((1, 1, d, fc), lambda i, c, te, lv: (layer, te[i], 0, chunk(i, c, te, lv))),
                  pl.BlockSpec((1, 1, fc, d), lambda i, c, te, lv: (layer, te[i], chunk(i, c, te, lv), 0))],
        out_specs=pl.BlockSpec((tm, d), lambda i, c, te, lv: (i, 0)),
        scratch_shapes=[pltpu.VMEM((tm, d), BF16), pltpu.VMEM((tm, d), F32)],
    )
    return pl.pallas_call(
        _expert_kernel,
        out_shape=jax.ShapeDtypeStruct((n_rows, d), F32),
        grid_spec=grid_spec,
        compiler_params=_cparams(2),
        name="moe_experts",
    )(tile_expert, tile_live, xs, wg, wu, wd)


def _combine_kernel(pos_ref, h_ref, rec_ref, ys_ref, o_ref, a0, a1, b0, b1, sem_a, sem_b):
    t = h_ref.shape[0]
    step = pl.program_id(0)
    n_steps = pl.num_programs(0)

    def request(tile, bufs, sem):
        base = tile * t * 2

        def issue(g, carry):
            row0 = pl.multiple_of(g * ROW_GROUP, ROW_GROUP)
            groups = [buf.at[pl.ds(row0, ROW_GROUP)] for buf in bufs]
            idx0 = base + 2 * row0
            for c in range(ROW_GROUP):
                for k in range(2):
                    p = pos_ref[idx0 + (2 * c + k)]
                    _row_copy(ys_ref, p, groups[k], c, sem).start(priority=k)
            return carry

        lax.fori_loop(0, t // ROW_GROUP, issue, 0)

    def finish(bufs, sem):
        def drain(g, carry):
            for _ in range(2 * ROW_GROUP):
                _row_copy(ys_ref, 0, bufs[0], 0, sem).wait()
            return carry

        lax.fori_loop(0, t // ROW_GROUP, drain, 0)
        rec = rec_ref[...]
        g1 = rec[:, R_GATE:R_GATE + 1]
        g2 = rec[:, R_GATE + 1:R_GATE + 2]
        o_ref[...] = h_ref[...] + g1 * bufs[0][...] + g2 * bufs[1][...]

    set_a, set_b = ((a0, a1), sem_a), ((b0, b1), sem_b)

    @pl.when(step == 0)
    def _():
        request(0, *set_a)

    for parity, mine, other in ((0, set_a, set_b), (1, set_b, set_a)):
        @pl.when(step % 2 == parity)
        def _():
            @pl.when(step + 1 < n_steps)
            def _():
                request(step + 1, *other)

            finish(*mine)


def _combine(pos, h, rec, ys):
    n, d = h.shape
    t = MOE_TILE
    grid_spec = pltpu.PrefetchScalarGridSpec(
        num_scalar_prefetch=1,
        grid=(n // t,),
        in_specs=[pl.BlockSpec((t, d), lambda i, pos: (i, 0)),
                  pl.BlockSpec((t, LANE), lambda i, pos: (i, 0)),
                  pl.BlockSpec(memory_space=pl.ANY)],
        out_specs=pl.BlockSpec((t, d), lambda i, pos: (i, 0)),
        scratch_shapes=[pltpu.VMEM((t, d), F32) for _ in range(4)]
        + [pltpu.SemaphoreType.DMA, pltpu.SemaphoreType.DMA],
    )
    return pl.pallas_call(
        _combine_kernel,
        out_shape=jax.ShapeDtypeStruct((n, d), F32),
        grid_spec=grid_spec,
        compiler_params=_cparams(1),
        name="moe_combine",
    )(pos, h, rec, ys)


def _norm_kernel(x_ref, tail_ref, g_ref, o_ref):
    tm = x_ref.shape[1]
    o_ref[0, 0:tm - N_META, :] = _rms(x_ref[0, N_META:tm, :], g_ref[...])
    o_ref[0, tm - N_META:tm, :] = _rms(tail_ref[0], g_ref[...])


def _final_norm(h3, g, *, t_len, tm=1024):
    b, lp, d = h3.shape
    assert t_len % tm == 0 and tm % N_META == 0
    return pl.pallas_call(
        _norm_kernel,
        out_shape=jax.ShapeDtypeStruct((b, t_len, d), F32),
        grid=(b, t_len // tm),
        in_specs=[pl.BlockSpec((1, tm, d), lambda bi, j: (bi, j, 0)),
                  pl.BlockSpec((1, N_META, d), lambda bi, j: (bi, (j + 1) * (tm // N_META), 0)),
                  pl.BlockSpec((1, d), lambda bi, j: (0, 0))],
        out_specs=pl.BlockSpec((1, tm, d), lambda bi, j: (bi, j, 0)),
        compiler_params=_cparams(2),
        name="final_norm",
    )(h3, h3, g.reshape(1, d))


def _fft_row_tile(lp):
    best = 16
    for tk in range(16, lp + 1, 16):
        if lp % tk == 0 and tk * lp * 2 <= 4608 * 1024:
            best = tk
    return best


def _trunk(x, meta_tokens, prm):
    b, t_len, d = x.shape
    depth = len(prm["w_in"])
    seq_len = N_META + t_len
    lp = -(-seq_len // LANE) * LANE
    n = b * lp
    rows = t_len // GRID_W
    assert t_len % GRID_W == 0 and rows >= NA_ROWS and n % MOE_TILE == 0

    meta = jnp.broadcast_to(meta_tokens.astype(x.dtype)[None], (b, N_META, d))
    h = jnp.concatenate([meta, x, jnp.zeros((b, lp - seq_len, d), x.dtype)], axis=1).reshape(n, d)

    cmat, smat = _twiddles(seq_len, lp)
    tk = _fft_row_tile(lp)
    tri = jnp.tril(jnp.ones((MOE_TILE, MOE_TILE), BF16), -1)

    for i in range(depth):
        z, ab = _norm_matmul(h, prm["norm_mix"][i], prm["w_in"][i], prm["chan_dft"])
        z3 = z.reshape(b, lp, D_IN)
        a = _attention(z3, prm["attn_bias"][i], prm["attn_mbias"][i], rows=rows, seq_len=seq_len)
        f = _fourier(cmat, smat, ab.reshape(b, lp, 2 * D_FFT), prm["w_fft"][i], tk=tk)
        p = _pool(z3, prm["w_pool"][i], prm["pool_scale"][i], seq_len=seq_len)
        h = _outproj(a.reshape(n, D_ATT), f.reshape(n, D_FFT), p.reshape(n, D_POOL), h,
                     prm["norm_groups"][i], prm["w_out"][i])
        j = i // 2
        if i % 2 == 0:
            h = _ffn(h, prm["norm_ffn"][i], prm["w_ff_gate"][j], prm["w_ff_up"][j], prm["w_ff_down"][j])
        else:
            xn, rec, counts = _router(h, prm["norm_ffn"][i], prm["w_router"][j], tri,
                                      seq_len=seq_len, lp=lp)
            n_row_tiles = -(-2 * b * seq_len // EXPERT_ROWS) + N_EXPERTS
            pad_token = (np.arange(n) % lp) >= seq_len
            pos_out, pos_back, tile_expert, tile_live, zero_tiles = _dispatch_plan(
                rec, counts, pad_token, n_row_tiles=n_row_tiles)
            n_spill = 2 * int(pad_token.sum())
            xs = _dispatch(pos_out, zero_tiles, xn, n_row_tiles * EXPERT_ROWS + n_spill)
            ys = _experts(tile_expert, tile_live, xs, n_row_tiles, j,
                          prm["w_exp_gate"], prm["w_exp_up"], prm["w_exp_down"])
            h = _combine(pos_back, h, rec, ys)
    return _final_norm(h.reshape(b, lp, d), prm["norm_final"], t_len=t_len)


def kernel(x_prompt, x_sample, meta_tokens, norm_mix, w_in, w_fft, w_pool, pool_scale, rel_bias, meta_bias,
           norm_groups, w_out, norm_ffn, w_ff_gate, w_ff_up, w_ff_down, w_router, w_exp_gate, w_exp_up,
           w_exp_down, norm_final):
    depth = w_in.shape[0]
    tabs = [_attn_bias_tables(rel_bias[i], meta_bias[i]) for i in range(depth)]
    key_scale = np.ones((D_IN,), np.float32)
    key_scale[D_ATT:2 * D_ATT] = LOG2E
    per_layer = lambda w: [w[i].astype(BF16) for i in range(w.shape[0])]
    prm = {
        "norm_mix": norm_mix, "norm_groups": norm_groups, "norm_ffn": norm_ffn, "norm_final": norm_final,
        "pool_scale": pool_scale,
        "w_in": per_layer(w_in * key_scale), "w_out": per_layer(w_out),
        "w_ff_gate": per_layer(w_ff_gate), "w_ff_up": per_layer(w_ff_up), "w_ff_down": per_layer(w_ff_down),
        "w_exp_gate": w_exp_gate.astype(BF16), "w_exp_up": w_exp_up.astype(BF16),
        "w_exp_down": w_exp_down.astype(BF16),
        "w_router": jnp.pad(w_router, ((0, 0), (0, 0), (0, LANE - N_EXPERTS))),
        "w_fft": jnp.stack([_block_diag(w_fft[i]) for i in range(depth)]).astype(BF16),
        "w_pool": jnp.stack([_block_diag(w_pool[i]) for i in range(depth)]).astype(BF16),
        "chan_dft": _channel_dft_matrix(),
        "attn_bias": [tb[0] for tb in tabs], "attn_mbias": [tb[1] for tb in tabs],
    }
    y_prompt = _trunk(x_prompt, meta_tokens, prm)
    y_sample = _trunk(x_sample, meta_tokens, prm)
    return (y_prompt, y_sample)
```

```python
import functools
import math

import numpy as np
import jax
import jax.numpy as jnp
from jax import lax
from jax.experimental import pallas as pl
from jax.experimental.pallas import tpu as pltpu

D_MODEL = 1024
N_META = 16
GRID_W = 64
D_HEAD = 64
D_ATT = 512
N_ATT_HEADS = 8
D_FFT = 256
N_FFT_HEADS = 4
D_FFT_HEAD = 64
D_POOL = 256
POOL_WINDOWS = (2, 4, 8, 16)
D_POOL_GROUP = 64
D_IN = 2048
NA_ROWS = 8
NA_COLS = 16
N_EXPERTS = 8
EPS = 1e-6

LANE = 128
NEG_BIG = -1e30
LOG2E = math.log2(math.e)
VMEM_LIMIT = 56 * 1024 * 1024

F32 = jnp.float32
BF16 = jnp.bfloat16


def _cparams(n_axes, vmem=VMEM_LIMIT):
    return pltpu.CompilerParams(dimension_semantics=("arbitrary",) * n_axes,
                                vmem_limit_bytes=vmem)


def _rms(x, g):
    return x * lax.rsqrt(jnp.mean(x * x, axis=-1, keepdims=True) + EPS) * g


def _norm_matmul_kernel(x_ref, g_ref, w_ref, cdft_ref, o_ref, ab_ref, xn_ref, *, tn, fft_col):
    xn_ref[...] = _rms(x_ref[...], g_ref[...]).astype(BF16)
    for n0 in range(0, o_ref.shape[1], tn):
        zc = jnp.dot(xn_ref[...], w_ref[:, n0:n0 + tn], preferred_element_type=F32).astype(o_ref.dtype)
        o_ref[:, n0:n0 + tn] = zc
        if n0 <= fft_col and fft_col + D_FFT <= n0 + tn:
            ab_ref[...] = jnp.dot(zc[:, fft_col - n0:fft_col - n0 + D_FFT], cdft_ref[...],
                                  preferred_element_type=F32).astype(ab_ref.dtype)


def _norm_matmul(x, g, w, cdft, *, tm=1024, tn=512):
    n, d = x.shape
    n_out = w.shape[1]
    fft_col = 3 * D_ATT
    assert fft_col % tn + D_FFT <= tn
    return pl.pallas_call(
        functools.partial(_norm_matmul_kernel, tn=tn, fft_col=fft_col),
        out_shape=(jax.ShapeDtypeStruct((n, n_out), BF16), jax.ShapeDtypeStruct((n, 2 * D_FFT), BF16)),
        grid=(n // tm,),
        in_specs=[pl.BlockSpec((tm, d), lambda i: (i, 0)),
                  pl.BlockSpec((1, d), lambda i: (0, 0)),
                  pl.BlockSpec((d, n_out), lambda i: (0, 0)),
                  pl.BlockSpec((D_FFT, 2 * D_FFT), lambda i: (0, 0))],
        out_specs=(pl.BlockSpec((tm, n_out), lambda i: (i, 0)),
                   pl.BlockSpec((tm, 2 * D_FFT), lambda i: (i, 0))),
        scratch_shapes=[pltpu.VMEM((tm, d), BF16)],
        compiler_params=_cparams(1),
        name="norm_matmul",
    )(x, g.reshape(1, d), w, cdft)


def _attn_kernel(q_ref, k_ref, v_ref, bias_ref, mb_ref, o_ref, smq_ref, s_a, s_b, s_c, s_d,
                 p_a, p_b, p_c, p_d, l_a, l_b, l_c, l_d, *, rows, seq_len):
    lp = o_ref.shape[1]
    n_win = NA_ROWS * GRID_W
    lane = lax.broadcasted_iota(jnp.int32, (1, LANE), 1)
    head0 = lane < D_HEAD
    nt = (((1,), (1,)), ((), ()))

    km16 = k_ref[0, 0:N_META, :]
    zero16 = jnp.zeros_like(km16)
    km = jnp.concatenate([jnp.where(head0, km16, zero16), jnp.where(head0, zero16, km16),
                          k_ref[0, 2 * N_META:LANE, :]], axis=0)
    vm = jnp.concatenate([v_ref[0, 0:N_META, :], v_ref[0, 0:N_META, :], v_ref[0, 2 * N_META:LANE, :]], axis=0)
    mb = mb_ref[0]

    def stack(q):
        q = q * jnp.asarray(D_HEAD ** -0.5, q.dtype)
        zero = jnp.zeros_like(q)
        return jnp.concatenate([jnp.where(head0, q, zero), jnp.where(head0, zero, q)], axis=0)

    def unstack(o, n):
        return jnp.where(head0, o[0:n], o[n:2 * n])

    q_all = q_ref[0] * jnp.asarray(D_HEAD ** -0.5, q_ref.dtype)
    smq_ref[...] = lax.dot_general(q_all, km, nt, preferred_element_type=F32)

    def meta_scores(q0, n, bias):
        blk = smq_ref[pl.ds(q0, n), :]
        return jnp.concatenate([blk, blk], axis=0) + bias

    sm = meta_scores(0, N_META, jnp.concatenate([mb[0:N_META], mb[GRID_W:GRID_W + N_META]], axis=0))
    pm = jnp.exp2(sm - jnp.max(sm, axis=-1, keepdims=True))
    om = jnp.dot(pm.astype(BF16), vm, preferred_element_type=F32)
    om = om / jnp.sum(pm, axis=-1, keepdims=True)
    o_ref[0, 0:N_META, :] = unstack(om, N_META).astype(o_ref.dtype)

    def window(t):
        t = jnp.minimum(t, rows - 1)
        rs = jnp.clip(t - NA_ROWS // 2, 0, rows - NA_ROWS)
        q0 = pl.multiple_of(N_META + t * GRID_W, 16)
        k0 = pl.multiple_of(N_META + rs * GRID_W, 16)
        return q0, k0, t - rs

    def scores(t, s_ref):
        q0, k0, off = window(t)
        qs = stack(q_ref[0, pl.ds(q0, GRID_W), :])
        kw = k_ref[0, pl.ds(k0, n_win), :]
        s_ref[...] = lax.dot_general(qs, kw, nt, preferred_element_type=F32) + bias_ref[0, off]

    dyn_zero = pl.multiple_of(jnp.minimum(pl.program_id(0), 0), LANE)

    def lane_tiles(x):
        return [x[:, i:i + LANE] for i in range(0, x.shape[1], LANE)]

    def softmax(t, s_ref, p_ref, l_ref):
        q0, _, _ = window(t)
        s = s_ref[pl.ds(dyn_zero, LANE), :]
        sm = meta_scores(q0, GRID_W, mb)
        m = jnp.max(functools.reduce(jnp.maximum, lane_tiles(s) + [sm]), axis=-1, keepdims=True)
        p = jnp.exp2(s - m)
        pm = jnp.exp2(sm - m)
        l_ref[...] = jnp.sum(functools.reduce(jnp.add, lane_tiles(p) + [pm]), axis=-1, keepdims=True)
        p_ref[:, 0:n_win] = p.astype(BF16)
        p_ref[:, n_win:] = pm.astype(BF16)

    def values(t, p_ref, l_ref):
        q0, k0, _ = window(t)
        vw = v_ref[0, pl.ds(k0, n_win), :]
        o = (jnp.dot(p_ref[:, 0:n_win], vw, preferred_element_type=F32)
             + jnp.dot(p_ref[:, n_win:], vm, preferred_element_type=F32))
        o = o / l_ref[...]
        o_ref[0, pl.ds(q0, GRID_W), :] = unstack(o, GRID_W).astype(o_ref.dtype)

    def step(t, s_in, s_out, pl_in, pl_out):
        scores(t + 4, s_out[0])
        scores(t + 5, s_out[1])
        values(t, *pl_in[0])
        values(t + 1, *pl_in[1])
        softmax(t + 2, s_in[0], *pl_out[0])
        softmax(t + 3, s_in[1], *pl_out[1])

    set_0 = ((p_a, l_a), (p_b, l_b))
    set_1 = ((p_c, l_c), (p_d, l_d))
    scores(0, s_c)
    scores(1, s_d)
    softmax(0, s_c, *set_0[0])
    softmax(1, s_d, *set_0[1])
    scores(2, s_a)
    scores(3, s_b)

    def quad_body(u, carry):
        t = 4 * u
        step(t, (s_a, s_b), (s_c, s_d), set_0, set_1)
        step(t + 2, (s_c, s_d), (s_a, s_b), set_1, set_0)
        return carry

    lax.fori_loop(0, rows // 4, quad_body, 0, unroll=4)
    if lp > seq_len:
        o_ref[0, seq_len:lp, :] = jnp.zeros((lp - seq_len, LANE), o_ref.dtype)


def _attention(z3, bias_tab, mb_tab, *, rows, seq_len):
    b, lp, _ = z3.shape
    assert rows % 4 == 0
    n_pairs = N_ATT_HEADS // 2
    n_keys = NA_ROWS * GRID_W
    blk = lambda off: pl.BlockSpec((1, lp, LANE), lambda hp, bi: (bi, 0, off + hp))
    scratch = ([pltpu.VMEM((lp, LANE), F32)]
               + [pltpu.VMEM((LANE, n_keys), F32) for _ in range(4)]
               + [pltpu.VMEM((LANE, n_keys + LANE), BF16) for _ in range(4)]
               + [pltpu.VMEM((LANE, 1), F32) for _ in range(4)])
    return pl.pallas_call(
        functools.partial(_attn_kernel, rows=rows, seq_len=seq_len),
        out_shape=jax.ShapeDtypeStruct((b, lp, D_ATT), BF16),
        grid=(n_pairs, b),
        in_specs=[blk(0), blk(n_pairs), blk(2 * n_pairs),
                  pl.BlockSpec((1, NA_ROWS, LANE, n_keys), lambda hp, bi: (hp, 0, 0, 0)),
                  pl.BlockSpec((1, LANE, LANE), lambda hp, bi: (hp, 0, 0))],
        out_specs=pl.BlockSpec((1, lp, LANE), lambda hp, bi: (bi, 0, hp)),
        scratch_shapes=scratch,
        compiler_params=_cparams(2),
        name="nbr_attention",
    )(z3, z3, z3, bias_tab, mb_tab)


def _attn_bias_tables(rel_bias, meta_bias):
    h, n_dr, n_dc = rel_bias.shape
    c = np.arange(GRID_W)[:, None]
    j = np.arange(GRID_W)[None, :]
    cs = np.clip(c - NA_COLS // 2, 0, GRID_W - NA_COLS)
    valid = (j >= cs) & (j < cs + NA_COLS)
    ext = jnp.zeros((h, n_dr, 2 * GRID_W), F32)
    ext = lax.dynamic_update_slice(ext, rel_bias.astype(F32), (0, 0, GRID_W - NA_COLS))
    skew = jnp.tile(ext, (1, 1, GRID_W))[:, :, :GRID_W * (2 * GRID_W - 1)]
    toep = skew.reshape(h, n_dr, GRID_W, 2 * GRID_W - 1)[..., GRID_W - 1:]
    toep = jnp.where(valid[None, None], toep * LOG2E, NEG_BIG)
    t = jnp.stack([toep[:, NA_ROWS - 1 - oi:2 * NA_ROWS - 1 - oi] for oi in range(NA_ROWS)], axis=1)
    t = t.transpose(0, 1, 3, 2, 4)
    t = t.reshape(h // 2, 2, NA_ROWS, GRID_W, NA_ROWS * GRID_W)
    t = t.transpose(0, 2, 1, 3, 4).reshape(h // 2, NA_ROWS, 2 * GRID_W, NA_ROWS * GRID_W)
    mbp = (meta_bias.astype(F32) * LOG2E).reshape(h // 2, 2, N_META)
    neg = jnp.full((h // 2, N_META), NEG_BIG, F32)
    mb0 = jnp.concatenate([mbp[:, 0], neg], axis=-1)
    mb1 = jnp.concatenate([neg, mbp[:, 1]], axis=-1)
    mb = jnp.stack([mb0, mb1], axis=1)
    mb = jnp.pad(mb, ((0, 0), (0, 0), (0, LANE - 2 * N_META)), constant_values=NEG_BIG)
    mb = jnp.broadcast_to(mb[:, :, None, :], (h // 2, 2, GRID_W, LANE))
    return t, mb.reshape(h // 2, 2 * GRID_W, LANE)


def _fft_kernel(c_ref, s_ref, ab_ref, w_ref, o_ref):
    a = ab_ref[0, :, 0:D_FFT]
    b = ab_ref[0, :, D_FFT:2 * D_FFT]
    f = (jnp.dot(c_ref[...], a, preferred_element_type=F32)
         + jnp.dot(s_ref[...], b, preferred_element_type=F32))
    o_ref[0] = jnp.dot(f.astype(BF16), w_ref[...], preferred_element_type=F32).astype(o_ref.dtype)


def _fourier(cmat, smat, ab3, w_bd, *, tk):
    b, lp, _ = ab3.shape
    return pl.pallas_call(
        _fft_kernel,
        out_shape=jax.ShapeDtypeStruct((b, lp, D_FFT), BF16),
        grid=(lp // tk, b),
        in_specs=[pl.BlockSpec((tk, lp), lambda j, bi: (j, 0)),
                  pl.BlockSpec((tk, lp), lambda j, bi: (j, 0)),
                  pl.BlockSpec((1, lp, 2 * D_FFT), lambda j, bi: (bi, 0, 0)),
                  pl.BlockSpec((D_FFT, D_FFT), lambda j, bi: (0, 0))],
        out_specs=pl.BlockSpec((1, tk, D_FFT), lambda j, bi: (bi, j, 0)),
        compiler_params=_cparams(2),
        name="fourier_mix",
    )(cmat, smat, ab3, w_bd)


def _twiddles(seq_len, lp):
    theta = 2.0 * math.pi / seq_len
    k = jnp.arange(lp, dtype=jnp.int32)[:, None]
    t1 = (GRID_W * jnp.arange(lp // GRID_W, dtype=jnp.int32))[None, :]
    t0 = jnp.arange(GRID_W, dtype=jnp.int32)[None, :]
    ang_a = ((k * t1) % seq_len).astype(F32) * theta
    ang_b = ((k * t0) % seq_len).astype(F32) * theta
    ca, sa = jnp.cos(ang_a)[:, :, None], jnp.sin(ang_a)[:, :, None]
    cb, sb = jnp.cos(ang_b)[:, None, :], jnp.sin(ang_b)[:, None, :]
    cmat = (ca * cb - sa * sb).reshape(lp, lp)
    smat = (sa * cb + ca * sb).reshape(lp, lp)
    idx = jnp.arange(lp)
    valid = (idx[:, None] < seq_len) & (idx[None, :] < seq_len)
    scale = seq_len ** -0.5
    cmat = jnp.where(valid, cmat * scale, 0.0).astype(BF16)
    smat = jnp.where(valid, -smat * scale, 0.0).astype(BF16)
    return cmat, smat


def _channel_dft_matrix():
    c = np.arange(D_FFT_HEAD)
    ang = 2.0 * np.pi * ((c[:, None] * c[None, :]) % D_FFT_HEAD) / D_FFT_HEAD
    cc = np.cos(ang) / math.sqrt(D_FFT_HEAD)
    sc = np.sin(ang) / math.sqrt(D_FFT_HEAD)
    out = np.zeros((D_FFT, 2 * D_FFT), np.float32)
    for g in range(N_FFT_HEADS):
        sl = slice(g * D_FFT_HEAD, (g + 1) * D_FFT_HEAD)
        out[sl, sl] = cc
        out[sl, D_FFT + g * D_FFT_HEAD:D_FFT + (g + 1) * D_FFT_HEAD] = sc
    return jnp.asarray(out, BF16)


def _block_diag(w):
    g, c, e = w.shape
    out = jnp.zeros((g * c, g * e), w.dtype)
    for i in range(g):
        out = lax.dynamic_update_slice(out, w[i], (i * c, i * e))
    return out


POOL_PAD = 16


def _pool_kernel(u_ref, w_ref, sc_ref, o_ref, s0, s1, s2, s3, s4, *, seq_len):
    lp = o_ref.shape[1]
    r_tot = lp + 2 * POOL_PAD
    lo, hi = 8, r_tot - 8
    row = lax.broadcasted_iota(jnp.int32, (lp, 1), 0)
    live = row < seq_len
    x = jnp.where(live, u_ref[0].astype(F32), 0.0)

    zeros_pad = jnp.zeros((POOL_PAD, D_POOL), F32)
    s0[0:POOL_PAD, :] = zeros_pad
    s0[POOL_PAD:POOL_PAD + lp, :] = x
    s0[POOL_PAD + lp:r_tot, :] = zeros_pad
    for s in (s1, s2, s3, s4):
        s[0:POOL_PAD, :] = zeros_pad
        s[POOL_PAD + lp:r_tot, :] = zeros_pad
    s1[lo:hi, :] = s0[lo - 1:hi - 1, :] + s0[lo:hi, :]
    s2[lo:hi, :] = s1[lo - 1:hi - 1, :] + s1[lo + 1:hi + 1, :]
    s3[lo:hi, :] = s2[lo - 2:hi - 2, :] + s2[lo + 2:hi + 2, :]
    s4[lo:hi, :] = s3[lo - 4:hi - 4, :] + s3[lo + 4:hi + 4, :]

    lane = lax.broadcasted_iota(jnp.int32, (1, D_POOL), 1)
    group = jnp.right_shift(lane, 6)
    half = jnp.where(group == 0, 1, jnp.where(group == 1, 2, jnp.where(group == 2, 4, 8)))
    cnt = jnp.minimum(row + half, seq_len) - jnp.maximum(row - half, 0)
    cnt = jnp.maximum(cnt, 1).astype(F32)
    sl = slice(POOL_PAD, POOL_PAD + lp)
    wsum = jnp.where(group == 0, s1[sl, :],
                     jnp.where(group == 1, s2[sl, :], jnp.where(group == 2, s3[sl, :], s4[sl, :])))
    p = jnp.where(live, wsum / cnt - x, 0.0)
    y = jnp.dot(p.astype(BF16), w_ref[...], preferred_element_type=F32) * sc_ref[...]
    o_ref[0] = y.astype(o_ref.dtype)


def _pool(z3, w_bd, scale, *, seq_len):
    b, lp, d_in = z3.shape
    col_block = (d_in - D_POOL) // D_POOL
    scratch = [pltpu.VMEM((lp + 2 * POOL_PAD, D_POOL), F32) for _ in range(5)]
    return pl.pallas_call(
        functools.partial(_pool_kernel, seq_len=seq_len),
        out_shape=jax.ShapeDtypeStruct((b, lp, D_POOL), BF16),
        grid=(b,),
        in_specs=[pl.BlockSpec((1, lp, D_POOL), lambda bi: (bi, 0, col_block)),
                  pl.BlockSpec((D_POOL, D_POOL), lambda bi: (0, 0)),
                  pl.BlockSpec((1, D_POOL), lambda bi: (0, 0))],
        out_specs=pl.BlockSpec((1, lp, D_POOL), lambda bi: (bi, 0, 0)),
        scratch_shapes=scratch,
        compiler_params=_cparams(1),
        name="pool_mix",
    )(z3, w_bd, scale.reshape(1, D_POOL))


def _mix_project(a_ref, f_ref, p_ref, g_ref, w_ref):
    an = _rms(a_ref[...].astype(F32), g_ref[:, 0:D_ATT]).astype(BF16)
    fn = _rms(f_ref[...].astype(F32), g_ref[:, D_ATT:D_ATT + D_FFT]).astype(BF16)
    pn = _rms(p_ref[...].astype(F32), g_ref[:, D_ATT + D_FFT:]).astype(BF16)
    acc = jnp.dot(an, w_ref[0:D_ATT, :], preferred_element_type=F32)
    acc += jnp.dot(fn, w_ref[D_ATT:D_ATT + D_FFT, :], preferred_element_type=F32)
    acc += jnp.dot(pn, w_ref[D_ATT + D_FFT:, :], preferred_element_type=F32)
    return acc


def _outproj_kernel(a_ref, f_ref, p_ref, h_ref, g_ref, w_ref, o_ref):
    o_ref[...] = h_ref[...] + _mix_project(a_ref, f_ref, p_ref, g_ref, w_ref)


def _outproj(a, f, p, h, g, w, *, tm=1024):
    n, d = h.shape
    row = lambda width: pl.BlockSpec((tm, width), lambda i: (i, 0))
    return pl.pallas_call(
        _outproj_kernel,
        out_shape=jax.ShapeDtypeStruct((n, d), F32),
        grid=(n // tm,),
        in_specs=[row(D_ATT), row(D_FFT), row(D_POOL), row(d),
                  pl.BlockSpec((1, d), lambda i: (0, 0)),
                  pl.BlockSpec((d, d), lambda i: (0, 0))],
        out_specs=row(d),
        compiler_params=_cparams(1),
        name="out_proj",
    )(a, f, p, h, g.reshape(1, d), w)


def _ffn_kernel(h_ref, g_ref, wg_ref, wu_ref, wd_ref, o_ref, xn_ref, acc_ref):
    c = pl.program_id(1)

    @pl.when(c == 0)
    def _():
        xn_ref[...] = _rms(h_ref[...], g_ref[...]).astype(BF16)

    xn = xn_ref[...]
    gate = jnp.dot(xn, wg_ref[...], preferred_element_type=F32)
    up = jnp.dot(xn, wu_ref[...], preferred_element_type=F32)
    hh = (gate * jax.nn.sigmoid(gate) * up).astype(BF16)
    part = jnp.dot(hh, wd_ref[...], preferred_element_type=F32)

    @pl.when(c == 0)
    def _():
        acc_ref[...] = part

    @pl.when(c > 0)
    def _():
        acc_ref[...] += part

    @pl.when(c == pl.num_programs(1) - 1)
    def _():
        o_ref[...] = h_ref[...] + acc_ref[...]


def _ffn(h, g, wg, wu, wd, *, tm=1024, n_chunks=2):
    n, d = h.shape
    d_ff = wg.shape[1]
    fc = d_ff // n_chunks
    return pl.pallas_call(
        _ffn_kernel,
        out_shape=jax.ShapeDtypeStruct((n, d), F32),
        grid=(n // tm, n_chunks),
        in_specs=[pl.BlockSpec((tm, d), lambda i, c: (i, 0)),
                  pl.BlockSpec((1, d), lambda i, c: (0, 0)),
                  pl.BlockSpec((d, fc), lambda i, c: (0, c)),
                  pl.BlockSpec((d, fc), lambda i, c: (0, c)),
                  pl.BlockSpec((fc, d), lambda i, c: (c, 0))],
        out_specs=pl.BlockSpec((tm, d), lambda i, c: (i, 0)),
        scratch_shapes=[pltpu.VMEM((tm, d), BF16), pltpu.VMEM((tm, d), F32)],
        compiler_params=_cparams(2),
        name="swiglu_ffn",
    )(h, g.reshape(1, d), wg, wu, wd)


MOE_TILE = 1024
EXPERT_ROWS = 512
R_GATE, R_EXPERT, R_RANK = 0, 2, 4


def _router_kernel(h_ref, g_ref, wr_ref, tri_ref, xn_ref, rec_ref, cnt_ref, *, seq_len, lp):
    t = h_ref.shape[0]
    y = _rms(h_ref[...], g_ref[...])
    xn_ref[...] = y
    logits = jnp.dot(y, wr_ref[...], preferred_element_type=F32, precision=lax.Precision.HIGHEST)
    lane = lax.broadcasted_iota(jnp.int32, (t, LANE), 1)
    lg = jnp.where(lane < N_EXPERTS, logits, -jnp.inf)
    m1 = jnp.max(lg, axis=-1, keepdims=True)
    i1 = jnp.min(jnp.where(lg == m1, lane, LANE), axis=-1, keepdims=True)
    lg2 = jnp.where(lane == i1, -jnp.inf, lg)
    m2 = jnp.max(lg2, axis=-1, keepdims=True)
    i2 = jnp.min(jnp.where(lg2 == m2, lane, LANE), axis=-1, keepdims=True)
    e2 = jnp.exp(m2 - m1)
    g1 = 1.0 / (1.0 + e2)
    g2 = e2 / (1.0 + e2)
    rowf = (pl.program_id(0) * t + lax.broadcasted_iota(jnp.int32, (t, 1), 0)).astype(F32)
    seq = jnp.floor((rowf + 0.5) * (1.0 / lp))
    live = (rowf - seq * lp) < seq_len
    first = lane == i1
    second = lane == i2
    member = jnp.where((first | second) & live, 1.0, 0.0)
    rank = jnp.dot(tri_ref[...], member.astype(BF16), preferred_element_type=F32)
    r1 = jnp.sum(jnp.where(first, rank, 0.0), axis=-1, keepdims=True)
    r2 = jnp.sum(jnp.where(second, rank, 0.0), axis=-1, keepdims=True)
    dead = jnp.logical_not(live)
    rec = jnp.zeros((t, LANE), F32)
    for ln, val in ((R_GATE, jnp.where(dead, 0.0, g1)), (R_GATE + 1, jnp.where(dead, 0.0, g2)),
                    (R_EXPERT, jnp.where(dead, -1.0, i1.astype(F32))),
                    (R_EXPERT + 1, jnp.where(dead, -1.0, i2.astype(F32))),
                    (R_RANK, r1), (R_RANK + 1, r2)):
        rec = jnp.where(lane == ln, val, rec)
    rec_ref[...] = rec
    cnt_ref[0] = jnp.broadcast_to(jnp.sum(member, axis=0, keepdims=True), (8, LANE))


def _router(h, g, w_router_pad, tri, *, seq_len, lp):
    n, d = h.shape
    t = MOE_TILE
    row = lambda width: pl.BlockSpec((t, width), lambda i: (i, 0))
    return pl.pallas_call(
        functools.partial(_router_kernel, seq_len=seq_len, lp=lp),
        out_shape=(jax.ShapeDtypeStruct((n, d), F32),
                   jax.ShapeDtypeStruct((n, LANE), F32),
                   jax.ShapeDtypeStruct((n // t, 8, LANE), F32)),
        grid=(n // t,),
        in_specs=[row(d),
                  pl.BlockSpec((1, d), lambda i: (0, 0)),
                  pl.BlockSpec((d, LANE), lambda i: (0, 0)),
                  pl.BlockSpec((t, t), lambda i: (0, 0))],
        out_specs=(row(d), row(LANE), pl.BlockSpec((1, 8, LANE), lambda i: (i, 0, 0))),
        compiler_params=_cparams(1),
        name="moe_router",
    )(h, g.reshape(1, d), w_router_pad, tri)


def _dispatch_plan(rec, counts, pad_token, *, n_row_tiles):
    n = rec.shape[0]
    n_tiles = counts.shape[0]
    cnt = counts[:, 0, :N_EXPERTS]
    before = jnp.cumsum(cnt, axis=0) - cnt
    total = jnp.sum(cnt, axis=0)
    padded = jnp.ceil(total / EXPERT_ROWS) * EXPERT_ROWS
    ends = jnp.cumsum(padded)
    start = ends - padded
    base = (start[None, :] + before)[:, None, :]
    pad_rank = np.cumsum(np.asarray(pad_token, np.int64)) - 1
    spill = n_row_tiles * EXPERT_ROWS + 2 * pad_rank.reshape(n_tiles, MOE_TILE)
    slots_out, slots_back = [], []
    for k in range(2):
        e = rec[:, R_EXPERT + k].reshape(n_tiles, MOE_TILE, 1)
        r = rec[:, R_RANK + k].reshape(n_tiles, MOE_TILE)
        hit = e == jnp.arange(N_EXPERTS, dtype=F32)[None, None, :]
        slot = (jnp.sum(jnp.where(hit, base, 0.0), axis=-1) + r).astype(jnp.int32)
        dead = e[..., 0] < 0
        slots_out.append(jnp.where(dead, jnp.asarray(spill + k, jnp.int32), slot))
        slots_back.append(jnp.where(dead, 0, slot))
    pos_out = jnp.stack(slots_out, axis=-1).reshape(n * 2)
    pos_back = jnp.stack(slots_back, axis=-1).reshape(n * 2)
    tile_row0 = jnp.arange(n_row_tiles, dtype=F32) * EXPERT_ROWS
    tile_expert = jnp.sum(tile_row0[:, None] >= ends[None, :], axis=-1)
    tile_live = (tile_row0 < ends[-1]).astype(jnp.int32)
    tile_expert = jnp.minimum(tile_expert, N_EXPERTS - 1).astype(jnp.int32)
    last_tile = jnp.where(padded > 0, ends / EXPERT_ROWS - 1, -1.0)
    unused = ends[-1] / EXPERT_ROWS + jnp.arange(N_EXPERTS, dtype=F32)
    unused = jnp.where(unused < n_row_tiles, unused, -1.0)
    zero_tiles = jnp.concatenate([last_tile, unused]).astype(jnp.int32)
    return pos_out, pos_back, tile_expert, tile_live, zero_tiles


ROW_GROUP = 8


def _row_copy(src_ref, src_row, dst_ref, dst_row, sem):
    return pltpu.make_async_copy(src_ref.at[pl.ds(src_row, 1)], dst_ref.at[pl.ds(dst_row, 1)], sem)


def _dispatch_kernel(pos_ref, zt_ref, x_ref, xs_ref, zero_buf, sem, zero_sem):
    t = x_ref.shape[0]
    base = pl.program_id(0) * t * 2

    @pl.when(pl.program_id(0) == 0)
    def _():
        zero_buf[...] = jnp.zeros(zero_buf.shape, zero_buf.dtype)

        def tile_copy(tix):
            row0 = pl.multiple_of(tix * EXPERT_ROWS, EXPERT_ROWS)
            return pltpu.make_async_copy(zero_buf, xs_ref.at[pl.ds(row0, EXPERT_ROWS)], zero_sem)

        for j in range(zt_ref.shape[0]):
            @pl.when(zt_ref[j] >= 0)
            def _():
                tile_copy(zt_ref[j]).start()

        for j in range(zt_ref.shape[0]):
            @pl.when(zt_ref[j] >= 0)
            def _():
                tile_copy(zt_ref[j]).wait()

    def issue(g, carry):
        row0 = pl.multiple_of(g * ROW_GROUP, ROW_GROUP)
        group = x_ref.at[pl.ds(row0, ROW_GROUP)]
        idx0 = base + 2 * row0
        for c in range(ROW_GROUP):
            for k in range(2):
                p = pos_ref[idx0 + (2 * c + k)]
                _row_copy(group, c, xs_ref, p, sem).start(priority=k)
        return carry

    def drain(g, carry):
        for _ in range(2 * ROW_GROUP):
            _row_copy(x_ref, 0, xs_ref, 0, sem).wait()
        return carry

    lax.fori_loop(0, t // ROW_GROUP, issue, 0)
    lax.fori_loop(0, t // ROW_GROUP, drain, 0)


def _dispatch(pos, zero_tiles, xn, n_rows):
    n, d = xn.shape
    t = MOE_TILE
    grid_spec = pltpu.PrefetchScalarGridSpec(
        num_scalar_prefetch=2,
        grid=(n // t,),
        in_specs=[pl.BlockSpec((t, d), lambda i, pos, zt: (i, 0))],
        out_specs=pl.BlockSpec(memory_space=pl.ANY),
        scratch_shapes=[pltpu.VMEM((EXPERT_ROWS, d), F32), pltpu.SemaphoreType.DMA, pltpu.SemaphoreType.DMA],
    )
    return pl.pallas_call(
        _dispatch_kernel,
        out_shape=jax.ShapeDtypeStruct((n_rows, d), F32),
        grid_spec=grid_spec,
        compiler_params=_cparams(1),
        name="moe_dispatch",
    )(pos, zero_tiles, xn)


def _expert_kernel(te_ref, live_ref, x_ref, wg_ref, wu_ref, wd_ref, o_ref, xb_ref, acc_ref):
    del te_ref
    i, c = pl.program_id(0), pl.program_id(1)
    last = pl.num_programs(1) - 1
    live = live_ref[i] > 0

    @pl.when(live)
    def _():
        @pl.when(c == 0)
        def _():
            xb_ref[...] = x_ref[...].astype(BF16)

        xb = xb_ref[...]
        gate = jnp.dot(xb, wg_ref[0, 0], preferred_element_type=F32)
        up = jnp.dot(xb, wu_ref[0, 0], preferred_element_type=F32)
        hh = (gate * jax.nn.sigmoid(gate) * up).astype(BF16)
        part = jnp.dot(hh, wd_ref[0, 0], preferred_element_type=F32)

        @pl.when(c == 0)
        def _():
            acc_ref[...] = part

        @pl.when(c > 0)
        def _():
            acc_ref[...] += part

        @pl.when(c == last)
        def _():
            o_ref[...] = acc_ref[...]

    @pl.when(jnp.logical_not(live) & (c == last))
    def _():
        o_ref[...] = jnp.zeros(o_ref.shape, o_ref.dtype)


def _experts(tile_expert, tile_live, xs, n_row_tiles, layer, wg, wu, wd, *, n_chunks=2):
    d = xs.shape[1]
    tm = EXPERT_ROWS
    n_rows = n_row_tiles * tm
    fc = wg.shape[3] // n_chunks
    chunk = lambda i, c, te, lv: jnp.where(lv[i] > 0, c, n_chunks - 1)
    grid_spec = pltpu.PrefetchScalarGridSpec(
        num_scalar_prefetch=2,
        grid=(n_rows // tm, n_chunks),
        in_specs=[pl.BlockSpec((tm, d), lambda i, c, te, lv: (i, 0)),
                  pl.BlockSpec((1, 1, d, fc), lambda i, c, te, lv: (layer, te[i], 0, chunk(i, c, te, lv))),
                  pl.BlockSpec((1, 1, d, fc), lambda i, c, te, lv: (layer, te[i], 0, chunk(i, c, te, lv))),
                  pl.BlockSpec((1, 1, fc, d), lambda i, c, te, lv: (layer, te[i], chunk(i, c, te, lv), 0))],
        out_specs=pl.BlockSpec((tm, d), lambda i, c, te, lv: (i, 0)),
        scratch_shapes=[pltpu.VMEM((tm, d), BF16), pltpu.VMEM((tm, d), F32)],
    )
    return pl.pallas_call(
        _expert_kernel,
        out_shape=jax.ShapeDtypeStruct((n_rows, d), F32),
        grid_spec=grid_spec,
        compiler_params=_cparams(2),
        name="moe_experts",
    )(tile_expert, tile_live, xs, wg, wu, wd)


def _combine_kernel(pos_ref, h_ref, rec_ref, ys_ref, o_ref, a0, a1, b0, b1, sem_a, sem_b):
    t = h_ref.shape[0]
    step = pl.program_id(0)
    n_steps = pl.num_programs(0)

    def request(tile, bufs, sem):
        base = tile * t * 2

        def issue(g, carry):
            row0 = pl.multiple_of(g * ROW_GROUP, ROW_GROUP)
            groups = [buf.at[pl.ds(row0, ROW_GROUP)] for buf in bufs]
            idx0 = base + 2 * row0
            for c in range(ROW_GROUP):
                for k in range(2):
                    p = pos_ref[idx0 + (2 * c + k)]
                    _row_copy(ys_ref, p, groups[k], c, sem).start(priority=k)
            return carry

        lax.fori_loop(0, t // ROW_GROUP, issue, 0)

    def finish(bufs, sem):
        def drain(g, carry):
            for _ in range(2 * ROW_GROUP):
                _row_copy(ys_ref, 0, bufs[0], 0, sem).wait()
            return carry

        lax.fori_loop(0, t // ROW_GROUP, drain, 0)
        rec = rec_ref[...]
        g1 = rec[:, R_GATE:R_GATE + 1]
        g2 = rec[:, R_GATE + 1:R_GATE + 2]
        o_ref[...] = h_ref[...] + g1 * bufs[0][...] + g2 * bufs[1][...]

    set_a, set_b = ((a0, a1), sem_a), ((b0, b1), sem_b)

    @pl.when(step == 0)
    def _():
        request(0, *set_a)

    for parity, mine, other in ((0, set_a, set_b), (1, set_b, set_a)):
        @pl.when(step % 2 == parity)
        def _():
            @pl.when(step + 1 < n_steps)
            def _():
                request(step + 1, *other)

            finish(*mine)


def _combine(pos, h, rec, ys):
    n, d = h.shape
    t = MOE_TILE
    grid_spec = pltpu.PrefetchScalarGridSpec(
        num_scalar_prefetch=1,
        grid=(n // t,),
        in_specs=[pl.BlockSpec((t, d), lambda i, pos: (i, 0)),
                  pl.BlockSpec((t, LANE), lambda i, pos: (i, 0)),
                  pl.BlockSpec(memory_space=pl.ANY)],
        out_specs=pl.BlockSpec((t, d), lambda i, pos: (i, 0)),
        scratch_shapes=[pltpu.VMEM((t, d), F32) for _ in range(4)]
        + [pltpu.SemaphoreType.DMA, pltpu.SemaphoreType.DMA],
    )
    return pl.pallas_call(
        _combine_kernel,
        out_shape=jax.ShapeDtypeStruct((n, d), F32),
        grid_spec=grid_spec,
        compiler_params=_cparams(1),
        name="moe_combine",
    )(pos, h, rec, ys)


def _norm_kernel(x_ref, tail_ref, g_ref, o_ref):
    tm = x_ref.shape[1]
    o_ref[0, 0:tm - N_META, :] = _rms(x_ref[0, N_META:tm, :], g_ref[...])
    o_ref[0, tm - N_META:tm, :] = _rms(tail_ref[0], g_ref[...])


def _final_norm(h3, g, *, t_len, tm=1024):
    b, lp, d = h3.shape
    assert t_len % tm == 0 and tm % N_META == 0
    return pl.pallas_call(
        _norm_kernel,
        out_shape=jax.ShapeDtypeStruct((b, t_len, d), F32),
        grid=(b, t_len // tm),
        in_specs=[pl.BlockSpec((1, tm, d), lambda bi, j: (bi, j, 0)),
                  pl.BlockSpec((1, N_META, d), lambda bi, j: (bi, (j + 1) * (tm // N_META), 0)),
                  pl.BlockSpec((1, d), lambda bi, j: (0, 0))],
        out_specs=pl.BlockSpec((1, tm, d), lambda bi, j: (bi, j, 0)),
        compiler_params=_cparams(2),
        name="final_norm",
    )(h3, h3, g.reshape(1, d))


def _fft_row_tile(lp):
    best = 16
    for tk in range(16, lp + 1, 16):
        if lp % tk == 0 and tk * lp * 2 <= 4608 * 1024:
            best = tk
    return best


def _trunk(x, meta_tokens, prm):
    b, t_len, d = x.shape
    depth = len(prm["w_in"])
    seq_len = N_META + t_len
    lp = -(-seq_len // LANE) * LANE
    n = b * lp
    rows = t_len // GRID_W
    assert t_len % GRID_W == 0 and rows >= NA_ROWS and n % MOE_TILE == 0

    meta = jnp.broadcast_to(meta_tokens.astype(x.dtype)[None], (b, N_META, d))
    h = jnp.concatenate([meta, x, jnp.zeros((b, lp - seq_len, d), x.dtype)], axis=1).reshape(n, d)

    cmat, smat = _twiddles(seq_len, lp)
    tk = _fft_row_tile(lp)
    tri = jnp.tril(jnp.ones((MOE_TILE, MOE_TILE), BF16), -1)

    for i in range(depth):
        z, ab = _norm_matmul(h, prm["norm_mix"][i], prm["w_in"][i], prm["chan_dft"])
        z3 = z.reshape(b, lp, D_IN)
        a = _attention(z3, prm["attn_bias"][i], prm["attn_mbias"][i], rows=rows, seq_len=seq_len)
        f = _fourier(cmat, smat, ab.reshape(b, lp, 2 * D_FFT), prm["w_fft"][i], tk=tk)
        p = _pool(z3, prm["w_pool"][i], prm["pool_scale"][i], seq_len=seq_len)
        h = _outproj(a.reshape(n, D_ATT), f.reshape(n, D_FFT), p.reshape(n, D_POOL), h,
                     prm["norm_groups"][i], prm["w_out"][i])
        j = i // 2
        if i % 2 == 0:
            h = _ffn(h, prm["norm_ffn"][i], prm["w_ff_gate"][j], prm["w_ff_up"][j], prm["w_ff_down"][j])
        else:
            xn, rec, counts = _router(h, prm["norm_ffn"][i], prm["w_router"][j], tri,
                                      seq_len=seq_len, lp=lp)
            n_row_tiles = -(-2 * b * seq_len // EXPERT_ROWS) + N_EXPERTS
            pad_token = (np.arange(n) % lp) >= seq_len
            pos_out, pos_back, tile_expert, tile_live, zero_tiles = _dispatch_plan(
                rec, counts, pad_token, n_row_tiles=n_row_tiles)
            n_spill = 2 * int(pad_token.sum())
            xs = _dispatch(pos_out, zero_tiles, xn, n_row_tiles * EXPERT_ROWS + n_spill)
            ys = _experts(tile_expert, tile_live, xs, n_row_tiles, j,
                          prm["w_exp_gate"], prm["w_exp_up"], prm["w_exp_down"])
            h = _combine(pos_back, h, rec, ys)
    return _final_norm(h.reshape(b, lp, d), prm["norm_final"], t_len=t_len)


def kernel(x_prompt, x_sample, meta_tokens, norm_mix, w_in, w_fft, w_pool, pool_scale, rel_bias, meta_bias,
           norm_groups, w_out, norm_ffn, w_ff_gate, w_ff_up, w_ff_down, w_router, w_exp_gate, w_exp_up,
           w_exp_down, norm_final):
    depth = w_in.shape[0]
    tabs = [_attn_bias_tables(rel_bias[i], meta_bias[i]) for i in range(depth)]
    key_scale = np.ones((D_IN,), np.float32)
    key_scale[D_ATT:2 * D_ATT] = LOG2E
    per_layer = lambda w: [w[i].astype(BF16) for i in range(w.shape[0])]
    prm = {
        "norm_mix": norm_mix, "norm_groups": norm_groups, "norm_ffn": norm_ffn, "norm_final": norm_final,
        "pool_scale": pool_scale,
        "w_in": per_layer(w_in * key_scale), "w_out": per_layer(w_out),
        "w_ff_gate": per_layer(w_ff_gate), "w_ff_up": per_layer(w_ff_up), "w_ff_down": per_layer(w_ff_down),
        "w_exp_gate": w_exp_gate.astype(BF16), "w_exp_up": w_exp_up.astype(BF16),
        "w_exp_down": w_exp_down.astype(BF16),
        "w_router": jnp.pad(w_router, ((0, 0), (0, 0), (0, LANE - N_EXPERTS))),
        "w_fft": jnp.stack([_block_diag(w_fft[i]) for i in range(depth)]).astype(BF16),
        "w_pool": jnp.stack([_block_diag(w_pool[i]) for i in range(depth)]).astype(BF16),
        "chan_dft": _channel_dft_matrix(),
        "attn_bias": [tb[0] for tb in tabs], "attn_mbias": [tb[1] for tb in tabs],
    }
    y_prompt = _trunk(x_prompt, meta_tokens, prm)
    y_sample = _trunk(x_sample, meta_tokens, prm)
    return (y_prompt, y_sample)
```

```python
import functools
import math

import numpy as np
import jax
import jax.numpy as jnp
from jax import lax
from jax.experimental import pallas as pl
from jax.experimental.pallas import tpu as pltpu

D_MODEL = 1024
N_META = 16
GRID_W = 64
D_HEAD = 64
D_ATT = 512
N_ATT_HEADS = 8
D_FFT = 256
N_FFT_HEADS = 4
D_FFT_HEAD = 64
D_POOL = 256
POOL_WINDOWS = (2, 4, 8, 16)
D_POOL_GROUP = 64
D_IN = 2048
NA_ROWS = 8
NA_COLS = 16
N_EXPERTS = 8
EPS = 1e-6

LANE = 128
NEG_BIG = -1e30
LOG2E = math.log2(math.e)
VMEM_LIMIT = 56 * 1024 * 1024

F32 = jnp.float32
BF16 = jnp.bfloat16


def _cparams(n_axes, vmem=VMEM_LIMIT):
    return pltpu.CompilerParams(dimension_semantics=("arbitrary",) * n_axes,
                                vmem_limit_bytes=vmem)


def _rms(x, g):
    return x * lax.rsqrt(jnp.mean(x * x, axis=-1, keepdims=True) + EPS) * g


def _norm_matmul_kernel(x_ref, g_ref, w_ref, cdft_ref, o_ref, ab_ref, xn_ref, *, tn, fft_col):
    xn_ref[...] = _rms(x_ref[...], g_ref[...]).astype(BF16)
    for n0 in range(0, o_ref.shape[1], tn):
        zc = jnp.dot(xn_ref[...], w_ref[:, n0:n0 + tn], preferred_element_type=F32).astype(o_ref.dtype)
        o_ref[:, n0:n0 + tn] = zc
        if n0 <= fft_col and fft_col + D_FFT <= n0 + tn:
            ab_ref[...] = jnp.dot(zc[:, fft_col - n0:fft_col - n0 + D_FFT], cdft_ref[...],
                                  preferred_element_type=F32).astype(ab_ref.dtype)


def _norm_matmul(x, g, w, cdft, *, tm=1024, tn=512):
    n, d = x.shape
    n_out = w.shape[1]
    fft_col = 3 * D_ATT
    assert fft_col % tn + D_FFT <= tn
    return pl.pallas_call(
        functools.partial(_norm_matmul_kernel, tn=tn, fft_col=fft_col),
        out_shape=(jax.ShapeDtypeStruct((n, n_out), BF16), jax.ShapeDtypeStruct((n, 2 * D_FFT), BF16)),
        grid=(n // tm,),
        in_specs=[pl.BlockSpec((tm, d), lambda i: (i, 0)),
                  pl.BlockSpec((1, d), lambda i: (0, 0)),
                  pl.BlockSpec((d, n_out), lambda i: (0, 0)),
                  pl.BlockSpec((D_FFT, 2 * D_FFT), lambda i: (0, 0))],
        out_specs=(pl.BlockSpec((tm, n_out), lambda i: (i, 0)),
                   pl.BlockSpec((tm, 2 * D_FFT), lambda i: (i, 0))),
        scratch_shapes=[pltpu.VMEM((tm, d), BF16)],
        compiler_params=_cparams(1),
        name="norm_matmul",
    )(x, g.reshape(1, d), w, cdft)


def _attn_kernel(q_ref, k_ref, v_ref, bias_ref, mb_ref, o_ref, smq_ref, s_a, s_b, s_c, s_d,
                 p_a, p_b, p_c, p_d, l_a, l_b, l_c, l_d, *, rows, seq_len):
    lp = o_ref.shape[1]
    n_win = NA_ROWS * GRID_W
    lane = lax.broadcasted_iota(jnp.int32, (1, LANE), 1)
    head0 = lane < D_HEAD
    nt = (((1,), (1,)), ((), ()))

    km16 = k_ref[0, 0:N_META, :]
    zero16 = jnp.zeros_like(km16)
    km = jnp.concatenate([jnp.where(head0, km16, zero16), jnp.where(head0, zero16, km16),
                          k_ref[0, 2 * N_META:LANE, :]], axis=0)
    vm = jnp.concatenate([v_ref[0, 0:N_META, :], v_ref[0, 0:N_META, :], v_ref[0, 2 * N_META:LANE, :]], axis=0)
    mb = mb_ref[0]

    def stack(q):
        q = q * jnp.asarray(D_HEAD ** -0.5, q.dtype)
        zero = jnp.zeros_like(q)
        return jnp.concatenate([jnp.where(head0, q, zero), jnp.where(head0, zero, q)], axis=0)

    def unstack(o, n):
        return jnp.where(head0, o[0:n], o[n:2 * n])

    q_all = q_ref[0] * jnp.asarray(D_HEAD ** -0.5, q_ref.dtype)
    smq_ref[...] = lax.dot_general(q_all, km, nt, preferred_element_type=F32)

    def meta_scores(q0, n, bias):
        blk = smq_ref[pl.ds(q0, n), :]
        return jnp.concatenate([blk, blk], axis=0) + bias

    sm = meta_scores(0, N_META, jnp.concatenate([mb[0:N_META], mb[GRID_W:GRID_W + N_META]], axis=0))
    pm = jnp.exp2(sm - jnp.max(sm, axis=-1, keepdims=True))
    om = jnp.dot(pm.astype(BF16), vm, preferred_element_type=F32)
    om = om / jnp.sum(pm, axis=-1, keepdims=True)
    o_ref[0, 0:N_META, :] = unstack(om, N_META).astype(o_ref.dtype)

    def window(t):
        t = jnp.minimum(t, rows - 1)
        rs = jnp.clip(t - NA_ROWS // 2, 0, rows - NA_ROWS)
        q0 = pl.multiple_of(N_META + t * GRID_W, 16)
        k0 = pl.multiple_of(N_META + rs * GRID_W, 16)
        return q0, k0, t - rs

    def scores(t, s_ref):
        q0, k0, off = window(t)
        qs = stack(q_ref[0, pl.ds(q0, GRID_W), :])
        kw = k_ref[0, pl.ds(k0, n_win), :]
        s_ref[...] = lax.dot_general(qs, kw, nt, preferred_element_type=F32) + bias_ref[0, off]

    dyn_zero = pl.multiple_of(jnp.minimum(pl.program_id(0), 0), LANE)

    def lane_tiles(x):
        return [x[:, i:i + LANE] for i in range(0, x.shape[1], LANE)]

    def softmax(t, s_ref, p_ref, l_ref):
        q0, _, _ = window(t)
        s = s_ref[pl.ds(dyn_zero, LANE), :]
        sm = meta_scores(q0, GRID_W, mb)
        m = jnp.max(functools.reduce(jnp.maximum, lane_tiles(s) + [sm]), axis=-1, keepdims=True)
        p = jnp.exp2(s - m)
        pm = jnp.exp2(sm - m)
        l_ref[...] = jnp.sum(functools.reduce(jnp.add, lane_tiles(p) + [pm]), axis=-1, keepdims=True)
        p_ref[:, 0:n_win] = p.astype(BF16)
        p_ref[:, n_win:] = pm.astype(BF16)

    def values(t, p_ref, l_ref):
        q0, k0, _ = window(t)
        vw = v_ref[0, pl.ds(k0, n_win), :]
        o = (jnp.dot(p_ref[:, 0:n_win], vw, preferred_element_type=F32)
             + jnp.dot(p_ref[:, n_win:], vm, preferred_element_type=F32))
        o = o / l_ref[...]
        o_ref[0, pl.ds(q0, GRID_W), :] = unstack(o, GRID_W).astype(o_ref.dtype)

    def step(t, s_in, s_out, pl_in, pl_out):
        scores(t + 4, s_out[0])
        scores(t + 5, s_out[1])
        values(t, *pl_in[0])
        values(t + 1, *pl_in[1])
        softmax(t + 2, s_in[0], *pl_out[0])
        softmax(t + 3, s_in[1], *pl_out[1])

    set_0 = ((p_a, l_a), (p_b, l_b))
    set_1 = ((p_c, l_c), (p_d, l_d))
    scores(0, s_c)
    scores(1, s_d)
    softmax(0, s_c, *set_0[0])
    softmax(1, s_d, *set_0[1])
    scores(2, s_a)
    scores(3, s_b)

    def quad_body(u, carry):
        t = 4 * u
        step(t, (s_a, s_b), (s_c, s_d), set_0, set_1)
        step(t + 2, (s_c, s_d), (s_a, s_b), set_1, set_0)
        return carry

    lax.fori_loop(0, rows // 4, quad_body, 0, unroll=4)
    if lp > seq_len:
        o_ref[0, seq_len:lp, :] = jnp.zeros((lp - seq_len, LANE), o_ref.dtype)


def _attention(z3, bias_tab, mb_tab, *, rows, seq_len):
    b, lp, _ = z3.shape
    assert rows % 4 == 0
    n_pairs = N_ATT_HEADS // 2
    n_keys = NA_ROWS * GRID_W
    blk = lambda off: pl.BlockSpec((1, lp, LANE), lambda hp, bi: (bi, 0, off + hp))
    scratch = ([pltpu.VMEM((lp, LANE), F32)]
               + [pltpu.VMEM((LANE, n_keys), F32) for _ in range(4)]
               + [pltpu.VMEM((LANE, n_keys + LANE), BF16) for _ in range(4)]
               + [pltpu.VMEM((LANE, 1), F32) for _ in range(4)])
    return pl.pallas_call(
        functools.partial(_attn_kernel, rows=rows, seq_len=seq_len),
        out_shape=jax.ShapeDtypeStruct((b, lp, D_ATT), BF16),
        grid=(n_pairs, b),
        in_specs=[blk(0), blk(n_pairs), blk(2 * n_pairs),
                  pl.BlockSpec((1, NA_ROWS, LANE, n_keys), lambda hp, bi: (hp, 0, 0, 0)),
                  pl.BlockSpec((1, LANE, LANE), lambda hp, bi: (hp, 0, 0))],
        out_specs=pl.BlockSpec((1, lp, LANE), lambda hp, bi: (bi, 0, hp)),
        scratch_shapes=scratch,
        compiler_params=_cparams(2),
        name="nbr_attention",
    )(z3, z3, z3, bias_tab, mb_tab)


def _attn_bias_tables(rel_bias, meta_bias):
    h, n_dr, n_dc = rel_bias.shape
    c = np.arange(GRID_W)[:, None]
    j = np.arange(GRID_W)[None, :]
    cs = np.clip(c - NA_COLS // 2, 0, GRID_W - NA_COLS)
    valid = (j >= cs) & (j < cs + NA_COLS)
    ext = jnp.zeros((h, n_dr, 2 * GRID_W), F32)
    ext = lax.dynamic_update_slice(ext, rel_bias.astype(F32), (0, 0, GRID_W - NA_COLS))
    skew = jnp.tile(ext, (1, 1, GRID_W))[:, :, :GRID_W * (2 * GRID_W - 1)]
    toep = skew.reshape(h, n_dr, GRID_W, 2 * GRID_W - 1)[..., GRID_W - 1:]
    toep = jnp.where(valid[None, None], toep * LOG2E, NEG_BIG)
    t = jnp.stack([toep[:, NA_ROWS - 1 - oi:2 * NA_ROWS - 1 - oi] for oi in range(NA_ROWS)], axis=1)
    t = t.transpose(0, 1, 3, 2, 4)
    t = t.reshape(h // 2, 2, NA_ROWS, GRID_W, NA_ROWS * GRID_W)
    t = t.transpose(0, 2, 1, 3, 4).reshape(h // 2, NA_ROWS, 2 * GRID_W, NA_ROWS * GRID_W)
    mbp = (meta_bias.astype(F32) * LOG2E).reshape(h // 2, 2, N_META)
    neg = jnp.full((h // 2, N_META), NEG_BIG, F32)
    mb0 = jnp.concatenate([mbp[:, 0], neg], axis=-1)
    mb1 = jnp.concatenate([neg, mbp[:, 1]], axis=-1)
    mb = jnp.stack([mb0, mb1], axis=1)
    mb = jnp.pad(mb, ((0, 0), (0, 0), (0, LANE - 2 * N_META)), constant_values=NEG_BIG)
    mb = jnp.broadcast_to(mb[:, :, None, :], (h // 2, 2, GRID_W, LANE))
    return t, mb.reshape(h // 2, 2 * GRID_W, LANE)


def _fft_kernel(c_ref, s_ref, ab_ref, w_ref, o_ref):
    a = ab_ref[0, :, 0:D_FFT]
    b = ab_ref[0, :, D_FFT:2 * D_FFT]
    f = (jnp.dot(c_ref[...], a, preferred_element_type=F32)
         + jnp.dot(s_ref[...], b, preferred_element_type=F32))
    o_ref[0] = jnp.dot(f.astype(BF16), w_ref[...], preferred_element_type=F32).astype(o_ref.dtype)


def _fourier(cmat, smat, ab3, w_bd, *, tk):
    b, lp, _ = ab3.shape
    return pl.pallas_call(
        _fft_kernel,
        out_shape=jax.ShapeDtypeStruct((b, lp, D_FFT), BF16),
        grid=(lp // tk, b),
        in_specs=[pl.BlockSpec((tk, lp), lambda j, bi: (j, 0)),
                  pl.BlockSpec((tk, lp), lambda j, bi: (j, 0)),
                  pl.BlockSpec((1, lp, 2 * D_FFT), lambda j, bi: (bi, 0, 0)),
                  pl.BlockSpec((D_FFT, D_FFT), lambda j, bi: (0, 0))],
        out_specs=pl.BlockSpec((1, tk, D_FFT), lambda j, bi: (bi, j, 0)),
        compiler_params=_cparams(2),
        name="fourier_mix",
    )(cmat, smat, ab3, w_bd)


def _twiddles(seq_len, lp):
    theta = 2.0 * math.pi / seq_len
    k = jnp.arange(lp, dtype=jnp.int32)[:, None]
    t1 = (GRID_W * jnp.arange(lp // GRID_W, dtype=jnp.int32))[None, :]
    t0 = jnp.arange(GRID_W, dtype=jnp.int32)[None, :]
    ang_a = ((k * t1) % seq_len).astype(F32) * theta
    ang_b = ((k * t0) % seq_len).astype(F32) * theta
    ca, sa = jnp.cos(ang_a)[:, :, None], jnp.sin(ang_a)[:, :, None]
    cb, sb = jnp.cos(ang_b)[:, None, :], jnp.sin(ang_b)[:, None, :]
    cmat = (ca * cb - sa * sb).reshape(lp, lp)
    smat = (sa * cb + ca * sb).reshape(lp, lp)
    idx = jnp.arange(lp)
    valid = (idx[:, None] < seq_len) & (idx[None, :] < seq_len)
    scale = seq_len ** -0.5
    cmat = jnp.where(valid, cmat * scale, 0.0).astype(BF16)
    smat = jnp.where(valid, -smat * scale, 0.0).astype(BF16)
    return cmat, smat


def _channel_dft_matrix():
    c = np.arange(D_FFT_HEAD)
    ang = 2.0 * np.pi * ((c[:, None] * c[None, :]) % D_FFT_HEAD) / D_FFT_HEAD
    cc = np.cos(ang) / math.sqrt(D_FFT_HEAD)
    sc = np.sin(ang) / math.sqrt(D_FFT_HEAD)
    out = np.zeros((D_FFT, 2 * D_FFT), np.float32)
    for g in range(N_FFT_HEADS):
        sl = slice(g * D_FFT_HEAD, (g + 1) * D_FFT_HEAD)
        out[sl, sl] = cc
        out[sl, D_FFT + g * D_FFT_HEAD:D_FFT + (g + 1) * D_FFT_HEAD] = sc
    return jnp.asarray(out, BF16)


def _block_diag(w):
    g, c, e = w.shape
    out = jnp.zeros((g * c, g * e), w.dtype)
    for i in range(g):
        out = lax.dynamic_update_slice(out, w[i], (i * c, i * e))
    return out


POOL_PAD = 16


def _pool_kernel(u_ref, w_ref, sc_ref, o_ref, s0, s1, s2, s3, s4, *, seq_len):
    lp = o_ref.shape[1]
    r_tot = lp + 2 * POOL_PAD
    lo, hi = 8, r_tot - 8
    row = lax.broadcasted_iota(jnp.int32, (lp, 1), 0)
    live = row < seq_len
    x = jnp.where(live, u_ref[0].astype(F32), 0.0)

    zeros_pad = jnp.zeros((POOL_PAD, D_POOL), F32)
    s0[0:POOL_PAD, :] = zeros_pad
    s0[POOL_PAD:POOL_PAD + lp, :] = x
    s0[POOL_PAD + lp:r_tot, :] = zeros_pad
    for s in (s1, s2, s3, s4):
        s[0:POOL_PAD, :] = zeros_pad
        s[POOL_PAD + lp:r_tot, :] = zeros_pad
    s1[lo:hi, :] = s0[lo - 1:hi - 1, :] + s0[lo:hi, :]
    s2[lo:hi, :] = s1[lo - 1:hi - 1, :] + s1[lo + 1:hi + 1, :]
    s3[lo:hi, :] = s2[lo - 2:hi - 2, :] + s2[lo + 2:hi + 2, :]
    s4[lo:hi, :] = s3[lo - 4:hi - 4, :] + s3[lo + 4:hi + 4, :]

    lane = lax.broadcasted_iota(jnp.int32, (1, D_POOL), 1)
    group = jnp.right_shift(lane, 6)
    half = jnp.where(group == 0, 1, jnp.where(group == 1, 2, jnp.where(group == 2, 4, 8)))
    cnt = jnp.minimum(row + half, seq_len) - jnp.maximum(row - half, 0)
    cnt = jnp.maximum(cnt, 1).astype(F32)
    sl = slice(POOL_PAD, POOL_PAD + lp)
    wsum = jnp.where(group == 0, s1[sl, :],
                     jnp.where(group == 1, s2[sl, :], jnp.where(group == 2, s3[sl, :], s4[sl, :])))
    p = jnp.where(live, wsum / cnt - x, 0.0)
    y = jnp.dot(p.astype(BF16), w_ref[...], preferred_element_type=F32) * sc_ref[...]
    o_ref[0] = y.astype(o_ref.dtype)


def _pool(z3, w_bd, scale, *, seq_len):
    b, lp, d_in = z3.shape
    col_block = (d_in - D_POOL) // D_POOL
    scratch = [pltpu.VMEM((lp + 2 * POOL_PAD, D_POOL), F32) for _ in range(5)]
    return pl.pallas_call(
        functools.partial(_pool_kernel, seq_len=seq_len),
        out_shape=jax.ShapeDtypeStruct((b, lp, D_POOL), BF16),
        grid=(b,),
        in_specs=[pl.BlockSpec((1, lp, D_POOL), lambda bi: (bi, 0, col_block)),
                  pl.BlockSpec((D_POOL, D_POOL), lambda bi: (0, 0)),
                  pl.BlockSpec((1, D_POOL), lambda bi: (0, 0))],
        out_specs=pl.BlockSpec((1, lp, D_POOL), lambda bi: (bi, 0, 0)),
        scratch_shapes=scratch,
        compiler_params=_cparams(1),
        name="pool_mix",
    )(z3, w_bd, scale.reshape(1, D_POOL))


def _mix_project(a_ref, f_ref, p_ref, g_ref, w_ref):
    an = _rms(a_ref[...].astype(F32), g_ref[:, 0:D_ATT]).astype(BF16)
    fn = _rms(f_ref[...].astype(F32), g_ref[:, D_ATT:D_ATT + D_FFT]).astype(BF16)
    pn = _rms(p_ref[...].astype(F32), g_ref[:, D_ATT + D_FFT:]).astype(BF16)
    acc = jnp.dot(an, w_ref[0:D_ATT, :], preferred_element_type=F32)
    acc += jnp.dot(fn, w_ref[D_ATT:D_ATT + D_FFT, :], preferred_element_type=F32)
    acc += jnp.dot(pn, w_ref[D_ATT + D_FFT:, :], preferred_element_type=F32)
    return acc


def _outproj_kernel(a_ref, f_ref, p_ref, h_ref, g_ref, w_ref, o_ref):
    o_ref[...] = h_ref[...] + _mix_project(a_ref, f_ref, p_ref, g_ref, w_ref)


def _outproj(a, f, p, h, g, w, *, tm=1024):
    n, d = h.shape
    row = lambda width: pl.BlockSpec((tm, width), lambda i: (i, 0))
    return pl.pallas_call(
        _outproj_kernel,
        out_shape=jax.ShapeDtypeStruct((n, d), F32),
        grid=(n // tm,),
        in_specs=[row(D_ATT), row(D_FFT), row(D_POOL), row(d),
                  pl.BlockSpec((1, d), lambda i: (0, 0)),
                  pl.BlockSpec((d, d), lambda i: (0, 0))],
        out_specs=row(d),
        compiler_params=_cparams(1),
        name="out_proj",
    )(a, f, p, h, g.reshape(1, d), w)


def _ffn_kernel(h_ref, g_ref, wg_ref, wu_ref, wd_ref, o_ref, xn_ref, acc_ref):
    c = pl.program_id(1)

    @pl.when(c == 0)
    def _():
        xn_ref[...] = _rms(h_ref[...], g_ref[...]).astype(BF16)

    xn = xn_ref[...]
    gate = jnp.dot(xn, wg_ref[...], preferred_element_type=F32)
    up = jnp.dot(xn, wu_ref[...], preferred_element_type=F32)
    hh = (gate * jax.nn.sigmoid(gate) * up).astype(BF16)
    part = jnp.dot(hh, wd_ref[...], preferred_element_type=F32)

    @pl.when(c == 0)
    def _():
        acc_ref[...] = part

    @pl.when(c > 0)
    def _():
        acc_ref[...] += part

    @pl.when(c == pl.num_programs(1) - 1)
    def _():
        o_ref[...] = h_ref[...] + acc_ref[...]


def _ffn(h, g, wg, wu, wd, *, tm=1024, n_chunks=2):
    n, d = h.shape
    d_ff = wg.shape[1]
    fc = d_ff // n_chunks
    return pl.pallas_call(
        _ffn_kernel,
        out_shape=jax.ShapeDtypeStruct((n, d), F32),
        grid=(n // tm, n_chunks),
        in_specs=[pl.BlockSpec((tm, d), lambda i, c: (i, 0)),
                  pl.BlockSpec((1, d), lambda i, c: (0, 0)),
                  pl.BlockSpec((d, fc), lambda i, c: (0, c)),
                  pl.BlockSpec((d, fc), lambda i, c: (0, c)),
                  pl.BlockSpec((fc, d), lambda i, c: (c, 0))],
        out_specs=pl.BlockSpec((tm, d), lambda i, c: (i, 0)),
        scratch_shapes=[pltpu.VMEM((tm, d), BF16), pltpu.VMEM((tm, d), F32)],
        compiler_params=_cparams(2),
        name="swiglu_ffn",
    )(h, g.reshape(1, d), wg, wu, wd)


MOE_TILE = 1024
EXPERT_ROWS = 512
R_GATE, R_EXPERT, R_RANK = 0, 2, 4


def _router_kernel(h_ref, g_ref, wr_ref, tri_ref, xn_ref, rec_ref, cnt_ref, *, seq_len, lp):
    t = h_ref.shape[0]
    y = _rms(h_ref[...], g_ref[...])
    xn_ref[...] = y
    logits = jnp.dot(y, wr_ref[...], preferred_element_type=F32, precision=lax.Precision.HIGHEST)
    lane = lax.broadcasted_iota(jnp.int32, (t, LANE), 1)
    lg = jnp.where(lane < N_EXPERTS, logits, -jnp.inf)
    m1 = jnp.max(lg, axis=-1, keepdims=True)
    i1 = jnp.min(jnp.where(lg == m1, lane, LANE), axis=-1, keepdims=True)
    lg2 = jnp.where(lane == i1, -jnp.inf, lg)
    m2 = jnp.max(lg2, axis=-1, keepdims=True)
    i2 = jnp.min(jnp.where(lg2 == m2, lane, LANE), axis=-1, keepdims=True)
    e2 = jnp.exp(m2 - m1)
    g1 = 1.0 / (1.0 + e2)
    g2 = e2 / (1.0 + e2)
    rowf = (pl.program_id(0) * t + lax.broadcasted_iota(jnp.int32, (t, 1), 0)).astype(F32)
    seq = jnp.floor((rowf + 0.5) * (1.0 / lp))
    live = (rowf - seq * lp) < seq_len
    first = lane == i1
    second = lane == i2
    member = jnp.where((first | second) & live, 1.0, 0.0)
    rank = jnp.dot(tri_ref[...], member.astype(BF16), preferred_element_type=F32)
    r1 = jnp.sum(jnp.where(first, rank, 0.0), axis=-1, keepdims=True)
    r2 = jnp.sum(jnp.where(second, rank, 0.0), axis=-1, keepdims=True)
    dead = jnp.logical_not(live)
    rec = jnp.zeros((t, LANE), F32)
    for ln, val in ((R_GATE, jnp.where(dead, 0.0, g1)), (R_GATE + 1, jnp.where(dead, 0.0, g2)),
                    (R_EXPERT, jnp.where(dead, -1.0, i1.astype(F32))),
                    (R_EXPERT + 1, jnp.where(dead, -1.0, i2.astype(F32))),
                    (R_RANK, r1), (R_RANK + 1, r2)):
        rec = jnp.where(lane == ln, val, rec)
    rec_ref[...] = rec
    cnt_ref[0] = jnp.broadcast_to(jnp.sum(member, axis=0, keepdims=True), (8, LANE))


def _router(h, g, w_router_pad, tri, *, seq_len, lp):
    n, d = h.shape
    t = MOE_TILE
    row = lambda width: pl.BlockSpec((t, width), lambda i: (i, 0))
    return pl.pallas_call(
        functools.partial(_router_kernel, seq_len=seq_len, lp=lp),
        out_shape=(jax.ShapeDtypeStruct((n, d), F32),
                   jax.ShapeDtypeStruct((n, LANE), F32),
                   jax.ShapeDtypeStruct((n // t, 8, LANE), F32)),
        grid=(n // t,),
        in_specs=[row(d),
                  pl.BlockSpec((1, d), lambda i: (0, 0)),
                  pl.BlockSpec((d, LANE), lambda i: (0, 0)),
                  pl.BlockSpec((t, t), lambda i: (0, 0))],
        out_specs=(row(d), row(LANE), pl.BlockSpec((1, 8, LANE), lambda i: (i, 0, 0))),
        compiler_params=_cparams(1),
        name="moe_router",
    )(h, g.reshape(1, d), w_router_pad, tri)


def _dispatch_plan(rec, counts, pad_token, *, n_row_tiles):
    n = rec.shape[0]
    n_tiles = counts.shape[0]
    cnt = counts[:, 0, :N_EXPERTS]
    before = jnp.cumsum(cnt, axis=0) - cnt
    total = jnp.sum(cnt, axis=0)
    padded = jnp.ceil(total / EXPERT_ROWS) * EXPERT_ROWS
    ends = jnp.cumsum(padded)
    start = ends - padded
    base = (start[None, :] + before)[:, None, :]
    pad_rank = np.cumsum(np.asarray(pad_token, np.int64)) - 1
    spill = n_row_tiles * EXPERT_ROWS + 2 * pad_rank.reshape(n_tiles, MOE_TILE)
    slots_out, slots_back = [], []
    for k in range(2):
        e = rec[:, R_EXPERT + k].reshape(n_tiles, MOE_TILE, 1)
        r = rec[:, R_RANK + k].reshape(n_tiles, MOE_TILE)
        hit = e == jnp.arange(N_EXPERTS, dtype=F32)[None, None, :]
        slot = (jnp.sum(jnp.where(hit, base, 0.0), axis=-1) + r).astype(jnp.int32)
        dead = e[..., 0] < 0
        slots_out.append(jnp.where(dead, jnp.asarray(spill + k, jnp.int32), slot))
        slots_back.append(jnp.where(dead, 0, slot))
    pos_out = jnp.stack(slots_out, axis=-1).reshape(n * 2)
    pos_back = jnp.stack(slots_back, axis=-1).reshape(n * 2)
    tile_row0 = jnp.arange(n_row_tiles, dtype=F32) * EXPERT_ROWS
    tile_expert = jnp.sum(tile_row0[:, None] >= ends[None, :], axis=-1)
    tile_live = (tile_row0 < ends[-1]).astype(jnp.int32)
    tile_expert = jnp.minimum(tile_expert, N_EXPERTS - 1).astype(jnp.int32)
    last_tile = jnp.where(padded > 0, ends / EXPERT_ROWS - 1, -1.0)
    unused = ends[-1] / EXPERT_ROWS + jnp.arange(N_EXPERTS, dtype=F32)
    unused = jnp.where(unused < n_row_tiles, unused, -1.0)
    zero_tiles = jnp.concatenate([last_tile, unused]).astype(jnp.int32)
    return pos_out, pos_back, tile_expert, tile_live, zero_tiles


ROW_GROUP = 8


def _row_copy(src_ref, src_row, dst_ref, dst_row, sem):
    return pltpu.make_async_copy(src_ref.at[pl.ds(src_row, 1)], dst_ref.at[pl.ds(dst_row, 1)], sem)


def _dispatch_kernel(pos_ref, zt_ref, x_ref, xs_ref, zero_buf, sem, zero_sem):
    t = x_ref.shape[0]
    base = pl.program_id(0) * t * 2

    @pl.when(pl.program_id(0) == 0)
    def _():
        zero_buf[...] = jnp.zeros(zero_buf.shape, zero_buf.dtype)

        def tile_copy(tix):
            row0 = pl.multiple_of(tix * EXPERT_ROWS, EXPERT_ROWS)
            return pltpu.make_async_copy(zero_buf, xs_ref.at[pl.ds(row0, EXPERT_ROWS)], zero_sem)

        for j in range(zt_ref.shape[0]):
            @pl.when(zt_ref[j] >= 0)
            def _():
                tile_copy(zt_ref[j]).start()

        for j in range(zt_ref.shape[0]):
            @pl.when(zt_ref[j] >= 0)
            def _():
                tile_copy(zt_ref[j]).wait()

    def issue(g, carry):
        row0 = pl.multiple_of(g * ROW_GROUP, ROW_GROUP)
        group = x_ref.at[pl.ds(row0, ROW_GROUP)]
        idx0 = base + 2 * row0
        for c in range(ROW_GROUP):
            for k in range(2):
                p = pos_ref[idx0 + (2 * c + k)]
                _row_copy(group, c, xs_ref, p, sem).start(priority=k)
        return carry

    def drain(g, carry):
        for _ in range(2 * ROW_GROUP):
            _row_copy(x_ref, 0, xs_ref, 0, sem).wait()
        return carry

    lax.fori_loop(0, t // ROW_GROUP, issue, 0)
    lax.fori_loop(0, t // ROW_GROUP, drain, 0)


def _dispatch(pos, zero_tiles, xn, n_rows):
    n, d = xn.shape
    t = MOE_TILE
    grid_spec = pltpu.PrefetchScalarGridSpec(
        num_scalar_prefetch=2,
        grid=(n // t,),
        in_specs=[pl.BlockSpec((t, d), lambda i, pos, zt: (i, 0))],
        out_specs=pl.BlockSpec(memory_space=pl.ANY),
        scratch_shapes=[pltpu.VMEM((EXPERT_ROWS, d), F32), pltpu.SemaphoreType.DMA, pltpu.SemaphoreType.DMA],
    )
    return pl.pallas_call(
        _dispatch_kernel,
        out_shape=jax.ShapeDtypeStruct((n_rows, d), F32),
        grid_spec=grid_spec,
        compiler_params=_cparams(1),
        name="moe_dispatch",
    )(pos, zero_tiles, xn)


def _expert_kernel(te_ref, live_ref, x_ref, wg_ref, wu_ref, wd_ref, o_ref, xb_ref, acc_ref):
    del te_ref
    i, c = pl.program_id(0), pl.program_id(1)
    last = pl.num_programs(1) - 1
    live = live_ref[i] > 0

    @pl.when(live)
    def _():
        @pl.when(c == 0)
        def _():
            xb_ref[...] = x_ref[...].astype(BF16)

        xb = xb_ref[...]
        gate = jnp.dot(xb, wg_ref[0, 0], preferred_element_type=F32)
        up = jnp.dot(xb, wu_ref[0, 0], preferred_element_type=F32)
        hh = (gate * jax.nn.sigmoid(gate) * up).astype(BF16)
        part = jnp.dot(hh, wd_ref[0, 0], preferred_element_type=F32)

        @pl.when(c == 0)
        def _():
            acc_ref[...] = part

        @pl.when(c > 0)
        def _():
            acc_ref[...] += part

        @pl.when(c == last)
        def _():
            o_ref[...] = acc_ref[...]

    @pl.when(jnp.logical_not(live) & (c == last))
    def _():
        o_ref[...] = jnp.zeros(o_ref.shape, o_ref.dtype)


def _experts(tile_expert, tile_live, xs, n_row_tiles, layer, wg, wu, wd, *, n_chunks=2):
    d = xs.shape[1]
    tm = EXPERT_ROWS
    n_rows = n_row_tiles * tm
    fc = wg.shape[3] // n_chunks
    chunk = lambda i, c, te, lv: jnp.where(lv[i] > 0, c, n_chunks - 1)
    grid_spec = pltpu.PrefetchScalarGridSpec(
        num_scalar_prefetch=2,
        grid=(n_rows // tm, n_chunks),
        in_specs=[pl.BlockSpec((tm, d), lambda i, c, te, lv: (i, 0)),
                  pl.BlockSpec((1, 1, d, fc), lambda i, c, te, lv: (layer, te[i], 0, chunk(i, c, te, lv))),
                  pl.BlockSpec((1, 1, d, fc), lambda i, c, te, lv: (layer, te[i], 0, chunk(i, c, te, lv))),
                  pl.BlockSpec((1, 1, fc, d), lambda i, c, te, lv: (layer, te[i], chunk(i, c, te, lv), 0))],
        out_specs=pl.BlockSpec((tm, d), lambda i, c, te, lv: (i, 0)),
        scratch_shapes=[pltpu.VMEM((tm, d), BF16), pltpu.VMEM((tm, d), F32)],
    )
    return pl.pallas_call(
        _expert_kernel,
        out_shape=jax.ShapeDtypeStruct((n_rows, d), F32),
        grid_spec=grid_spec,
        compiler_params=_cparams(2),
        name="moe_experts",
    )(tile_expert, tile_live, xs, wg, wu, wd)


def _combine_kernel(pos_ref, h_ref, rec_ref, ys_ref, o_ref, a0, a1, b0, b1, sem_a, sem_b):
    t = h_ref.shape[0]
    step = pl.program_id(0)
    n_steps = pl.num_programs(0)

    def request(tile, bufs, sem):
        base = tile * t * 2

        def issue(g, carry):
            row0 = pl.multiple_of(g * ROW_GROUP, ROW_GROUP)
            groups = [buf.at[pl.ds(row0, ROW_GROUP)] for buf in bufs]
            idx0 = base + 2 * row0
            for c in range(ROW_GROUP):
                for k in range(2):
                    p = pos_ref[idx0 + (2 * c + k)]
                    _row_copy(ys_ref, p, groups[k], c, sem).start(priority=k)
            return carry

        lax.fori_loop(0, t // ROW_GROUP, issue, 0)

    def finish(bufs, sem):
        def drain(g, carry):
            for _ in range(2 * ROW_GROUP):
                _row_copy(ys_ref, 0, bufs[0], 0, sem).wait()
            return carry

        lax.fori_loop(0, t // ROW_GROUP, drain, 0)
        rec = rec_ref[...]
        g1 = rec[:, R_GATE:R_GATE + 1]
        g2 = rec[:, R_GATE + 1:R_GATE + 2]
        o_ref[...] = h_ref[...] + g1 * bufs[0][...] + g2 * bufs[1][...]

    set_a, set_b = ((a0, a1), sem_a), ((b0, b1), sem_b)

    @pl.when(step == 0)
    def _():
        request(0, *set_a)

    for parity, mine, other in ((0, set_a, set_b), (1, set_b, set_a)):
        @pl.when(step % 2 == parity)
        def _():
            @pl.when(step + 1 < n_steps)
            def _():
                request(step + 1, *other)

            finish(*mine)


def _combine(pos, h, rec, ys):
    n, d = h.shape
    t = MOE_TILE
    grid_spec = pltpu.PrefetchScalarGridSpec(
        num_scalar_prefetch=1,
        grid=(n // t,),
        in_specs=[pl.BlockSpec((t, d), lambda i, pos: (i, 0)),
                  pl.BlockSpec((t, LANE), lambda i, pos: (i, 0)),
                  pl.BlockSpec(memory_space=pl.ANY)],
        out_specs=pl.BlockSpec((t, d), lambda i, pos: (i, 0)),
        scratch_shapes=[pltpu.VMEM((t, d), F32) for _ in range(4)]
        + [pltpu.SemaphoreType.DMA, pltpu.SemaphoreType.DMA],
    )
    return pl.pallas_call(
        _combine_kernel,
        out_shape=jax.ShapeDtypeStruct((n, d), F32),
        grid_spec=grid_spec,
        compiler_params=_cparams(1),
        name="moe_combine",
    )(pos, h, rec, ys)


def _norm_kernel(x_ref, tail_ref, g_ref, o_ref):
    tm = x_ref.shape[1]
    o_ref[0, 0:tm - N_META, :] = _rms(x_ref[0, N_META:tm, :], g_ref[...])
    o_ref[0, tm - N_META:tm, :] = _rms(tail_ref[0], g_ref[...])


def _final_norm(h3, g, *, t_len, tm=1024):
    b, lp, d = h3.shape
    assert t_len % tm == 0 and tm % N_META == 0
    return pl.pallas_call(
        _norm_kernel,
        out_shape=jax.ShapeDtypeStruct((b, t_len, d), F32),
        grid=(b, t_len // tm),
        in_specs=[pl.BlockSpec((1, tm, d), lambda bi, j: (bi, j, 0)),
                  pl.BlockSpec((1, N_META, d), lambda bi, j: (bi, (j + 1) * (tm // N_META), 0)),
                  pl.BlockSpec((1, d), lambda bi, j: (0, 0))],
        out_specs=pl.BlockSpec((1, tm, d), lambda bi, j: (bi, j, 0)),
        compiler_params=_cparams(2),
        name="final_norm",
    )(h3, h3, g.reshape(1, d))


def _fft_row_tile(lp):
    best = 16
    for tk in range(16, lp + 1, 16):
        if lp % tk == 0 and tk * lp * 2 <= 9216 * 1024:
            best = tk
    return best


def _trunk(x, meta_tokens, prm):
    b, t_len, d = x.shape
    depth = len(prm["w_in"])
    seq_len = N_META + t_len
    lp = -(-seq_len // LANE) * LANE
    n = b * lp
    rows = t_len // GRID_W
    assert t_len % GRID_W == 0 and rows >= NA_ROWS and n % MOE_TILE == 0

    meta = jnp.broadcast_to(meta_tokens.astype(x.dtype)[None], (b, N_META, d))
    h = jnp.concatenate([meta, x, jnp.zeros((b, lp - seq_len, d), x.dtype)], axis=1).reshape(n, d)

    cmat, smat = _twiddles(seq_len, lp)
    tk = _fft_row_tile(lp)
    tri = jnp.tril(jnp.ones((MOE_TILE, MOE_TILE), BF16), -1)

    for i in range(depth):
        z, ab = _norm_matmul(h, prm["norm_mix"][i], prm["w_in"][i], prm["chan_dft"])
        z3 = z.reshape(b, lp, D_IN)
        a = _attention(z3, prm["attn_bias"][i], prm["attn_mbias"][i], rows=rows, seq_len=seq_len)
        f = _fourier(cmat, smat, ab.reshape(b, lp, 2 * D_FFT), prm["w_fft"][i], tk=tk)
        p = _pool(z3, prm["w_pool"][i], prm["pool_scale"][i], seq_len=seq_len)
        h = _outproj(a.reshape(n, D_ATT), f.reshape(n, D_FFT), p.reshape(n, D_POOL), h,
                     prm["norm_groups"][i], prm["w_out"][i])
        j = i // 2
        if i % 2 == 0:
            h = _ffn(h, prm["norm_ffn"][i], prm["w_ff_gate"][j], prm["w_ff_up"][j], prm["w_ff_down"][j])
        else:
            xn, rec, counts = _router(h, prm["norm_ffn"][i], prm["w_router"][j], tri,
                                      seq_len=seq_len, lp=lp)
            n_row_tiles = -(-2 * b * seq_len // EXPERT_ROWS) + N_EXPERTS
            pad_token = (np.arange(n) % lp) >= seq_len
            pos_out, pos_back, tile_expert, tile_live, zero_tiles = _dispatch_plan(
                rec, counts, pad_token, n_row_tiles=n_row_tiles)
            n_spill = 2 * int(pad_token.sum())
            xs = _dispatch(pos_out, zero_tiles, xn, n_row_tiles * EXPERT_ROWS + n_spill)
            ys = _experts(tile_expert, tile_live, xs, n_row_tiles, j,
                          prm["w_exp_gate"], prm["w_exp_up"], prm["w_exp_down"])
            h = _combine(pos_back, h, rec, ys)
    return _final_norm(h.reshape(b, lp, d), prm["norm_final"], t_len=t_len)


def kernel(x_prompt, x_sample, meta_tokens, norm_mix, w_in, w_fft, w_pool, pool_scale, rel_bias, meta_bias,
           norm_groups, w_out, norm_ffn, w_ff_gate, w_ff_up, w_ff_down, w_router, w_exp_gate, w_exp_up,
           w_exp_down, norm_final):
    depth = w_in.shape[0]
    tabs = [_attn_bias_tables(rel_bias[i], meta_bias[i]) for i in range(depth)]
    key_scale = np.ones((D_IN,), np.float32)
    key_scale[D_ATT:2 * D_ATT] = LOG2E
    per_layer = lambda w: [w[i].astype(BF16) for i in range(w.shape[0])]
    prm = {
        "norm_mix": norm_mix, "norm_groups": norm_groups, "norm_ffn": norm_ffn, "norm_final": norm_final,
        "pool_scale": pool_scale,
        "w_in": per_layer(w_in * key_scale), "w_out": per_layer(w_out),
        "w_ff_gate": per_layer(w_ff_gate), "w_ff_up": per_layer(w_ff_up), "w_ff_down": per_layer(w_ff_down),
        "w_exp_gate": w_exp_gate.astype(BF16), "w_exp_up": w_exp_up.astype(BF16),
        "w_exp_down": w_exp_down.astype(BF16),
        "w_router": jnp.pad(w_router, ((0, 0), (0, 0), (0, LANE - N_EXPERTS))),
        "w_fft": jnp.stack([_block_diag(w_fft[i]) for i in range(depth)]).astype(BF16),
        "w_pool": jnp.stack([_block_diag(w_pool[i]) for i in range(depth)]).astype(BF16),
        "chan_dft": _channel_dft_matrix(),
        "attn_bias": [tb[0] for tb in tabs], "attn_mbias": [tb[1] for tb in tabs],
    }
    y_prompt = _trunk(x_prompt, meta_tokens, prm)
    y_sample = _trunk(x_sample, meta_tokens, prm)
    return (y_prompt, y_sample)
```
